```python
import math
import jax, jax.numpy as jnp
from jax import lax
import numpy as np

D_MODEL = 1024
BATCH = 16
SEQ = 2048
DEPTH = 2

N_MIXERS = 2
N_HEADS = 16
HEAD_DIM = D_MODEL // N_HEADS
Q_BLOCK = 128
SSM_GROUP = 16
N_GROUPS = D_MODEL // SSM_GROUP
STATE = 64
D_FF = ((8 * D_MODEL // 3 + 127) // 128) * 128
CONV_W = 3
N_ATTN = (DEPTH + 1) // 2
N_SSM = DEPTH // 2
EPS = 1e-6
DT_MIN = 1e-3
DT_MAX = 1e-1

kernel_name = "hybrid_stickbreak_s5_convffn_adaln"


def rms_norm(x, g):
    xf = x.astype(jnp.float32)
    y = xf * lax.rsqrt(jnp.mean(xf * xf, axis=-1, keepdims=True) + EPS)
    return (y * g.astype(jnp.float32)).astype(x.dtype)


def modulate(h, shift, scale):
    return h * (1 + scale[:, None, :]) + shift[:, None, :]


def stick_breaking_attention(h, w_qkv, w_o):
    b, s, d = h.shape
    q, k, v = jnp.split(h @ w_qkv, 3, axis=-1)
    to_heads = lambda t: t.reshape(b, s, N_HEADS, HEAD_DIM).transpose(0, 2, 1, 3)
    q, k, v = to_heads(q), to_heads(k), to_heads(v)
    kf = k.astype(jnp.float32)
    vf = v.astype(jnp.float32)
    n_blk = s // Q_BLOCK
    qb = q.reshape(b, N_HEADS, n_blk, Q_BLOCK, HEAD_DIM).transpose(2, 0, 1, 3, 4)
    key_pos = jnp.arange(s)
    scale = HEAD_DIM ** -0.5

    def block(args):
        q_blk, blk_idx = args
        q_pos = blk_idx * Q_BLOCK + jnp.arange(Q_BLOCK)
        z = jnp.einsum('bhqd,bhkd->bhqk', q_blk.astype(jnp.float32), kf) * scale
        mask = key_pos[None, :] < q_pos[:, None]
        log_beta = jax.nn.log_sigmoid(z)
        log_1mb = jnp.where(mask, jax.nn.log_sigmoid(-z), 0.0)
        suffix = lax.cumsum(log_1mb, axis=3, reverse=True) - log_1mb
        w = jnp.where(mask, jnp.exp(log_beta + suffix), 0.0)
        o = jnp.einsum('bhqk,bhkd->bhqd', w, vf)
        return o.astype(h.dtype)

    o = lax.map(block, (qb, jnp.arange(n_blk)))
    o = o.transpose(1, 0, 3, 2, 4).reshape(b, s, d)
    return o @ w_o


def s5_ssm(h, w_in, a_re, a_im, log_dt, b_re, b_im, c_re, c_im, d_skip, w_glu, b_glu, w_o):
    b, s, d = h.shape
    u = h @ w_in
    uf = u.astype(jnp.float32)
    ug = uf.reshape(b, s, N_GROUPS, SSM_GROUP)
    lam = lax.complex(a_re.astype(jnp.float32), a_im.astype(jnp.float32))
    dt = jnp.exp(log_dt.astype(jnp.float32))[:, None]
    lam_bar = jnp.exp(lam * dt)
    b_mat = lax.complex(b_re.astype(jnp.float32), b_im.astype(jnp.float32))
    b_bar = ((lam_bar - 1) / lam)[..., None] * b_mat
    bu = jnp.einsum('gph,bsgh->bsgp', b_bar, ug.astype(jnp.complex64))
    a_seq = jnp.broadcast_to(lam_bar, (1, s, N_GROUPS, STATE))

    def combine(e1, e2):
        a1, x1 = e1
        a2, x2 = e2
        return a2 * a1, a2 * x1 + x2

    _, states = lax.associative_scan(combine, (a_seq, bu), axis=1)
    c_mat = lax.complex(c_re.astype(jnp.float32), c_im.astype(jnp.float32))
    y = jnp.einsum('ghp,bsgp->bsgh', c_mat, states).real.reshape(b, s, d)
    y = (y + d_skip.astype(jnp.float32) * uf).astype(h.dtype)
    z = jax.nn.gelu(y)
    g = z * jax.nn.sigmoid(z @ w_glu + b_glu)
    return g @ w_o


def conv_ffn(h, w_up, conv_w, conv_b, w_down):
    up = h @ w_up
    up = lax.conv_general_dilated(
        up, conv_w[:, None, :], window_strides=(1,), padding=[(CONV_W - 1, 0)],
        dimension_numbers=('NWC', 'WIO', 'NWC'), feature_group_count=2 * D_FF) + conv_b
    gate, val = jnp.split(up, 2, axis=-1)
    return (jax.nn.silu(gate) * val) @ w_down


def setup_inputs(seed: int = 0) -> dict:
    key = jax.random.key(seed)
    ks = iter(jax.random.split(key, 40))
    nrm = lambda shape, std: jax.random.normal(next(ks), shape, jnp.float32) * std
    D, G, P, H, F = D_MODEL, N_GROUPS, STATE, SSM_GROUP, D_FF
    n_idx = jnp.arange(P, dtype=jnp.float32)
    inp = {}
    inp["x"] = nrm((BATCH, SEQ, D), 1.0)
    inp["c"] = nrm((BATCH, D), 1.0)
    inp["norm_mix"] = 1.0 + nrm((DEPTH, D), 0.02)
    inp["norm_ffn"] = 1.0 + nrm((DEPTH, D), 0.02)
    inp["w_mod"] = nrm((DEPTH, D, 6 * D), 0.5 * D ** -0.5)
    inp["b_mod"] = nrm((DEPTH, 6 * D), 0.02)
    inp["w_qkv"] = nrm((N_ATTN, D, 3 * D), D ** -0.5)
    inp["w_o_attn"] = nrm((N_ATTN, D, D), D ** -0.5)
    inp["w_in_ssm"] = nrm((N_SSM, D, D), D ** -0.5)
    inp["a_re"] = -0.5 + nrm((N_SSM, G, P), 0.01)
    inp["a_im"] = math.pi * n_idx + nrm((N_SSM, G, P), 0.01)
    inp["log_dt"] = jax.random.uniform(next(ks), (N_SSM, G), jnp.float32,
                                       math.log(DT_MIN), math.log(DT_MAX))
    inp["b_re"] = nrm((N_SSM, G, P, H), (2 * H) ** -0.5)
    inp["b_im"] = nrm((N_SSM, G, P, H), (2 * H) ** -0.5)
    inp["c_re"] = nrm((N_SSM, G, H, P), (2 * P) ** -0.5 * 4.0)
    inp["c_im"] = nrm((N_SSM, G, H, P), (2 * P) ** -0.5 * 4.0)
    inp["d_skip"] = nrm((N_SSM, D), 1.0)
    inp["w_glu"] = nrm((N_SSM, D, D), D ** -0.5)
    inp["b_glu"] = nrm((N_SSM, D), 0.02)
    inp["w_o_ssm"] = nrm((N_SSM, D, D), D ** -0.5)
    inp["w_up"] = nrm((DEPTH, D, 2 * F), D ** -0.5)
    inp["conv_w"] = nrm((DEPTH, CONV_W, 2 * F), CONV_W ** -0.5)
    inp["conv_b"] = nrm((DEPTH, 2 * F), 0.02)
    inp["w_down"] = nrm((DEPTH, F, D), F ** -0.5)
    inp["norm_out"] = 1.0 + nrm((D,), 0.02)
    inp["w_fin"] = nrm((D, 2 * D), 0.5 * D ** -0.5)
    inp["b_fin"] = nrm((2 * D,), 0.02)
    return inp


def reference(x, c, norm_mix, norm_ffn, w_mod, b_mod, w_qkv, w_o_attn, w_in_ssm,
              a_re, a_im, log_dt, b_re, b_im, c_re, c_im, d_skip, w_glu, b_glu, w_o_ssm,
              w_up, conv_w, conv_b, w_down, norm_out, w_fin, b_fin):
    c_act = jax.nn.silu(c)
    for i in range(DEPTH):
        mod = c_act @ w_mod[i] + b_mod[i]
        sh1, sc1, g1, sh2, sc2, g2 = jnp.split(mod, 6, axis=-1)
        h = modulate(rms_norm(x, norm_mix[i]), sh1, sc1)
        j = i // N_MIXERS
        if i % N_MIXERS == 0:
            y = stick_breaking_attention(h, w_qkv[j], w_o_attn[j])
        else:
            y = s5_ssm(h, w_in_ssm[j], a_re[j], a_im[j], log_dt[j], b_re[j], b_im[j],
                       c_re[j], c_im[j], d_skip[j], w_glu[j], b_glu[j], w_o_ssm[j])
        x = x + g1[:, None, :] * y
        h = modulate(rms_norm(x, norm_ffn[i]), sh2, sc2)
        x = x + g2[:, None, :] * conv_ffn(h, w_up[i], conv_w[i], conv_b[i], w_down[i])
    fin = c_act @ w_fin + b_fin
    sh, sc = jnp.split(fin, 2, axis=-1)
    return modulate(rms_norm(x, norm_out), sh, sc)
```

```python
import functools

import jax
import jax.numpy as jnp
from jax import lax
from jax.experimental import pallas as pl
from jax.experimental.pallas import tpu as pltpu

F32 = jnp.float32
BF16 = jnp.bfloat16

HEAD_DIM = 64
SSM_GROUP = 16
STATE = 64
CONV_W = 3
EPS = 1e-6

LANES = 128
SUBLANES = 8
KEY_TILE = 256
KEY_ROWS = KEY_TILE // SUBLANES
HALO = 16
SSM_CHUNK = LANES
VMEM_LIMIT = 56 * 1024 * 1024

ROW_TILE = 512
SSM_TIME_TILE = 128
MOD_COL_TILE = 2048


def _params(*sem):
    return pltpu.CompilerParams(dimension_semantics=sem, vmem_limit_bytes=VMEM_LIMIT)


def _resident(shape):
    nd = len(shape)
    return pl.BlockSpec(shape, lambda *_: (0,) * nd, pipeline_mode=pl.Buffered(1))


def _modnorm(x, gain, shift, scale):
    ms = jnp.mean(x * x, axis=-1, keepdims=True)
    y = x * lax.rsqrt(ms + EPS) * gain
    return y * (1.0 + scale) + shift


def _mod_kernel(c_ref, w_ref, b_ref, o_ref):
    ca = jax.nn.silu(c_ref[...])
    o_ref[0] = jnp.dot(ca, w_ref[0], preferred_element_type=F32) + b_ref[0]


def _mod_project(c, w, b):
    nl, d, n = w.shape
    bsz = c.shape[0]
    tn = min(MOD_COL_TILE, n)
    return pl.pallas_call(
        _mod_kernel,
        grid=(nl, n // tn),
        in_specs=[
            pl.BlockSpec((bsz, d), lambda l, j: (0, 0)),
            pl.BlockSpec((1, d, tn), lambda l, j: (l, 0, j)),
            pl.BlockSpec((1, 1, tn), lambda l, j: (l, 0, j)),
        ],
        out_specs=pl.BlockSpec((1, bsz, tn), lambda l, j: (l, 0, j)),
        out_shape=jax.ShapeDtypeStruct((nl, bsz, n), F32),
        compiler_params=_params("arbitrary", "arbitrary"),
        name="adaln_project",
    )(c, w, b.reshape(nl, 1, n))


def _norm_matmul_kernel(x_ref, mod_ref, gain_ref, w_ref, *o_refs, col_chunk):
    m = mod_ref[0]
    h = _modnorm(x_ref[0], gain_ref[...], m[0:1], m[1:2]).astype(BF16)
    col = 0
    for o_ref in o_refs:
        n = o_ref.shape[-1]
        for c0 in range(0, n, col_chunk):
            y = jnp.dot(h, w_ref[:, col + c0:col + c0 + col_chunk], preferred_element_type=F32)
            if len(o_ref.shape) == 3:
                o_ref[0, :, c0:c0 + col_chunk] = y.astype(o_ref.dtype)
            else:
                o_ref[:, c0:c0 + col_chunk] = y.astype(o_ref.dtype)
        col += n


def _norm_project(x, mod, gain, w, out_dtypes, time_major):
    bsz, s, d = x.shape
    n_out = len(out_dtypes)
    n = w.shape[1] // n_out
    tm = min(ROW_TILE, s)
    if time_major:
        out_specs = [pl.BlockSpec((tm, n), lambda b, j: (j, b)) for _ in out_dtypes]
        out_shape = [jax.ShapeDtypeStruct((s, bsz * n), dt) for dt in out_dtypes]
    else:
        out_specs = [pl.BlockSpec((1, tm, n), lambda b, j: (b, j, 0)) for _ in out_dtypes]
        out_shape = [jax.ShapeDtypeStruct((bsz, s, n), dt) for dt in out_dtypes]
    return pl.pallas_call(
        functools.partial(_norm_matmul_kernel, col_chunk=min(n, 1024)),
        grid=(bsz, s // tm),
        in_specs=[
            pl.BlockSpec((1, tm, d), lambda b, j: (b, j, 0)),
            pl.BlockSpec((1,) + mod.shape[1:], lambda b, j: (b, 0, 0)),
            _resident(gain.shape),
            _resident(w.shape),
        ],
        out_specs=out_specs,
        out_shape=out_shape,
        compiler_params=_params("arbitrary", "arbitrary"),
        name="norm_project",
    )(x, mod, gain, w)


def _attn_tile(j, carry, acc, q_h, k_ref, vt_ref, s_ref, a_ref, head, row, qpos, masked):
    k0 = pl.multiple_of(j * KEY_TILE, KEY_TILE)
    k_t = k_ref[0, head, pl.ds(k0, KEY_TILE), :]
    z_t = lax.dot_general(k_t, q_h, (((1,), (1,)), ((), ())),
                          preferred_element_type=F32)
    run = jnp.ones_like(carry)
    for r in range(KEY_ROWS - 1, -1, -1):
        z = z_t[SUBLANES * r:SUBLANES * (r + 1), :]
        e = jnp.exp(-jnp.abs(z))
        rc = 1.0 / (1.0 + e)
        m = e * rc
        pos = z >= 0.0
        beta = jnp.where(pos, rc, m)
        omb = jnp.where(pos, m, rc)
        if masked:
            valid = (k0 + row * KEY_ROWS + r) < qpos
            beta = jnp.where(valid, beta, 0.0)
            omb = jnp.where(valid, omb, 1.0)
        s_ref[SUBLANES * r:SUBLANES * (r + 1), :] = beta * run
        run = run * omb
    p = run
    for d in (1, 2, 4):
        p = jnp.where(row + d < SUBLANES, p * pltpu.roll(p, SUBLANES - d, 0), p)
    excl = jnp.where(row < SUBLANES - 1, pltpu.roll(p, SUBLANES - 1, 0), 1.0)
    off = carry * excl
    off2 = jnp.concatenate([off, off], axis=0)
    for r2 in range(KEY_TILE // 16):
        a_ref[16 * r2:16 * (r2 + 1), :] = (s_ref[16 * r2:16 * (r2 + 1), :] * off2).astype(BF16)
    v_t = vt_ref[0, head, :, pl.ds(k0, KEY_TILE)]
    acc = acc + jnp.dot(v_t, a_ref[...], preferred_element_type=F32)
    carry = carry * jnp.broadcast_to(p[0:1, :], carry.shape)
    return carry, acc


def _attn_kernel(q_ref, k_ref, vt_ref, o_ref, s_ref, a_ref):
    tq = q_ref.shape[1]
    qi = pl.program_id(2)
    row = lax.broadcasted_iota(jnp.int32, (SUBLANES, tq), 0)
    qpos = qi * tq + lax.broadcasted_iota(jnp.int32, (SUBLANES, tq), 1)
    accs = []
    for head in range(LANES // HEAD_DIM):
        q_h = q_ref[0, :, head * HEAD_DIM:(head + 1) * HEAD_DIM]
        tile = functools.partial(_attn_tile, q_h=q_h, k_ref=k_ref, vt_ref=vt_ref, s_ref=s_ref,
                                 a_ref=a_ref, head=head, row=row, qpos=qpos)
        carry = jnp.ones((SUBLANES, tq), F32)
        acc = jnp.zeros((HEAD_DIM, tq), F32)
        carry, acc = tile(qi, carry, acc, masked=True)

        def body(it, st):
            return tile(qi - 1 - it, st[0], st[1], masked=False)

        carry, acc = lax.fori_loop(0, qi, body, (carry, acc))
        accs.append(acc)
    o_ref[0] = jnp.concatenate(accs, axis=0).T.astype(o_ref.dtype)


def _attention(q, k_perm, vt_perm):
    bsz, s, d = q.shape
    tq = KEY_TILE
    hp = LANES // HEAD_DIM
    return pl.pallas_call(
        _attn_kernel,
        grid=(bsz, d // LANES, s // tq),
        in_specs=[
            pl.BlockSpec((1, tq, LANES), lambda b, h, i: (b, i, h)),
            pl.BlockSpec((1, hp, s, HEAD_DIM), lambda b, h, i: (b, h, 0, 0)),
            pl.BlockSpec((1, hp, HEAD_DIM, s), lambda b, h, i: (b, h, 0, 0)),
        ],
        out_specs=pl.BlockSpec((1, tq, LANES), lambda b, h, i: (b, i, h)),
        out_shape=jax.ShapeDtypeStruct((bsz, s, d), BF16),
        scratch_shapes=[pltpu.VMEM((KEY_TILE, tq), F32), pltpu.VMEM((KEY_TILE, tq), BF16)],
        compiler_params=_params("arbitrary", "arbitrary", "arbitrary"),
        name="stickbreak_attention",
    )(q, k_perm, vt_perm)


def _permute_keys(k, v):
    bsz, s, d = k.shape
    nh = d // HEAD_DIM
    shape6 = (bsz, s // KEY_TILE, SUBLANES, KEY_ROWS, nh, HEAD_DIM)
    k_perm = k.reshape(shape6).transpose(0, 4, 1, 3, 2, 5).reshape(bsz, nh, s, HEAD_DIM)
    vt_perm = v.reshape(shape6).transpose(0, 4, 5, 1, 3, 2).reshape(bsz, nh, HEAD_DIM, s)
    return k_perm, vt_perm


def _proj_residual_kernel(a_ref, w_ref, x_ref, mod_ref, o_ref, *, gate_row):
    y = jnp.dot(a_ref[0], w_ref[...], preferred_element_type=F32)
    o_ref[0] = x_ref[0] + mod_ref[0][gate_row:gate_row + 1] * y


def _proj_residual(a, w, x, mod, gate_row):
    bsz, s, d = x.shape
    tm = min(ROW_TILE, s)
    return pl.pallas_call(
        functools.partial(_proj_residual_kernel, gate_row=gate_row),
        grid=(bsz, s // tm),
        in_specs=[
            pl.BlockSpec((1, tm, a.shape[-1]), lambda b, j: (b, j, 0)),
            _resident(w.shape),
            pl.BlockSpec((1, tm, d), lambda b, j: (b, j, 0)),
            pl.BlockSpec((1,) + mod.shape[1:], lambda b, j: (b, 0, 0)),
        ],
        out_specs=pl.BlockSpec((1, tm, d), lambda b, j: (b, j, 0)),
        out_shape=jax.ShapeDtypeStruct(x.shape, F32),
        compiler_params=_params("arbitrary", "arbitrary"),
        name="proj_residual",
    )(a, w, x, mod)


def _ssm_kernel(u_ref, bmat_ref, cmat_ref, lam_ref, d_ref, z_ref, xs_ref, st_ref, *, nb, row_chunk):
    rows = u_ref.shape[0]
    half = st_ref.shape[1] // 2

    @pl.when(pl.program_id(1) == 0)
    def _():
        st_ref[...] = jnp.zeros_like(st_ref)

    for r0 in range(0, rows, row_chunk):
        xs_ref[r0:r0 + row_chunk, :] = jnp.dot(u_ref[r0:r0 + row_chunk, :].astype(BF16), bmat_ref[0],
                                               preferred_element_type=F32)

    a_re = jnp.broadcast_to(lam_ref[0, 0:1, :], (nb, half))
    a_im = jnp.broadcast_to(lam_ref[0, 1:2, :], (nb, half))

    def step(t, st):
        x_re, x_im = st
        r = pl.multiple_of(t * nb, nb)
        n_re = a_re * x_re - a_im * x_im + xs_ref[pl.ds(r, nb), :half]
        n_im = a_re * x_im + a_im * x_re + xs_ref[pl.ds(r, nb), half:]
        xs_ref[pl.ds(r, nb), :half] = n_re
        xs_ref[pl.ds(r, nb), half:] = n_im
        return n_re, n_im

    x_re, x_im = lax.fori_loop(0, rows // nb, step, (st_ref[:, :half], st_ref[:, half:]), unroll=2)
    st_ref[:, :half] = x_re
    st_ref[:, half:] = x_im

    for r0 in range(0, rows, row_chunk):
        y = jnp.dot(xs_ref[r0:r0 + row_chunk, :].astype(BF16), cmat_ref[0], preferred_element_type=F32)
        z_ref[r0:r0 + row_chunk, :] = jax.nn.gelu(y + d_ref[...] * u_ref[r0:r0 + row_chunk, :])


def _ssm(u_tm, bmat, cmat, lam, d_skip, nb):
    rows_total, d = u_tm.shape
    s = rows_total // nb
    tt = min(SSM_TIME_TILE, s)
    rows = tt * nb
    n_state = bmat.shape[-1]
    return pl.pallas_call(
        functools.partial(_ssm_kernel, nb=nb, row_chunk=min(256, rows)),
        grid=(d // SSM_CHUNK, s // tt),
        in_specs=[
            pl.BlockSpec((rows, SSM_CHUNK), lambda c, i: (i, c)),
            pl.BlockSpec((1, SSM_CHUNK, n_state), lambda c, i: (c, 0, 0)),
            pl.BlockSpec((1, n_state, SSM_CHUNK), lambda c, i: (c, 0, 0)),
            pl.BlockSpec((1, 2, n_state // 2), lambda c, i: (c, 0, 0)),
            pl.BlockSpec((1, SSM_CHUNK), lambda c, i: (0, c)),
        ],
        out_specs=pl.BlockSpec((rows, SSM_CHUNK), lambda c, i: (i, c)),
        out_shape=jax.ShapeDtypeStruct(u_tm.shape, F32),
        scratch_shapes=[pltpu.VMEM((rows, n_state), F32), pltpu.VMEM((nb, n_state), F32)],
        compiler_params=_params("arbitrary", "arbitrary"),
        name="s5_scan",
    )(u_tm, bmat, cmat, lam, d_skip)


def _ssm_matrices(a_re, a_im, log_dt, b_re, b_im, c_re, c_im):
    g, p = a_re.shape
    hch = b_re.shape[-1]
    gpc = SSM_CHUNK // hch
    nc = g // gpc
    lam = lax.complex(a_re.astype(F32), a_im.astype(F32))
    dt = jnp.exp(log_dt.astype(F32))[:, None]
    lam_bar = jnp.exp(lam * dt)
    b_bar = ((lam_bar - 1) / lam)[..., None] * lax.complex(b_re.astype(F32), b_im.astype(F32))
    eye = jnp.eye(gpc, dtype=F32)

    def in_blocks(m):
        m = m.reshape(nc, gpc, p, hch).transpose(0, 1, 3, 2)
        return jnp.einsum('cghp,gk->cghkp', m, eye).reshape(nc, gpc * hch, gpc * p)

    def out_blocks(m):
        m = m.reshape(nc, gpc, hch, p).transpose(0, 1, 3, 2)
        return jnp.einsum('cgph,gk->cgpkh', m, eye).reshape(nc, gpc * p, gpc * hch)

    bmat = jnp.concatenate([in_blocks(jnp.real(b_bar)), in_blocks(jnp.imag(b_bar))], axis=-1)
    cmat = jnp.concatenate([out_blocks(c_re.astype(F32)), out_blocks(-c_im.astype(F32))], axis=1)
    lam_ri = jnp.stack([jnp.real(lam_bar).reshape(nc, gpc * p), jnp.imag(lam_bar).reshape(nc, gpc * p)], axis=1)
    return bmat.astype(BF16), cmat.astype(BF16), lam_ri


def _glu_kernel(z_ref, wg_ref, bg_ref, wo_ref, x_ref, mod_ref, o_ref):
    z = z_ref[...]
    a = jnp.dot(z.astype(BF16), wg_ref[...], preferred_element_type=F32) + bg_ref[...]
    g = z * jax.nn.sigmoid(a)
    y = jnp.dot(g.astype(BF16), wo_ref[...], preferred_element_type=F32)
    o_ref[0] = x_ref[0] + mod_ref[0][2:3] * y


def _glu_residual(z_tm, w_glu, b_glu, w_o, x, mod):
    bsz, s, d = x.shape
    tm = min(ROW_TILE, s)
    return pl.pallas_call(
        _glu_kernel,
        grid=(bsz, s // tm),
        in_specs=[
            pl.BlockSpec((tm, d), lambda b, j: (j, b)),
            _resident(w_glu.shape),
            _resident(b_glu.shape),
            _resident(w_o.shape),
            pl.BlockSpec((1, tm, d), lambda b, j: (b, j, 0)),
            pl.BlockSpec((1,) + mod.shape[1:], lambda b, j: (b, 0, 0)),
        ],
        out_specs=pl.BlockSpec((1, tm, d), lambda b, j: (b, j, 0)),
        out_shape=jax.ShapeDtypeStruct(x.shape, F32),
        compiler_params=_params("arbitrary", "arbitrary"),
        name="glu_residual",
    )(z_tm, w_glu, b_glu, w_o, x, mod)


def _ffn_kernel(x_ref, xh_ref, mod_ref, gain_ref, wup_ref, cw_ref, cb_ref, wdn_ref, *rest, n_chunks, final):
    if final:
        fmod_ref, fgain_ref, o_ref, hs_ref, ug_ref, uv_ref, acc_ref = rest
    else:
        o_ref, hs_ref, ug_ref, uv_ref, acc_ref = rest
    tm = x_ref.shape[1]
    f = wdn_ref.shape[0]
    tf = f // n_chunks
    m = mod_ref[0]
    gain = gain_ref[...]
    x = x_ref[0]
    h_halo = _modnorm(xh_ref[0], gain, m[3:4], m[4:5])
    h_halo = jnp.where(pl.program_id(1) > 0, h_halo, 0.0)
    hs_ref[0:HALO, :] = h_halo.astype(BF16)
    hs_ref[HALO:, :] = _modnorm(x, gain, m[3:4], m[4:5]).astype(BF16)

    def conv(u_ref, c0):
        w = cw_ref[:, c0:c0 + tf]
        return (w[2:3] * u_ref[pl.ds(HALO, tm), :] + w[1:2] * u_ref[pl.ds(HALO - 1, tm), :]
                + w[0:1] * u_ref[pl.ds(HALO - 2, tm), :] + cb_ref[:, c0:c0 + tf])

    for c in range(n_chunks):
        g0 = c * tf
        v0 = f + c * tf
        hs = hs_ref[...]
        ug_ref[...] = jnp.dot(hs, wup_ref[:, g0:g0 + tf], preferred_element_type=F32)
        uv_ref[...] = jnp.dot(hs, wup_ref[:, v0:v0 + tf], preferred_element_type=F32)
        act = (jax.nn.silu(conv(ug_ref, g0)) * conv(uv_ref, v0)).astype(BF16)
        y = jnp.dot(act, wdn_ref[g0:g0 + tf, :], preferred_element_type=F32)
        if c == 0:
            acc_ref[...] = y
        else:
            acc_ref[...] += y
    out = x + m[5:6] * acc_ref[...]
    if final:
        fm = fmod_ref[0]
        out = _modnorm(out, fgain_ref[...], fm[0:1], fm[1:2])
    o_ref[0] = out


def _conv_ffn(x, mod, gain, w_up, conv_w, conv_b, w_down, final=None):
    bsz, s, d = x.shape
    f = w_down.shape[0]
    tm = min(ROW_TILE, s)
    n_chunks = 2
    tf = f // n_chunks
    halo_blocks = tm // HALO
    in_specs = [
        pl.BlockSpec((1, tm, d), lambda b, j: (b, j, 0)),
        pl.BlockSpec((1, HALO, d), lambda b, j: (b, jnp.maximum(j * halo_blocks - 1, 0), 0)),
        pl.BlockSpec((1,) + mod.shape[1:], lambda b, j: (b, 0, 0)),
        _resident(gain.shape),
        _resident(w_up.shape),
        _resident(conv_w.shape),
        _resident(conv_b.shape),
        _resident(w_down.shape),
    ]
    args = [x, x, mod, gain, w_up, conv_w, conv_b, w_down]
    if final is not None:
        in_specs += [pl.BlockSpec((1,) + final[0].shape[1:], lambda b, j: (b, 0, 0)), _resident(final[1].shape)]
        args += list(final)
    return pl.pallas_call(
        functools.partial(_ffn_kernel, n_chunks=n_chunks, final=final is not None),
        grid=(bsz, s // tm),
        in_specs=in_specs,
        out_specs=pl.BlockSpec((1, tm, d), lambda b, j: (b, j, 0)),
        out_shape=jax.ShapeDtypeStruct(x.shape, F32),
        scratch_shapes=[
            pltpu.VMEM((tm + HALO, d), BF16),
            pltpu.VMEM((tm + HALO, tf), F32),
            pltpu.VMEM((tm + HALO, tf), F32),
            pltpu.VMEM((tm, d), F32),
        ],
        compiler_params=_params("arbitrary", "arbitrary"),
        name="conv_ffn",
    )(*args)


def kernel(x, c, norm_mix, norm_ffn, w_mod, b_mod, w_qkv, w_o_attn, w_in_ssm, a_re, a_im, log_dt, b_re, b_im, c_re, c_im, d_skip, w_glu, b_glu, w_o_ssm, w_up, conv_w, conv_b, w_down, norm_out, w_fin, b_fin):
    bsz, s, d = x.shape
    depth = w_mod.shape[0]
    assert s % KEY_TILE == 0 and d % LANES == 0 and bsz % SUBLANES == 0

    mods = _mod_project(c, w_mod, b_mod).reshape(depth, bsz, 6, d)
    fin_mod = _mod_project(c, w_fin[None], b_fin[None]).reshape(bsz, 2, d)

    for i in range(depth):
        mod = mods[i]
        j = i // 2
        gain_mix = norm_mix[i][None]
        if i % 2 == 0:
            w = jnp.concatenate([w_qkv[j][:, :d] * (HEAD_DIM ** -0.5), w_qkv[j][:, d:]], axis=1).astype(BF16)
            q, k, v = _norm_project(x, mod, gain_mix, w, (BF16, BF16, BF16), time_major=False)
            k_perm, vt_perm = _permute_keys(k, v)
            o = _attention(q, k_perm, vt_perm)
            x = _proj_residual(o, w_o_attn[j].astype(BF16), x, mod, gate_row=2)
        else:
            (u_tm,) = _norm_project(x, mod, gain_mix, w_in_ssm[j].astype(BF16), (F32,), time_major=True)
            bmat, cmat, lam = _ssm_matrices(a_re[j], a_im[j], log_dt[j], b_re[j], b_im[j], c_re[j], c_im[j])
            z_tm = _ssm(u_tm.reshape(s * bsz, d), bmat, cmat, lam, d_skip[j][None], nb=bsz)
            x = _glu_residual(z_tm.reshape(s, bsz * d), w_glu[j].astype(BF16), b_glu[j][None],
                              w_o_ssm[j].astype(BF16), x, mod)
        final = (fin_mod, norm_out[None]) if i == depth - 1 else None
        x = _conv_ffn(x, mod, norm_ffn[i][None], w_up[i].astype(BF16), conv_w[i], conv_b[i][None],
                      w_down[i].astype(BF16), final=final)
    return x
```

```python
import functools
import math

import jax
import jax.numpy as jnp
from jax import lax
from jax.experimental import pallas as pl
from jax.experimental.pallas import tpu as pltpu

F32 = jnp.float32
BF16 = jnp.bfloat16

HEAD_DIM = 64
SSM_GROUP = 16
STATE = 64
CONV_W = 3
EPS = 1e-6

LANES = 128
SUBLANES = 8
KEY_BLOCK = 128
KEY_ROWS = KEY_BLOCK // SUBLANES
HALO = 16
SSM_CHUNK = LANES
VMEM_LIMIT = 56 * 1024 * 1024
MASKED_SCORE = -1e30

ROW_TILE = 512
SSM_TIME_TILE = 128
MOD_COL_TILE = 2048


def _params(*sem):
    return pltpu.CompilerParams(dimension_semantics=sem, vmem_limit_bytes=VMEM_LIMIT)


def _resident(shape):
    nd = len(shape)
    return pl.BlockSpec(shape, lambda *_: (0,) * nd, pipeline_mode=pl.Buffered(1))


def _modnorm(x, gain, shift, scale):
    ms = jnp.mean(x * x, axis=-1, keepdims=True)
    y = x * lax.rsqrt(ms + EPS) * gain
    return y * (1.0 + scale) + shift


def _mod_kernel(c_ref, w_ref, b_ref, o_ref):
    ca = jax.nn.silu(c_ref[...])
    o_ref[0] = jnp.dot(ca, w_ref[0], preferred_element_type=F32) + b_ref[0]


def _mod_project(c, w, b):
    nl, d, n = w.shape
    bsz = c.shape[0]
    tn = min(MOD_COL_TILE, n)
    return pl.pallas_call(
        _mod_kernel,
        grid=(nl, n // tn),
        in_specs=[
            pl.BlockSpec((bsz, d), lambda l, j: (0, 0)),
            pl.BlockSpec((1, d, tn), lambda l, j: (l, 0, j)),
            pl.BlockSpec((1, 1, tn), lambda l, j: (l, 0, j)),
        ],
        out_specs=pl.BlockSpec((1, bsz, tn), lambda l, j: (l, 0, j)),
        out_shape=jax.ShapeDtypeStruct((nl, bsz, n), F32),
        compiler_params=_params("arbitrary", "arbitrary"),
        name="adaln_project",
    )(c, w, b.reshape(nl, 1, n))


def _qkv_kernel(x_ref, mod_ref, gain_ref, wq_ref, wk_ref, wvt_ref, q_ref, k_ref, vt_ref, h_ref, hp_ref):
    tm = x_ref.shape[1]
    m = mod_ref[0]
    h = _modnorm(x_ref[0], gain_ref[...], m[0:1], m[1:2])
    q_ref[0] = jnp.dot(h.astype(BF16), wq_ref[...], preferred_element_type=F32).astype(q_ref.dtype)
    for c in range(h_ref.shape[0]):
        h_ref[c] = h[:, c * LANES:(c + 1) * LANES]
        for blk in range(tm // KEY_BLOCK):
            for r2 in range(KEY_ROWS // 2):
                base = blk * KEY_BLOCK
                lo = h_ref[c, pl.ds(base + 2 * r2, SUBLANES, stride=KEY_ROWS), :]
                hi = h_ref[c, pl.ds(base + 2 * r2 + 1, SUBLANES, stride=KEY_ROWS), :]
                hp_ref[base + 16 * r2:base + 16 * (r2 + 1), c * LANES:(c + 1) * LANES] = (
                    jnp.concatenate([lo, hi], axis=0).astype(BF16))
    hp = hp_ref[...]
    k_ref[0] = jnp.dot(hp, wk_ref[...], preferred_element_type=F32).astype(k_ref.dtype)
    vt_ref[0] = lax.dot_general(wvt_ref[...], hp, (((1,), (1,)), ((), ())),
                                preferred_element_type=F32).astype(vt_ref.dtype)


def _qkv_project(x, mod, gain, wq, wk, wvt):
    bsz, s, d = x.shape
    tm = min(ROW_TILE, s)
    row_spec = pl.BlockSpec((1, tm, d), lambda b, j: (b, j, 0))
    return pl.pallas_call(
        _qkv_kernel,
        grid=(bsz, s // tm),
        in_specs=[
            row_spec,
            pl.BlockSpec((1,) + mod.shape[1:], lambda b, j: (b, 0, 0)),
            _resident(gain.shape),
            _resident(wq.shape),
            _resident(wk.shape),
            _resident(wvt.shape),
        ],
        out_specs=[row_spec, row_spec, pl.BlockSpec((1, d, tm), lambda b, j: (b, 0, j))],
        out_shape=[jax.ShapeDtypeStruct((bsz, s, d), BF16), jax.ShapeDtypeStruct((bsz, s, d), BF16),
                   jax.ShapeDtypeStruct((bsz, d, s), BF16)],
        scratch_shapes=[pltpu.VMEM((d // LANES, tm, LANES), F32), pltpu.VMEM((tm, d), BF16)],
        compiler_params=_params("arbitrary", "arbitrary"),
        name="qkv_project",
    )(x, mod, gain, wq, wk, wvt)


def _norm_matmul_kernel(x_ref, mod_ref, gain_ref, w_ref, o_ref):
    m = mod_ref[0]
    h = _modnorm(x_ref[0], gain_ref[...], m[0:1], m[1:2]).astype(BF16)
    o_ref[...] = jnp.dot(h, w_ref[...], preferred_element_type=F32).astype(o_ref.dtype)


def _norm_project_time_major(x, mod, gain, w):
    bsz, s, d = x.shape
    n = w.shape[1]
    tm = min(ROW_TILE, s)
    return pl.pallas_call(
        _norm_matmul_kernel,
        grid=(bsz, s // tm),
        in_specs=[
            pl.BlockSpec((1, tm, d), lambda b, j: (b, j, 0)),
            pl.BlockSpec((1,) + mod.shape[1:], lambda b, j: (b, 0, 0)),
            _resident(gain.shape),
            _resident(w.shape),
        ],
        out_specs=pl.BlockSpec((tm, n), lambda b, j: (j, b)),
        out_shape=jax.ShapeDtypeStruct((s, bsz * n), F32),
        compiler_params=_params("arbitrary", "arbitrary"),
        name="norm_project",
    )(x, mod, gain, w)


def _streams(z_view, col, base, kq, s_view):
    run = jnp.ones((SUBLANES, KEY_BLOCK), F32)
    for r in range(KEY_ROWS - 1, -1, -1):
        lo = base + SUBLANES * r
        z = z_view[lo:lo + SUBLANES, col:col + KEY_BLOCK]
        if kq is not None:
            z = jnp.where(kq > r, z, MASKED_SCORE)
        neg_abs = pltpu.bitcast(pltpu.bitcast(z, jnp.uint32) | jnp.uint32(0x80000000), F32)
        e = jnp.exp2(neg_abs)
        t1 = run * (1.0 / (1.0 + e))
        t2 = e * t1
        pos = z >= 0.0
        s_view[lo:lo + SUBLANES, :] = jnp.where(pos, t1, t2)
        run = jnp.where(pos, t2, t1)
    return run


def _suffix(tot, row):
    p = tot
    for d in (1, 2, 4):
        p = jnp.where(row + d < SUBLANES, p * pltpu.roll(p, SUBLANES - d, 0), p)
    excl = jnp.where(row < SUBLANES - 1, pltpu.roll(p, SUBLANES - 1, 0), 1.0)
    return excl, jnp.broadcast_to(p[0:1, :], p.shape)


def _apply_offsets(s_view, a_view, base, off):
    off2 = jnp.concatenate([off, off], axis=0)
    for r2 in range(KEY_BLOCK // 16):
        lo = base + 16 * r2
        a_view[lo:lo + 16, :] = (s_view[lo:lo + 16, :] * off2).astype(BF16)


def _nt_dot(a, b):
    return lax.dot_general(a, b, (((1,), (1,)), ((), ())), preferred_element_type=F32)


def _attn_kernel(q_ref, k_ref, vt_ref, o_ref, kp_ref, vp_ref, z_ref, s_ref, a_ref, oacc_ref, c_ref, flag_ref):
    s_len = q_ref.shape[1]
    n_heads = LANES // HEAD_DIM
    row = lax.broadcasted_iota(jnp.int32, (SUBLANES, KEY_BLOCK), 0)
    lane = lax.broadcasted_iota(jnp.int32, (SUBLANES, KEY_BLOCK), 1)
    kq = lane - KEY_ROWS * row
    q_lane = lax.broadcasted_iota(jnp.int32, (KEY_BLOCK, LANES), 1)

    n_blocks = s_len // KEY_BLOCK

    kp_ref[0:KEY_BLOCK, :] = jnp.zeros((KEY_BLOCK, LANES), kp_ref.dtype)
    kp_ref[KEY_BLOCK:, :] = k_ref[0]
    vp_ref[:, 0:KEY_BLOCK] = jnp.zeros((LANES, KEY_BLOCK), vp_ref.dtype)
    vp_ref[:, KEY_BLOCK:] = vt_ref[0]

    def stacked_queries(w):
        q = q_ref[0, pl.ds(pl.multiple_of(w * KEY_BLOCK, KEY_BLOCK), KEY_BLOCK), :]
        zero = jnp.zeros_like(q)
        return jnp.concatenate([jnp.where((q_lane >= h * HEAD_DIM) & (q_lane < (h + 1) * HEAD_DIM), q, zero)
                                for h in range(n_heads)], axis=0)

    def store_out(w):
        o_ref[0, pl.ds(pl.multiple_of(w * KEY_BLOCK, KEY_BLOCK), KEY_BLOCK), :] = (
            oacc_ref[w].T.astype(o_ref.dtype))

    def scores(p, slot):
        for i in range(2):
            w = jnp.minimum(2 * p + i, n_blocks - 1)
            k_win = kp_ref[pl.ds(pl.multiple_of(w * KEY_BLOCK, KEY_BLOCK), 2 * KEY_BLOCK), :]
            z_ref[slot, i] = _nt_dot(k_win, stacked_queries(w))

    def weights(p, slot):
        for i in range(2):
            w = 2 * p + i
            for h in range(n_heads):
                z_view, s_view, a_view = z_ref.at[slot, i], s_ref.at[slot, i, h], a_ref.at[slot, i, h]
                tot_d = _streams(z_view, h * KEY_BLOCK, KEY_BLOCK, kq, s_view)
                tot_p = _streams(z_view, h * KEY_BLOCK, 0, None, s_view)
                excl_d, total_d = _suffix(tot_d, row)
                excl_p, total_p = _suffix(tot_p, row)
                _apply_offsets(s_view, a_view, KEY_BLOCK, excl_d)
                _apply_offsets(s_view, a_view, 0, total_d * excl_p)
                c_ref[w, h] = total_d * total_p
            alive = jnp.max(jnp.maximum(c_ref[w, 0], c_ref[w, 1])) > 0.0
            flag_ref[w] = alive.astype(jnp.int32)

    def values(p, slot):
        for i in range(2):
            w = jnp.maximum(2 * p + i, 0)
            k0 = pl.multiple_of(w * KEY_BLOCK, KEY_BLOCK)
            for h in range(n_heads):
                v_win = vp_ref[h * HEAD_DIM:(h + 1) * HEAD_DIM, pl.ds(k0, 2 * KEY_BLOCK)]
                oacc_ref[w, h * HEAD_DIM:(h + 1) * HEAD_DIM, :] = jnp.dot(v_win, a_ref[slot, i, h],
                                                                          preferred_element_type=F32)
            store_out(w)

    scores(0, 0)
    a_ref[1] = jnp.zeros(a_ref.shape[1:], a_ref.dtype)

    def two_pairs(m, _):
        p = 2 * m
        scores(p + 1, 1)
        weights(p, 0)
        values(p - 1, 1)
        scores(p + 2, 0)
        weights(p + 1, 1)
        values(p, 0)
        return 0

    lax.fori_loop(0, n_blocks // 4, two_pairs, 0)
    values(n_blocks // 2 - 1, 1)

    def older_blocks(w):
        q2 = stacked_queries(w)

        def cond(st):
            return jnp.logical_and(st[0] >= 0, st[1] > 0)

        def body(st):
            j = st[0]
            k0 = pl.multiple_of(j * KEY_BLOCK, KEY_BLOCK)
            z_ref[0, 0, 0:KEY_BLOCK, :] = _nt_dot(k_ref[0, pl.ds(k0, KEY_BLOCK), :], q2)
            carries = []
            for h in range(n_heads):
                s_view, a_view = s_ref.at[0, 0, h], a_ref.at[0, 0, h]
                tot = _streams(z_ref.at[0, 0], h * KEY_BLOCK, 0, None, s_view)
                excl, total = _suffix(tot, row)
                carry = c_ref[w, h]
                _apply_offsets(s_view, a_view, 0, carry * excl)
                v_blk = vt_ref[0, h * HEAD_DIM:(h + 1) * HEAD_DIM, pl.ds(k0, KEY_BLOCK)]
                oacc_ref[w, h * HEAD_DIM:(h + 1) * HEAD_DIM, :] += jnp.dot(
                    v_blk, a_ref[0, 0, h, 0:KEY_BLOCK, :], preferred_element_type=F32)
                carries.append(carry * total)
                c_ref[w, h] = carries[-1]
            alive = jnp.max(functools.reduce(jnp.maximum, carries)) > 0.0
            return (j - 1, alive.astype(jnp.int32))

        lax.while_loop(cond, body, (w - 2, jnp.int32(1)))

    def finish(w, _):
        @pl.when(flag_ref[w] > 0)
        def _():
            older_blocks(w)
            store_out(w)

        return 0

    lax.fori_loop(2, n_blocks, finish, 0)


def _attention(q, k_perm, vt_perm):
    bsz, s, d = q.shape
    n_heads = LANES // HEAD_DIM
    n_blocks = s // KEY_BLOCK
    return pl.pallas_call(
        _attn_kernel,
        grid=(bsz, d // LANES),
        in_specs=[
            pl.BlockSpec((1, s, LANES), lambda b, h: (b, 0, h)),
            pl.BlockSpec((1, s, LANES), lambda b, h: (b, 0, h)),
            pl.BlockSpec((1, LANES, s), lambda b, h: (b, h, 0)),
        ],
        out_specs=pl.BlockSpec((1, s, LANES), lambda b, h: (b, 0, h)),
        out_shape=jax.ShapeDtypeStruct((bsz, s, d), BF16),
        scratch_shapes=[
            pltpu.VMEM((s + KEY_BLOCK, LANES), BF16),
            pltpu.VMEM((LANES, s + KEY_BLOCK), BF16),
            pltpu.VMEM((2, 2, 2 * KEY_BLOCK, n_heads * KEY_BLOCK), F32),
            pltpu.VMEM((2, 2, n_heads, 2 * KEY_BLOCK, KEY_BLOCK), F32),
            pltpu.VMEM((2, 2, n_heads, 2 * KEY_BLOCK, KEY_BLOCK), BF16),
            pltpu.VMEM((n_blocks, LANES, KEY_BLOCK), F32),
            pltpu.VMEM((n_blocks, n_heads, SUBLANES, KEY_BLOCK), F32),
            pltpu.SMEM((n_blocks,), jnp.int32),
        ],
        compiler_params=_params("arbitrary", "arbitrary"),
        name="stickbreak_attention",
    )(q, k_perm, vt_perm)


def _proj_residual_kernel(a_ref, w_ref, x_ref, mod_ref, o_ref, *, gate_row):
    y = jnp.dot(a_ref[0], w_ref[...], preferred_element_type=F32)
    o_ref[0] = x_ref[0] + mod_ref[0][gate_row:gate_row + 1] * y


def _proj_residual(a, w, x, mod, gate_row):
    bsz, s, d = x.shape
    tm = min(ROW_TILE, s)
    return pl.pallas_call(
        functools.partial(_proj_residual_kernel, gate_row=gate_row),
        grid=(bsz, s // tm),
        in_specs=[
            pl.BlockSpec((1, tm, a.shape[-1]), lambda b, j: (b, j, 0)),
            _resident(w.shape),
            pl.BlockSpec((1, tm, d), lambda b, j: (b, j, 0)),
            pl.BlockSpec((1,) + mod.shape[1:], lambda b, j: (b, 0, 0)),
        ],
        out_specs=pl.BlockSpec((1, tm, d), lambda b, j: (b, j, 0)),
        out_shape=jax.ShapeDtypeStruct(x.shape, F32),
        compiler_params=_params("arbitrary", "arbitrary"),
        name="proj_residual",
    )(a, w, x, mod)


def _ssm_kernel(u_ref, bmat_ref, cmat_ref, lam_ref, d_ref, z_ref, xs_ref, st_ref, *, nb, row_chunk):
    rows = u_ref.shape[0]
    half = st_ref.shape[1] // 2

    @pl.when(pl.program_id(1) == 0)
    def _():
        st_ref[...] = jnp.zeros_like(st_ref)

    for r0 in range(0, rows, row_chunk):
        xs_ref[r0:r0 + row_chunk, :] = jnp.dot(u_ref[r0:r0 + row_chunk, :].astype(BF16), bmat_ref[0],
                                               preferred_element_type=F32)

    a_re = jnp.broadcast_to(lam_ref[0, 0:1, :], (nb, half))
    a_im = jnp.broadcast_to(lam_ref[0, 1:2, :], (nb, half))

    def step(t, st):
        x_re, x_im = st
        r = pl.multiple_of(t * nb, nb)
        n_re = a_re * x_re - a_im * x_im + xs_ref[pl.ds(r, nb), :half]
        n_im = a_re * x_im + a_im * x_re + xs_ref[pl.ds(r, nb), half:]
        xs_ref[pl.ds(r, nb), :half] = n_re
        xs_ref[pl.ds(r, nb), half:] = n_im
        return n_re, n_im

    x_re, x_im = lax.fori_loop(0, rows // nb, step, (st_ref[:, :half], st_ref[:, half:]), unroll=2)
    st_ref[:, :half] = x_re
    st_ref[:, half:] = x_im

    for r0 in range(0, rows, row_chunk):
        y = jnp.dot(xs_ref[r0:r0 + row_chunk, :].astype(BF16), cmat_ref[0], preferred_element_type=F32)
        z_ref[r0:r0 + row_chunk, :] = jax.nn.gelu(y + d_ref[...] * u_ref[r0:r0 + row_chunk, :])


def _ssm(u_tm, bmat, cmat, lam, d_skip, nb):
    rows_total, d = u_tm.shape
    s = rows_total // nb
    tt = min(SSM_TIME_TILE, s)
    rows = tt * nb
    n_state = bmat.shape[-1]
    return pl.pallas_call(
        functools.partial(_ssm_kernel, nb=nb, row_chunk=min(256, rows)),
        grid=(d // SSM_CHUNK, s // tt),
        in_specs=[
            pl.BlockSpec((rows, SSM_CHUNK), lambda c, i: (i, c)),
            pl.BlockSpec((1, SSM_CHUNK, n_state), lambda c, i: (c, 0, 0)),
            pl.BlockSpec((1, n_state, SSM_CHUNK), lambda c, i: (c, 0, 0)),
            pl.BlockSpec((1, 2, n_state // 2), lambda c, i: (c, 0, 0)),
            pl.BlockSpec((1, SSM_CHUNK), lambda c, i: (0, c)),
        ],
        out_specs=pl.BlockSpec((rows, SSM_CHUNK), lambda c, i: (i, c)),
        out_shape=jax.ShapeDtypeStruct(u_tm.shape, F32),
        scratch_shapes=[pltpu.VMEM((rows, n_state), F32), pltpu.VMEM((nb, n_state), F32)],
        compiler_params=_params("arbitrary", "arbitrary"),
        name="s5_scan",
    )(u_tm, bmat, cmat, lam, d_skip)


def _ssm_matrices(a_re, a_im, log_dt, b_re, b_im, c_re, c_im):
    g, p = a_re.shape
    hch = b_re.shape[-1]
    gpc = SSM_CHUNK // hch
    nc = g // gpc
    lam = lax.complex(a_re.astype(F32), a_im.astype(F32))
    dt = jnp.exp(log_dt.astype(F32))[:, None]
    lam_bar = jnp.exp(lam * dt)
    b_bar = ((lam_bar - 1) / lam)[..., None] * lax.complex(b_re.astype(F32), b_im.astype(F32))
    eye = jnp.eye(gpc, dtype=F32)

    def in_blocks(m):
        m = m.reshape(nc, gpc, p, hch).transpose(0, 1, 3, 2)
        return jnp.einsum('cghp,gk->cghkp', m, eye).reshape(nc, gpc * hch, gpc * p)

    def out_blocks(m):
        m = m.reshape(nc, gpc, hch, p).transpose(0, 1, 3, 2)
        return jnp.einsum('cgph,gk->cgpkh', m, eye).reshape(nc, gpc * p, gpc * hch)

    bmat = jnp.concatenate([in_blocks(jnp.real(b_bar)), in_blocks(jnp.imag(b_bar))], axis=-1)
    cmat = jnp.concatenate([out_blocks(c_re.astype(F32)), out_blocks(-c_im.astype(F32))], axis=1)
    lam_ri = jnp.stack([jnp.real(lam_bar).reshape(nc, gpc * p), jnp.imag(lam_bar).reshape(nc, gpc * p)], axis=1)
    return bmat.astype(BF16), cmat.astype(BF16), lam_ri


def _glu_kernel(z_ref, wg_ref, bg_ref, wo_ref, x_ref, mod_ref, o_ref):
    z = z_ref[...]
    a = jnp.dot(z.astype(BF16), wg_ref[...], preferred_element_type=F32) + bg_ref[...]
    g = z * jax.nn.sigmoid(a)
    y = jnp.dot(g.astype(BF16), wo_ref[...], preferred_element_type=F32)
    o_ref[0] = x_ref[0] + mod_ref[0][2:3] * y


def _glu_residual(z_tm, w_glu, b_glu, w_o, x, mod):
    bsz, s, d = x.shape
    tm = min(ROW_TILE, s)
    return pl.pallas_call(
        _glu_kernel,
        grid=(bsz, s // tm),
        in_specs=[
            pl.BlockSpec((tm, d), lambda b, j: (j, b)),
            _resident(w_glu.shape),
            _resident(b_glu.shape),
            _resident(w_o.shape),
            pl.BlockSpec((1, tm, d), lambda b, j: (b, j, 0)),
            pl.BlockSpec((1,) + mod.shape[1:], lambda b, j: (b, 0, 0)),
        ],
        out_specs=pl.BlockSpec((1, tm, d), lambda b, j: (b, j, 0)),
        out_shape=jax.ShapeDtypeStruct(x.shape, F32),
        compiler_params=_params("arbitrary", "arbitrary"),
        name="glu_residual",
    )(z_tm, w_glu, b_glu, w_o, x, mod)


def _ffn_kernel(x_ref, xh_ref, mod_ref, gain_ref, wup_ref, cw_ref, cb_ref, wdn_ref, *rest, n_chunks, final):
    if final:
        fmod_ref, fgain_ref, o_ref, hs_ref, ug_ref, uv_ref, acc_ref = rest
    else:
        o_ref, hs_ref, ug_ref, uv_ref, acc_ref = rest
    tm = x_ref.shape[1]
    f = wdn_ref.shape[0]
    tf = f // n_chunks
    m = mod_ref[0]
    gain = gain_ref[...]
    x = x_ref[0]
    h_halo = _modnorm(xh_ref[0], gain, m[3:4], m[4:5])
    h_halo = jnp.where(pl.program_id(1) > 0, h_halo, 0.0)
    hs_ref[0:HALO, :] = h_halo.astype(BF16)
    hs_ref[HALO:, :] = _modnorm(x, gain, m[3:4], m[4:5]).astype(BF16)

    def conv(u_ref, c0):
        w = cw_ref[:, c0:c0 + tf]
        return (w[2:3] * u_ref[pl.ds(HALO, tm), :] + w[1:2] * u_ref[pl.ds(HALO - 1, tm), :]
                + w[0:1] * u_ref[pl.ds(HALO - 2, tm), :] + cb_ref[:, c0:c0 + tf])

    for c in range(n_chunks):
        g0 = c * tf
        v0 = f + c * tf
        hs = hs_ref[...]
        ug_ref[...] = jnp.dot(hs, wup_ref[:, g0:g0 + tf], preferred_element_type=F32)
        uv_ref[...] = jnp.dot(hs, wup_ref[:, v0:v0 + tf], preferred_element_type=F32)
        act = (jax.nn.silu(conv(ug_ref, g0)) * conv(uv_ref, v0)).astype(BF16)
        y = jnp.dot(act, wdn_ref[g0:g0 + tf, :], preferred_element_type=F32)
        if c == 0:
            acc_ref[...] = y
        else:
            acc_ref[...] += y
    out = x + m[5:6] * acc_ref[...]
    if final:
        fm = fmod_ref[0]
        out = _modnorm(out, fgain_ref[...], fm[0:1], fm[1:2])
    o_ref[0] = out


def _conv_ffn(x, mod, gain, w_up, conv_w, conv_b, w_down, final=None):
    bsz, s, d = x.shape
    f = w_down.shape[0]
    tm = min(ROW_TILE, s)
    n_chunks = 2
    tf = f // n_chunks
    halo_blocks = tm // HALO
    in_specs = [
        pl.BlockSpec((1, tm, d), lambda b, j: (b, j, 0)),
        pl.BlockSpec((1, HALO, d), lambda b, j: (b, jnp.maximum(j * halo_blocks - 1, 0), 0)),
        pl.BlockSpec((1,) + mod.shape[1:], lambda b, j: (b, 0, 0)),
        _resident(gain.shape),
        _resident(w_up.shape),
        _resident(conv_w.shape),
        _resident(conv_b.shape),
        _resident(w_down.shape),
    ]
    args = [x, x, mod, gain, w_up, conv_w, conv_b, w_down]
    if final is not None:
        in_specs += [pl.BlockSpec((1,) + final[0].shape[1:], lambda b, j: (b, 0, 0)), _resident(final[1].shape)]
        args += list(final)
    return pl.pallas_call(
        functools.partial(_ffn_kernel, n_chunks=n_chunks, final=final is not None),
        grid=(bsz, s // tm),
        in_specs=in_specs,
        out_specs=pl.BlockSpec((1, tm, d), lambda b, j: (b, j, 0)),
        out_shape=jax.ShapeDtypeStruct(x.shape, F32),
        scratch_shapes=[
            pltpu.VMEM((tm + HALO, d), BF16),
            pltpu.VMEM((tm + HALO, tf), F32),
            pltpu.VMEM((tm + HALO, tf), F32),
            pltpu.VMEM((tm, d), F32),
        ],
        compiler_params=_params("arbitrary", "arbitrary"),
        name="conv_ffn",
    )(*args)


def kernel(x, c, norm_mix, norm_ffn, w_mod, b_mod, w_qkv, w_o_attn, w_in_ssm, a_re, a_im, log_dt, b_re, b_im, c_re, c_im, d_skip, w_glu, b_glu, w_o_ssm, w_up, conv_w, conv_b, w_down, norm_out, w_fin, b_fin):
    bsz, s, d = x.shape
    depth = w_mod.shape[0]
    assert s % (4 * KEY_BLOCK) == 0 and d % LANES == 0 and bsz % SUBLANES == 0

    mods = _mod_project(c, w_mod, b_mod).reshape(depth, bsz, 6, d)
    fin_mod = _mod_project(c, w_fin[None], b_fin[None]).reshape(bsz, 2, d)

    for i in range(depth):
        mod = mods[i]
        j = i // 2
        gain_mix = norm_mix[i][None]
        if i % 2 == 0:
            wq = (w_qkv[j][:, :d] * (math.log2(math.e) * HEAD_DIM ** -0.5)).astype(BF16)
            wk = w_qkv[j][:, d:2 * d].astype(BF16)
            wvt = w_qkv[j][:, 2 * d:].T.astype(BF16)
            q, k_perm, vt_perm = _qkv_project(x, mod, gain_mix, wq, wk, wvt)
            o = _attention(q, k_perm, vt_perm)
            x = _proj_residual(o, w_o_attn[j].astype(BF16), x, mod, gate_row=2)
        else:
            u_tm = _norm_project_time_major(x, mod, gain_mix, w_in_ssm[j].astype(BF16))
            bmat, cmat, lam = _ssm_matrices(a_re[j], a_im[j], log_dt[j], b_re[j], b_im[j], c_re[j], c_im[j])
            z_tm = _ssm(u_tm.reshape(s * bsz, d), bmat, cmat, lam, d_skip[j][None], nb=bsz)
            x = _glu_residual(z_tm.reshape(s, bsz * d), w_glu[j].astype(BF16), b_glu[j][None],
                              w_o_ssm[j].astype(BF16), x, mod)
        final = (fin_mod, norm_out[None]) if i == depth - 1 else None
        x = _conv_ffn(x, mod, norm_ffn[i][None], w_up[i].astype(BF16), conv_w[i], conv_b[i][None],
                      w_down[i].astype(BF16), final=final)
    return x
```

```python
import functools
import math

import jax
import jax.numpy as jnp
from jax import lax
from jax.experimental import pallas as pl
from jax.experimental.pallas import tpu as pltpu

F32 = jnp.float32
BF16 = jnp.bfloat16

HEAD_DIM = 64
SSM_GROUP = 16
STATE = 64
CONV_W = 3
EPS = 1e-6

LANES = 128
SUBLANES = 8
Q_BLOCK = LANES
KEY_BLOCK = 64
KEY_ROWS = KEY_BLOCK // SUBLANES
PAST_BLOCKS = 3
WINDOW_BLOCKS = PAST_BLOCKS + Q_BLOCK // KEY_BLOCK
KEY_PAD = PAST_BLOCKS * KEY_BLOCK
WINDOW = WINDOW_BLOCKS * KEY_BLOCK
V_WINDOW = -(-WINDOW // LANES) * LANES
HALO = 16
SSM_CHUNK = LANES
VMEM_LIMIT = 56 * 1024 * 1024
MASKED_SCORE = 1e30
SATURATED_SCORE = 64.0

ROW_TILE = 512
SSM_TIME_TILE = 128
MOD_COL_TILE = 2048


def _params(*sem):
    return pltpu.CompilerParams(dimension_semantics=sem, vmem_limit_bytes=VMEM_LIMIT)


def _resident(shape):
    nd = len(shape)
    return pl.BlockSpec(shape, lambda *_: (0,) * nd, pipeline_mode=pl.Buffered(1))


def _modnorm(x, gain, shift, scale):
    ms = jnp.mean(x * x, axis=-1, keepdims=True)
    y = x * lax.rsqrt(ms + EPS) * gain
    return y * (1.0 + scale) + shift


def _mod_kernel(c_ref, w_ref, b_ref, o_ref):
    ca = jax.nn.silu(c_ref[...])
    o_ref[0] = jnp.dot(ca, w_ref[0], preferred_element_type=F32) + b_ref[0]


def _mod_project(c, w, b):
    nl, d, n = w.shape
    bsz = c.shape[0]
    tn = min(MOD_COL_TILE, n)
    return pl.pallas_call(
        _mod_kernel,
        grid=(nl, n // tn),
        in_specs=[
            pl.BlockSpec((bsz, d), lambda l, j: (0, 0)),
            pl.BlockSpec((1, d, tn), lambda l, j: (l, 0, j)),
            pl.BlockSpec((1, 1, tn), lambda l, j: (l, 0, j)),
        ],
        out_specs=pl.BlockSpec((1, bsz, tn), lambda l, j: (l, 0, j)),
        out_shape=jax.ShapeDtypeStruct((nl, bsz, n), F32),
        compiler_params=_params("arbitrary", "arbitrary"),
        name="adaln_project",
    )(c, w, b.reshape(nl, 1, n))


def _qkv_kernel(x_ref, mod_ref, gain_ref, wq_ref, wk_ref, wvt_ref, q_ref, k_ref, vt_ref, h_ref, hp_ref):
    tm = x_ref.shape[1]
    m = mod_ref[0]
    h = _modnorm(x_ref[0], gain_ref[...], m[0:1], m[1:2])
    q_ref[0] = jnp.dot(h.astype(BF16), wq_ref[...], preferred_element_type=F32).astype(q_ref.dtype)
    for c in range(h_ref.shape[0]):
        h_ref[c] = h[:, c * LANES:(c + 1) * LANES]
        for blk in range(tm // KEY_BLOCK):
            for r2 in range(KEY_ROWS // 2):
                base = blk * KEY_BLOCK
                lo = h_ref[c, pl.ds(base + 2 * r2, SUBLANES, stride=KEY_ROWS), :]
                hi = h_ref[c, pl.ds(base + 2 * r2 + 1, SUBLANES, stride=KEY_ROWS), :]
                hp_ref[base + 2 * SUBLANES * r2:base + 2 * SUBLANES * (r2 + 1), c * LANES:(c + 1) * LANES] = (
                    jnp.concatenate([lo, hi], axis=0).astype(BF16))
    hp = hp_ref[...]
    k_ref[0] = jnp.dot(hp, wk_ref[...], preferred_element_type=F32).astype(k_ref.dtype)
    vt_ref[0] = lax.dot_general(wvt_ref[...], hp, (((1,), (1,)), ((), ())),
                                preferred_element_type=F32).astype(vt_ref.dtype)


def _qkv_project(x, mod, gain, wq, wk, wvt):
    bsz, s, d = x.shape
    tm = min(ROW_TILE, s)
    row_spec = pl.BlockSpec((1, tm, d), lambda b, j: (b, j, 0))
    return pl.pallas_call(
        _qkv_kernel,
        grid=(bsz, s // tm),
        in_specs=[
            row_spec,
            pl.BlockSpec((1,) + mod.shape[1:], lambda b, j: (b, 0, 0)),
            _resident(gain.shape),
            _resident(wq.shape),
            _resident(wk.shape),
            _resident(wvt.shape),
        ],
        out_specs=[row_spec, row_spec, pl.BlockSpec((1, d, tm), lambda b, j: (b, 0, j))],
        out_shape=[jax.ShapeDtypeStruct((bsz, s, d), BF16), jax.ShapeDtypeStruct((bsz, s, d), BF16),
                   jax.ShapeDtypeStruct((bsz, d, s), BF16)],
        scratch_shapes=[pltpu.VMEM((d // LANES, tm, LANES), F32), pltpu.VMEM((tm, d), BF16)],
        compiler_params=_params("arbitrary", "arbitrary"),
        name="qkv_project",
    )(x, mod, gain, wq, wk, wvt)


def _norm_matmul_kernel(x_ref, mod_ref, gain_ref, w_ref, o_ref):
    m = mod_ref[0]
    h = _modnorm(x_ref[0], gain_ref[...], m[0:1], m[1:2]).astype(BF16)
    o_ref[...] = jnp.dot(h, w_ref[...], preferred_element_type=F32).astype(o_ref.dtype)


def _norm_project_time_major(x, mod, gain, w):
    bsz, s, d = x.shape
    n = w.shape[1]
    tm = min(ROW_TILE, s)
    return pl.pallas_call(
        _norm_matmul_kernel,
        grid=(bsz, s // tm),
        in_specs=[
            pl.BlockSpec((1, tm, d), lambda b, j: (b, j, 0)),
            pl.BlockSpec((1,) + mod.shape[1:], lambda b, j: (b, 0, 0)),
            _resident(gain.shape),
            _resident(w.shape),
        ],
        out_specs=pl.BlockSpec((tm, n), lambda b, j: (j, b)),
        out_shape=jax.ShapeDtypeStruct((s, bsz * n), F32),
        compiler_params=_params("arbitrary", "arbitrary"),
        name="norm_project",
    )(x, mod, gain, w)


def _streams(z_view, col, base, kq, s_view):
    run = jnp.ones((SUBLANES, Q_BLOCK), F32)
    for r in range(KEY_ROWS - 1, -1, -1):
        lo = base + SUBLANES * r
        zn = z_view[lo:lo + SUBLANES, col:col + Q_BLOCK]
        if kq is not None:
            zn = jnp.where(kq > r, zn, MASKED_SCORE)
        e = jnp.exp2(zn)
        a = run * (1.0 / (1.0 + e))
        s_view[lo:lo + SUBLANES, :] = a
        run = jnp.where(zn > SATURATED_SCORE, run, e * a)
    return run


def _suffix(tot, row):
    p = tot
    for d in (1, 2, 4):
        p = jnp.where(row + d < SUBLANES, p * pltpu.roll(p, SUBLANES - d, 0), p)
    excl = jnp.where(row < SUBLANES - 1, pltpu.roll(p, SUBLANES - 1, 0), 1.0)
    return excl, jnp.broadcast_to(p[0:1, :], p.shape)


def _apply_offsets(s_view, a_view, base, off):
    off2 = jnp.concatenate([off, off], axis=0)
    for r2 in range(KEY_BLOCK // 16):
        lo = base + 16 * r2
        a_view[lo:lo + 16, :] = (s_view[lo:lo + 16, :] * off2).astype(BF16)


def _nt_dot(a, b):
    return lax.dot_general(a, b, (((1,), (1,)), ((), ())), preferred_element_type=F32)


def _attn_kernel(q_ref, k_ref, vt_ref, o_ref, kp_ref, vp_ref, z_ref, s_ref, a_ref, t_ref, oacc_ref, c_ref,
                 flag_ref):
    s_len = q_ref.shape[1]
    n_heads = LANES // HEAD_DIM
    row = lax.broadcasted_iota(jnp.int32, (SUBLANES, Q_BLOCK), 0)
    lane = lax.broadcasted_iota(jnp.int32, (SUBLANES, Q_BLOCK), 1)
    kqs = [lane - KEY_ROWS * row - (b - PAST_BLOCKS) * KEY_BLOCK if b >= PAST_BLOCKS else None
           for b in range(WINDOW_BLOCKS)]
    q_lane = lax.broadcasted_iota(jnp.int32, (Q_BLOCK, LANES), 1)

    n_blocks = s_len // Q_BLOCK

    kp_ref[0:KEY_PAD, :] = jnp.zeros((KEY_PAD, LANES), kp_ref.dtype)
    kp_ref[KEY_PAD:, :] = k_ref[0]
    vp_ref[:, 0:KEY_PAD] = jnp.zeros((LANES, KEY_PAD), vp_ref.dtype)
    vp_ref[:, KEY_PAD:KEY_PAD + s_len] = vt_ref[0]
    vp_ref[:, KEY_PAD + s_len:] = jnp.zeros((LANES, V_WINDOW - WINDOW), vp_ref.dtype)

    def stacked_queries(w):
        q = q_ref[0, pl.ds(pl.multiple_of(w * Q_BLOCK, Q_BLOCK), Q_BLOCK), :]
        zero = jnp.zeros_like(q)
        return jnp.concatenate([jnp.where((q_lane >= h * HEAD_DIM) & (q_lane < (h + 1) * HEAD_DIM), q, zero)
                                for h in range(n_heads)], axis=0)

    def store_out(w):
        o_ref[0, pl.ds(pl.multiple_of(w * Q_BLOCK, Q_BLOCK), Q_BLOCK), :] = (
            oacc_ref[w].T.astype(o_ref.dtype))

    def scores(p, slot, i):
        w = jnp.minimum(2 * p + i, n_blocks - 1)
        k_win = kp_ref[pl.ds(pl.multiple_of(w * Q_BLOCK, Q_BLOCK), WINDOW), :]
        z_ref[slot, i] = _nt_dot(k_win, stacked_queries(w))

    def weights(p, slot, i):
        w = 2 * p + i
        for h in range(n_heads):
            z_view, s_view, a_view = z_ref.at[slot, i], s_ref.at[slot, i, h], a_ref.at[slot, i, h]
            tots = [_streams(z_view, h * Q_BLOCK, b * KEY_BLOCK, kqs[b], s_view)
                    for b in range(WINDOW_BLOCKS)]
            carry = None
            for b in range(WINDOW_BLOCKS - 1, -1, -1):
                excl, total = _suffix(tots[b], row)
                _apply_offsets(s_view, a_view, b * KEY_BLOCK, excl if carry is None else carry * excl)
                carry = total if carry is None else carry * total
            c_ref[w, h] = carry
        alive = jnp.max(jnp.maximum(c_ref[w, 0], c_ref[w, 1])) > 0.0
        flag_ref[w] = alive.astype(jnp.int32)

    def values(p, slot, i):
        w = jnp.maximum(2 * p + i, 0)
        k0 = pl.multiple_of(w * Q_BLOCK, Q_BLOCK)
        for h in range(n_heads):
            v_win = vp_ref[h * HEAD_DIM:(h + 1) * HEAD_DIM, pl.ds(k0, V_WINDOW)]
            oacc_ref[w, h * HEAD_DIM:(h + 1) * HEAD_DIM, :] = jnp.dot(v_win, a_ref[slot, i, h],
                                                                      preferred_element_type=F32)
        store_out(w)

    for i in range(2):
        scores(0, 0, i)
    a_ref[...] = jnp.zeros(a_ref.shape, a_ref.dtype)

    def two_pairs(m, _):
        p = 2 * m
        for p_w, slot in ((p, 0), (p + 1, 1)):
            for i in range(2):
                scores(p_w + 1, 1 - slot, i)
                weights(p_w, slot, i)
                values(p_w - 1, 1 - slot, i)
        return 0

    lax.fori_loop(0, n_blocks // 4, two_pairs, 0)
    for i in range(2):
        values(n_blocks // 2 - 1, 1, i)

    def older_keys(w):
        q2 = stacked_queries(w)
        n_sub = Q_BLOCK // KEY_BLOCK

        def cond(st):
            return jnp.logical_and(st[0] >= 1, st[1] > 0)

        def body(st):
            j = st[0]
            k0 = pl.multiple_of(j * Q_BLOCK, Q_BLOCK)
            z_ref[0, 0, 0:Q_BLOCK, :] = _nt_dot(kp_ref[pl.ds(k0, Q_BLOCK), :], q2)
            carries = []
            for h in range(n_heads):
                s_view, a_view = s_ref.at[0, 0, h], t_ref.at[h]
                tots = [_streams(z_ref.at[0, 0], h * Q_BLOCK, b * KEY_BLOCK, None, s_view) for b in range(n_sub)]
                carry = c_ref[w, h]
                for b in range(n_sub - 1, -1, -1):
                    excl, total = _suffix(tots[b], row)
                    _apply_offsets(s_view, a_view, b * KEY_BLOCK, carry * excl)
                    carry = carry * total
                v_blk = vp_ref[h * HEAD_DIM:(h + 1) * HEAD_DIM, pl.ds(k0, Q_BLOCK)]
                oacc_ref[w, h * HEAD_DIM:(h + 1) * HEAD_DIM, :] += jnp.dot(
                    v_blk, t_ref[h], preferred_element_type=F32)
                carries.append(carry)
                c_ref[w, h] = carry
            alive = jnp.max(functools.reduce(jnp.maximum, carries)) > 0.0
            return (j - 1, alive.astype(jnp.int32))

        lax.while_loop(cond, body, (w - 1, jnp.int32(1)))

    def finish(w, _):
        @pl.when(flag_ref[w] > 0)
        def _():
            older_keys(w)
            store_out(w)

        return 0

    lax.fori_loop(2, n_blocks, finish, 0)


def _attention(q, k_perm, vt_perm):
    bsz, s, d = q.shape
    n_heads = LANES // HEAD_DIM
    n_blocks = s // Q_BLOCK
    return pl.pallas_call(
        _attn_kernel,
        grid=(bsz, d // LANES),
        in_specs=[
            pl.BlockSpec((1, s, LANES), lambda b, h: (b, 0, h)),
            pl.BlockSpec((1, s, LANES), lambda b, h: (b, 0, h)),
            pl.BlockSpec((1, LANES, s), lambda b, h: (b, h, 0)),
        ],
        out_specs=pl.BlockSpec((1, s, LANES), lambda b, h: (b, 0, h)),
        out_shape=jax.ShapeDtypeStruct((bsz, s, d), BF16),
        scratch_shapes=[
            pltpu.VMEM((KEY_PAD + s, LANES), BF16),
            pltpu.VMEM((LANES, KEY_PAD + s + V_WINDOW - WINDOW), BF16),
            pltpu.VMEM((2, 2, WINDOW, n_heads * Q_BLOCK), F32),
            pltpu.VMEM((2, 2, n_heads, WINDOW, Q_BLOCK), F32),
            pltpu.VMEM((2, 2, n_heads, V_WINDOW, Q_BLOCK), BF16),
            pltpu.VMEM((n_heads, Q_BLOCK, Q_BLOCK), BF16),
            pltpu.VMEM((n_blocks, LANES, Q_BLOCK), F32),
            pltpu.VMEM((n_blocks, n_heads, SUBLANES, Q_BLOCK), F32),
            pltpu.SMEM((n_blocks,), jnp.int32),
        ],
        compiler_params=_params("arbitrary", "arbitrary"),
        name="stickbreak_attention",
    )(q, k_perm, vt_perm)


def _proj_residual_kernel(a_ref, w_ref, x_ref, mod_ref, o_ref, *, gate_row):
    y = jnp.dot(a_ref[0], w_ref[...], preferred_element_type=F32)
    o_ref[0] = x_ref[0] + mod_ref[0][gate_row:gate_row + 1] * y


def _proj_residual(a, w, x, mod, gate_row):
    bsz, s, d = x.shape
    tm = min(ROW_TILE, s)
    return pl.pallas_call(
        functools.partial(_proj_residual_kernel, gate_row=gate_row),
        grid=(bsz, s // tm),
        in_specs=[
            pl.BlockSpec((1, tm, a.shape[-1]), lambda b, j: (b, j, 0)),
            _resident(w.shape),
            pl.BlockSpec((1, tm, d), lambda b, j: (b, j, 0)),
            pl.BlockSpec((1,) + mod.shape[1:], lambda b, j: (b, 0, 0)),
        ],
        out_specs=pl.BlockSpec((1, tm, d), lambda b, j: (b, j, 0)),
        out_shape=jax.ShapeDtypeStruct(x.shape, F32),
        compiler_params=_params("arbitrary", "arbitrary"),
        name="proj_residual",
    )(a, w, x, mod)


def _ssm_kernel(u_ref, bmat_ref, cmat_ref, lam_ref, d_ref, z_ref, xs_ref, st_ref, *, nb, row_chunk):
    rows = u_ref.shape[0]
    half = st_ref.shape[1] // 2

    @pl.when(pl.program_id(1) == 0)
    def _():
        st_ref[...] = jnp.zeros_like(st_ref)

    for r0 in range(0, rows, row_chunk):
        xs_ref[r0:r0 + row_chunk, :] = jnp.dot(u_ref[r0:r0 + row_chunk, :].astype(BF16), bmat_ref[0],
                                               preferred_element_type=F32)

    a_re = jnp.broadcast_to(lam_ref[0, 0:1, :], (nb, half))
    a_im = jnp.broadcast_to(lam_ref[0, 1:2, :], (nb, half))

    def step(t, st):
        x_re, x_im = st
        r = pl.multiple_of(t * nb, nb)
        n_re = a_re * x_re - a_im * x_im + xs_ref[pl.ds(r, nb), :half]
        n_im = a_re * x_im + a_im * x_re + xs_ref[pl.ds(r, nb), half:]
        xs_ref[pl.ds(r, nb), :half] = n_re
        xs_ref[pl.ds(r, nb), half:] = n_im
        return n_re, n_im

    x_re, x_im = lax.fori_loop(0, rows // nb, step, (st_ref[:, :half], st_ref[:, half:]), unroll=2)
    st_ref[:, :half] = x_re
    st_ref[:, half:] = x_im

    for r0 in range(0, rows, row_chunk):
        y = jnp.dot(xs_ref[r0:r0 + row_chunk, :].astype(BF16), cmat_ref[0], preferred_element_type=F32)
        z_ref[r0:r0 + row_chunk, :] = jax.nn.gelu(y + d_ref[...] * u_ref[r0:r0 + row_chunk, :])


def _ssm(u_tm, bmat, cmat, lam, d_skip, nb):
    rows_total, d = u_tm.shape
    s = rows_total // nb
    tt = min(SSM_TIME_TILE, s)
    rows = tt * nb
    n_state = bmat.shape[-1]
    return pl.pallas_call(
        functools.partial(_ssm_kernel, nb=nb, row_chunk=min(256, rows)),
        grid=(d // SSM_CHUNK, s // tt),
        in_specs=[
            pl.BlockSpec((rows, SSM_CHUNK), lambda c, i: (i, c)),
            pl.BlockSpec((1, SSM_CHUNK, n_state), lambda c, i: (c, 0, 0)),
            pl.BlockSpec((1, n_state, SSM_CHUNK), lambda c, i: (c, 0, 0)),
            pl.BlockSpec((1, 2, n_state // 2), lambda c, i: (c, 0, 0)),
            pl.BlockSpec((1, SSM_CHUNK), lambda c, i: (0, c)),
        ],
        out_specs=pl.BlockSpec((rows, SSM_CHUNK), lambda c, i: (i, c)),
        out_shape=jax.ShapeDtypeStruct(u_tm.shape, F32),
        scratch_shapes=[pltpu.VMEM((rows, n_state), F32), pltpu.VMEM((nb, n_state), F32)],
        compiler_params=_params("arbitrary", "arbitrary"),
        name="s5_scan",
    )(u_tm, bmat, cmat, lam, d_skip)


def _ssm_matrices(a_re, a_im, log_dt, b_re, b_im, c_re, c_im):
    g, p = a_re.shape
    hch = b_re.shape[-1]
    gpc = SSM_CHUNK // hch
    nc = g // gpc
    lam = lax.complex(a_re.astype(F32), a_im.astype(F32))
    dt = jnp.exp(log_dt.astype(F32))[:, None]
    lam_bar = jnp.exp(lam * dt)
    b_bar = ((lam_bar - 1) / lam)[..., None] * lax.complex(b_re.astype(F32), b_im.astype(F32))
    eye = jnp.eye(gpc, dtype=F32)

    def in_blocks(m):
        m = m.reshape(nc, gpc, p, hch).transpose(0, 1, 3, 2)
        return jnp.einsum('cghp,gk->cghkp', m, eye).reshape(nc, gpc * hch, gpc * p)

    def out_blocks(m):
        m = m.reshape(nc, gpc, hch, p).transpose(0, 1, 3, 2)
        return jnp.einsum('cgph,gk->cgpkh', m, eye).reshape(nc, gpc * p, gpc * hch)

    bmat = jnp.concatenate([in_blocks(jnp.real(b_bar)), in_blocks(jnp.imag(b_bar))], axis=-1)
    cmat = jnp.concatenate([out_blocks(c_re.astype(F32)), out_blocks(-c_im.astype(F32))], axis=1)
    lam_ri = jnp.stack([jnp.real(lam_bar).reshape(nc, gpc * p), jnp.imag(lam_bar).reshape(nc, gpc * p)], axis=1)
    return bmat.astype(BF16), cmat.astype(BF16), lam_ri


def _glu_kernel(z_ref, wg_ref, bg_ref, wo_ref, x_ref, mod_ref, o_ref):
    z = z_ref[...]
    a = jnp.dot(z.astype(BF16), wg_ref[...], preferred_element_type=F32) + bg_ref[...]
    g = z * jax.nn.sigmoid(a)
    y = jnp.dot(g.astype(BF16), wo_ref[...], preferred_element_type=F32)
    o_ref[0] = x_ref[0] + mod_ref[0][2:3] * y


def _glu_residual(z_tm, w_glu, b_glu, w_o, x, mod):
    bsz, s, d = x.shape
    tm = min(ROW_TILE, s)
    return pl.pallas_call(
        _glu_kernel,
        grid=(bsz, s // tm),
        in_specs=[
            pl.BlockSpec((tm, d), lambda b, j: (j, b)),
            _resident(w_glu.shape),
            _resident(b_glu.shape),
            _resident(w_o.shape),
            pl.BlockSpec((1, tm, d), lambda b, j: (b, j, 0)),
            pl.BlockSpec((1,) + mod.shape[1:], lambda b, j: (b, 0, 0)),
        ],
        out_specs=pl.BlockSpec((1, tm, d), lambda b, j: (b, j, 0)),
        out_shape=jax.ShapeDtypeStruct(x.shape, F32),
        compiler_params=_params("arbitrary", "arbitrary"),
        name="glu_residual",
    )(z_tm, w_glu, b_glu, w_o, x, mod)


def _ffn_kernel(x_ref, xh_ref, mod_ref, gain_ref, wup_ref, cw_ref, cb_ref, wdn_ref, *rest, n_chunks, final):
    if final:
        fmod_ref, fgain_ref, o_ref, hs_ref, ug_ref, uv_ref, acc_ref = rest
    else:
        o_ref, hs_ref, ug_ref, uv_ref, acc_ref = rest
    tm = x_ref.shape[1]
    f = wdn_ref.shape[0]
    tf = f // n_chunks
    m = mod_ref[0]
    gain = gain_ref[...]
    x = x_ref[0]
    h_halo = _modnorm(xh_ref[0], gain, m[3:4], m[4:5])
    h_halo = jnp.where(pl.program_id(1) > 0, h_halo, 0.0)
    hs_ref[0:HALO, :] = h_halo.astype(BF16)
    hs_ref[HALO:, :] = _modnorm(x, gain, m[3:4], m[4:5]).astype(BF16)

    def conv(u_ref, c0):
        w = cw_ref[:, c0:c0 + tf]
        return (w[2:3] * u_ref[pl.ds(HALO, tm), :] + w[1:2] * u_ref[pl.ds(HALO - 1, tm), :]
                + w[0:1] * u_ref[pl.ds(HALO - 2, tm), :] + cb_ref[:, c0:c0 + tf])

    for c in range(n_chunks):
        g0 = c * tf
        v0 = f + c * tf
        hs = hs_ref[...]
        ug_ref[...] = jnp.dot(hs, wup_ref[:, g0:g0 + tf], preferred_element_type=F32)
        uv_ref[...] = jnp.dot(hs, wup_ref[:, v0:v0 + tf], preferred_element_type=F32)
        act = (jax.nn.silu(conv(ug_ref, g0)) * conv(uv_ref, v0)).astype(BF16)
        y = jnp.dot(act, wdn_ref[g0:g0 + tf, :], preferred_element_type=F32)
        if c == 0:
            acc_ref[...] = y
        else:
            acc_ref[...] += y
    out = x + m[5:6] * acc_ref[...]
    if final:
        fm = fmod_ref[0]
        out = _modnorm(out, fgain_ref[...], fm[0:1], fm[1:2])
    o_ref[0] = out


def _conv_ffn(x, mod, gain, w_up, conv_w, conv_b, w_down, final=None):
    bsz, s, d = x.shape
    f = w_down.shape[0]
    tm = min(ROW_TILE, s)
    n_chunks = 2
    tf = f // n_chunks
    halo_blocks = tm // HALO
    in_specs = [
        pl.BlockSpec((1, tm, d), lambda b, j: (b, j, 0)),
        pl.BlockSpec((1, HALO, d), lambda b, j: (b, jnp.maximum(j * halo_blocks - 1, 0), 0)),
        pl.BlockSpec((1,) + mod.shape[1:], lambda b, j: (b, 0, 0)),
        _resident(gain.shape),
        _resident(w_up.shape),
        _resident(conv_w.shape),
        _resident(conv_b.shape),
        _resident(w_down.shape),
    ]
    args = [x, x, mod, gain, w_up, conv_w, conv_b, w_down]
    if final is not None:
        in_specs += [pl.BlockSpec((1,) + final[0].shape[1:], lambda b, j: (b, 0, 0)), _resident(final[1].shape)]
        args += list(final)
    return pl.pallas_call(
        functools.partial(_ffn_kernel, n_chunks=n_chunks, final=final is not None),
        grid=(bsz, s // tm),
        in_specs=in_specs,
        out_specs=pl.BlockSpec((1, tm, d), lambda b, j: (b, j, 0)),
        out_shape=jax.ShapeDtypeStruct(x.shape, F32),
        scratch_shapes=[
            pltpu.VMEM((tm + HALO, d), BF16),
            pltpu.VMEM((tm + HALO, tf), F32),
            pltpu.VMEM((tm + HALO, tf), F32),
            pltpu.VMEM((tm, d), F32),
        ],
        compiler_params=_params("arbitrary", "arbitrary"),
        name="conv_ffn",
    )(*args)


def kernel(x, c, norm_mix, norm_ffn, w_mod, b_mod, w_qkv, w_o_attn, w_in_ssm, a_re, a_im, log_dt, b_re, b_im, c_re, c_im, d_skip, w_glu, b_glu, w_o_ssm, w_up, conv_w, conv_b, w_down, norm_out, w_fin, b_fin):
    bsz, s, d = x.shape
    depth = w_mod.shape[0]
    assert s % (4 * Q_BLOCK) == 0 and d % LANES == 0 and bsz % SUBLANES == 0

    mods = _mod_project(c, w_mod, b_mod).reshape(depth, bsz, 6, d)
    fin_mod = _mod_project(c, w_fin[None], b_fin[None]).reshape(bsz, 2, d)

    for i in range(depth):
        mod = mods[i]
        j = i // 2
        gain_mix = norm_mix[i][None]
        if i % 2 == 0:
            wq = (w_qkv[j][:, :d] * (-math.log2(math.e) * HEAD_DIM ** -0.5)).astype(BF16)
            wk = w_qkv[j][:, d:2 * d].astype(BF16)
            wvt = w_qkv[j][:, 2 * d:].T.astype(BF16)
            q, k_perm, vt_perm = _qkv_project(x, mod, gain_mix, wq, wk, wvt)
            o = _attention(q, k_perm, vt_perm)
            x = _proj_residual(o, w_o_attn[j].astype(BF16), x, mod, gate_row=2)
        else:
            u_tm = _norm_project_time_major(x, mod, gain_mix, w_in_ssm[j].astype(BF16))
            bmat, cmat, lam = _ssm_matrices(a_re[j], a_im[j], log_dt[j], b_re[j], b_im[j], c_re[j], c_im[j])
            z_tm = _ssm(u_tm.reshape(s * bsz, d), bmat, cmat, lam, d_skip[j][None], nb=bsz)
            x = _glu_residual(z_tm.reshape(s, bsz * d), w_glu[j].astype(BF16), b_glu[j][None],
                              w_o_ssm[j].astype(BF16), x, mod)
        final = (fin_mod, norm_out[None]) if i == depth - 1 else None
        x = _conv_ffn(x, mod, norm_ffn[i][None], w_up[i].astype(BF16), conv_w[i], conv_b[i][None],
                      w_down[i].astype(BF16), final=final)
    return x
```

```python
import functools
import math

import jax
import jax.numpy as jnp
from jax import lax
from jax.experimental import pallas as pl
from jax.experimental.pallas import tpu as pltpu

F32 = jnp.float32
BF16 = jnp.bfloat16

HEAD_DIM = 64
SSM_GROUP = 16
STATE = 64
CONV_W = 3
EPS = 1e-6

LANES = 128
SUBLANES = 8
MXU_COLS = 256
Q_BLOCK = LANES
KEY_BLOCK = 64
KEY_ROWS = KEY_BLOCK // SUBLANES
PAST_BLOCKS = 3
WINDOW_BLOCKS = PAST_BLOCKS + Q_BLOCK // KEY_BLOCK
KEY_PAD = PAST_BLOCKS * KEY_BLOCK
WINDOW = WINDOW_BLOCKS * KEY_BLOCK
V_WINDOW = -(-WINDOW // LANES) * LANES
HALO = 16
SSM_CHUNK = LANES
VMEM_LIMIT = 56 * 1024 * 1024
MASKED_SCORE = 1e30
SATURATED_SCORE = 64.0

ROW_TILE = 512
FFN_CHUNKS = 2
TIME_TILE = 32
SSM_TIME_TILE = 128
MOD_COL_TILE = 2048


def _params(*sem):
    return pltpu.CompilerParams(dimension_semantics=sem, vmem_limit_bytes=VMEM_LIMIT)


def _resident(shape):
    nd = len(shape)
    return pl.BlockSpec(shape, lambda *_: (0,) * nd, pipeline_mode=pl.Buffered(1))


def _modnorm(x, gain, shift, scale):
    ms = jnp.mean(x * x, axis=-1, keepdims=True)
    y = x * lax.rsqrt(ms + EPS) * gain
    return y * (1.0 + scale) + shift


def _mod_kernel(c_ref, w_ref, b_ref, o_ref):
    ca = jax.nn.silu(c_ref[...])
    o_ref[0] = jnp.dot(ca, w_ref[0], preferred_element_type=F32) + b_ref[0]


def _mod_project(c, w, b):
    nl, d, n = w.shape
    bsz = c.shape[0]
    tn = min(MOD_COL_TILE, n)
    return pl.pallas_call(
        _mod_kernel,
        grid=(nl, n // tn),
        in_specs=[
            pl.BlockSpec((bsz, d), lambda l, j: (0, 0)),
            pl.BlockSpec((1, d, tn), lambda l, j: (l, 0, j)),
            pl.BlockSpec((1, 1, tn), lambda l, j: (l, 0, j)),
        ],
        out_specs=pl.BlockSpec((1, bsz, tn), lambda l, j: (l, 0, j)),
        out_shape=jax.ShapeDtypeStruct((nl, bsz, n), F32),
        compiler_params=_params("arbitrary", "arbitrary"),
        name="adaln_project",
    )(c, w, b.reshape(nl, 1, n))


def _qkv_kernel(x_ref, mod_ref, gain_ref, wq_ref, wk_ref, wvt_ref, q_ref, k_ref, vt_ref, h_ref, hp_ref):
    tm = x_ref.shape[1]
    m = mod_ref[0]
    h = _modnorm(x_ref[0], gain_ref[...], m[0:1], m[1:2])
    q_ref[0] = jnp.dot(h.astype(BF16), wq_ref[...], preferred_element_type=F32).astype(q_ref.dtype)
    for c in range(h_ref.shape[0]):
        h_ref[c] = h[:, c * LANES:(c + 1) * LANES]
        for blk in range(tm // KEY_BLOCK):
            for r2 in range(KEY_ROWS // 2):
                base = blk * KEY_BLOCK
                lo = h_ref[c, pl.ds(base + 2 * r2, SUBLANES, stride=KEY_ROWS), :]
                hi = h_ref[c, pl.ds(base + 2 * r2 + 1, SUBLANES, stride=KEY_ROWS), :]
                hp_ref[base + 2 * SUBLANES * r2:base + 2 * SUBLANES * (r2 + 1), c * LANES:(c + 1) * LANES] = (
                    jnp.concatenate([lo, hi], axis=0).astype(BF16))
    hp = hp_ref[...]
    k_ref[0] = jnp.dot(hp, wk_ref[...], preferred_element_type=F32).astype(k_ref.dtype)
    vt_ref[0] = lax.dot_general(wvt_ref[...], hp, (((1,), (1,)), ((), ())),
                                preferred_element_type=F32).astype(vt_ref.dtype)


def _qkv_project(x, mod, gain, wq, wk, wvt):
    bsz, s, d = x.shape
    tm = min(ROW_TILE, s)
    row_spec = pl.BlockSpec((1, tm, d), lambda b, j: (b, j, 0))
    return pl.pallas_call(
        _qkv_kernel,
        grid=(bsz, s // tm),
        in_specs=[
            row_spec,
            pl.BlockSpec((1,) + mod.shape[1:], lambda b, j: (b, 0, 0)),
            _resident(gain.shape),
            _resident(wq.shape),
            _resident(wk.shape),
            _resident(wvt.shape),
        ],
        out_specs=[row_spec, row_spec, pl.BlockSpec((1, d, tm), lambda b, j: (b, 0, j))],
        out_shape=[jax.ShapeDtypeStruct((bsz, s, d), BF16), jax.ShapeDtypeStruct((bsz, s, d), BF16),
                   jax.ShapeDtypeStruct((bsz, d, s), BF16)],
        scratch_shapes=[pltpu.VMEM((d // LANES, tm, LANES), F32), pltpu.VMEM((tm, d), BF16)],
        compiler_params=_params("arbitrary", "arbitrary"),
        name="qkv_project",
    )(x, mod, gain, wq, wk, wvt)


def _pitch(rows):
    p = -(-rows // SUBLANES)
    return SUBLANES * (p if p % 2 else p + 1)


def _norm_matmul_kernel(x_ref, mod_ref, gain_ref, w_ref, o_ref, hs_ref, slab_ref):
    bsz, tt, d = x_ref.shape
    pitch = slab_ref.shape[1] // bsz
    for b in range(bsz):
        h = _modnorm(x_ref[b], gain_ref[...], mod_ref[b, 0:1], mod_ref[b, 1:2])
        hs_ref[b * tt:(b + 1) * tt, :] = h.astype(BF16)
    u = jnp.dot(hs_ref[...], w_ref[...], preferred_element_type=F32)
    for c in range(d // LANES):
        for b in range(bsz):
            slab_ref[c, b * pitch:b * pitch + tt, :] = u[b * tt:(b + 1) * tt, c * LANES:(c + 1) * LANES]
        for t in range(tt):
            o_ref[t * bsz:(t + 1) * bsz, c * LANES:(c + 1) * LANES] = slab_ref[c, pl.ds(t, bsz, stride=pitch), :]


def _norm_project_time_major(x, mod, gain, w):
    bsz, s, d = x.shape
    n = w.shape[1]
    tt = min(TIME_TILE, s)
    pitch = _pitch(tt)
    return pl.pallas_call(
        _norm_matmul_kernel,
        grid=(s // tt,),
        in_specs=[
            pl.BlockSpec((bsz, tt, d), lambda j: (0, j, 0)),
            _resident(mod.shape),
            _resident(gain.shape),
            _resident(w.shape),
        ],
        out_specs=pl.BlockSpec((tt * bsz, n), lambda j: (j, 0)),
        out_shape=jax.ShapeDtypeStruct((s * bsz, n), F32),
        scratch_shapes=[pltpu.VMEM((bsz * tt, d), BF16), pltpu.VMEM((n // LANES, bsz * pitch, LANES), F32)],
        compiler_params=_params("arbitrary"),
        name="norm_project",
    )(x, mod, gain, w)


def _streams(z_view, col, base, kq, s_view):
    run = jnp.ones((SUBLANES, Q_BLOCK), F32)
    for r in range(KEY_ROWS - 1, -1, -1):
        lo = base + SUBLANES * r
        zn = z_view[lo:lo + SUBLANES, col:col + Q_BLOCK]
        if kq is not None:
            zn = jnp.where(kq > r, zn, MASKED_SCORE)
        e = jnp.exp2(zn)
        a = run * (1.0 / (1.0 + e))
        s_view[lo:lo + SUBLANES, :] = a
        run = jnp.where(zn > SATURATED_SCORE, run, e * a)
    return run


def _suffix(tot, row):
    p = tot
    for d in (1, 2, 4):
        p = jnp.where(row + d < SUBLANES, p * pltpu.roll(p, SUBLANES - d, 0), p)
    excl = jnp.where(row < SUBLANES - 1, pltpu.roll(p, SUBLANES - 1, 0), 1.0)
    return excl, jnp.broadcast_to(p[0:1, :], p.shape)


def _apply_offsets(s_view, a_view, base, off):
    off2 = jnp.concatenate([off, off], axis=0)
    for r2 in range(KEY_BLOCK // 16):
        lo = base + 16 * r2
        a_view[lo:lo + 16, :] = (s_view[lo:lo + 16, :] * off2).astype(BF16)


def _nt_dot(a, b):
    return lax.dot_general(a, b, (((1,), (1,)), ((), ())), preferred_element_type=F32)


def _attn_kernel(q_ref, k_ref, vt_ref, o_ref, kp_ref, vp_ref, z_ref, s_ref, a_ref, t_ref, oacc_ref, c_ref,
                 flag_ref):
    s_len = q_ref.shape[1]
    n_heads = LANES // HEAD_DIM
    row = lax.broadcasted_iota(jnp.int32, (SUBLANES, Q_BLOCK), 0)
    lane = lax.broadcasted_iota(jnp.int32, (SUBLANES, Q_BLOCK), 1)
    kqs = [lane - KEY_ROWS * row - (b - PAST_BLOCKS) * KEY_BLOCK if b >= PAST_BLOCKS else None
           for b in range(WINDOW_BLOCKS)]
    q_lane = lax.broadcasted_iota(jnp.int32, (Q_BLOCK, LANES), 1)

    n_blocks = s_len // Q_BLOCK

    kp_ref[0:KEY_PAD, :] = jnp.zeros((KEY_PAD, LANES), kp_ref.dtype)
    kp_ref[KEY_PAD:, :] = k_ref[0]
    vp_ref[:, 0:KEY_PAD] = jnp.zeros((LANES, KEY_PAD), vp_ref.dtype)
    vp_ref[:, KEY_PAD:KEY_PAD + s_len] = vt_ref[0]
    vp_ref[:, KEY_PAD + s_len:] = jnp.zeros((LANES, V_WINDOW - WINDOW), vp_ref.dtype)

    def stacked_queries(w):
        q = q_ref[0, pl.ds(pl.multiple_of(w * Q_BLOCK, Q_BLOCK), Q_BLOCK), :]
        zero = jnp.zeros_like(q)
        return jnp.concatenate([jnp.where((q_lane >= h * HEAD_DIM) & (q_lane < (h + 1) * HEAD_DIM), q, zero)
                                for h in range(n_heads)], axis=0)

    def store_out(w):
        o_ref[0, pl.ds(pl.multiple_of(w * Q_BLOCK, Q_BLOCK), Q_BLOCK), :] = (
            oacc_ref[w].T.astype(o_ref.dtype))

    def scores(p, slot, i):
        w = jnp.minimum(2 * p + i, n_blocks - 1)
        k_win = kp_ref[pl.ds(pl.multiple_of(w * Q_BLOCK, Q_BLOCK), WINDOW), :]
        z_ref[slot, i] = _nt_dot(k_win, stacked_queries(w))

    def weights(p, slot, i):
        w = 2 * p + i
        for h in range(n_heads):
            z_view, s_view, a_view = z_ref.at[slot, i], s_ref.at[slot, i, h], a_ref.at[slot, i, h]
            tots = [_streams(z_view, h * Q_BLOCK, b * KEY_BLOCK, kqs[b], s_view)
                    for b in range(WINDOW_BLOCKS)]
            carry = None
            for b in range(WINDOW_BLOCKS - 1, -1, -1):
                excl, total = _suffix(tots[b], row)
                _apply_offsets(s_view, a_view, b * KEY_BLOCK, excl if carry is None else carry * excl)
                carry = total if carry is None else carry * total
            c_ref[w, h] = carry
        alive = jnp.max(jnp.maximum(c_ref[w, 0], c_ref[w, 1])) > 0.0
        flag_ref[w] = alive.astype(jnp.int32)

    def values(p, slot, i):
        w = jnp.maximum(2 * p + i, 0)
        k0 = pl.multiple_of(w * Q_BLOCK, Q_BLOCK)
        for h in range(n_heads):
            v_win = vp_ref[h * HEAD_DIM:(h + 1) * HEAD_DIM, pl.ds(k0, V_WINDOW)]
            oacc_ref[w, h * HEAD_DIM:(h + 1) * HEAD_DIM, :] = jnp.dot(v_win, a_ref[slot, i, h],
                                                                      preferred_element_type=F32)
        store_out(w)

    for i in range(2):
        scores(0, 0, i)
    a_ref[...] = jnp.zeros(a_ref.shape, a_ref.dtype)

    def two_pairs(m, _):
        p = 2 * m
        for p_w, slot in ((p, 0), (p + 1, 1)):
            for i in range(2):
                scores(p_w + 1, 1 - slot, i)
                weights(p_w, slot, i)
                values(p_w - 1, 1 - slot, i)
        return 0

    lax.fori_loop(0, n_blocks // 4, two_pairs, 0)
    for i in range(2):
        values(n_blocks // 2 - 1, 1, i)

    def older_keys(w):
        q2 = stacked_queries(w)
        n_sub = Q_BLOCK // KEY_BLOCK

        def cond(st):
            return jnp.logical_and(st[0] >= 1, st[1] > 0)

        def body(st):
            j = st[0]
            k0 = pl.multiple_of(j * Q_BLOCK, Q_BLOCK)
            z_ref[0, 0, 0:Q_BLOCK, :] = _nt_dot(kp_ref[pl.ds(k0, Q_BLOCK), :], q2)
            carries = []
            for h in range(n_heads):
                s_view, a_view = s_ref.at[0, 0, h], t_ref.at[h]
                tots = [_streams(z_ref.at[0, 0], h * Q_BLOCK, b * KEY_BLOCK, None, s_view) for b in range(n_sub)]
                carry = c_ref[w, h]
                for b in range(n_sub - 1, -1, -1):
                    excl, total = _suffix(tots[b], row)
                    _apply_offsets(s_view, a_view, b * KEY_BLOCK, carry * excl)
                    carry = carry * total
                v_blk = vp_ref[h * HEAD_DIM:(h + 1) * HEAD_DIM, pl.ds(k0, Q_BLOCK)]
                oacc_ref[w, h * HEAD_DIM:(h + 1) * HEAD_DIM, :] += jnp.dot(
                    v_blk, t_ref[h], preferred_element_type=F32)
                carries.append(carry)
                c_ref[w, h] = carry
            alive = jnp.max(functools.reduce(jnp.maximum, carries)) > 0.0
            return (j - 1, alive.astype(jnp.int32))

        lax.while_loop(cond, body, (w - 1, jnp.int32(1)))

    def finish(w, _):
        @pl.when(flag_ref[w] > 0)
        def _():
            older_keys(w)
            store_out(w)

        return 0

    lax.fori_loop(2, n_blocks, finish, 0)


def _attention(q, k_perm, vt_perm):
    bsz, s, d = q.shape
    n_heads = LANES // HEAD_DIM
    n_blocks = s // Q_BLOCK
    return pl.pallas_call(
        _attn_kernel,
        grid=(bsz, d // LANES),
        in_specs=[
            pl.BlockSpec((1, s, LANES), lambda b, h: (b, 0, h)),
            pl.BlockSpec((1, s, LANES), lambda b, h: (b, 0, h)),
            pl.BlockSpec((1, LANES, s), lambda b, h: (b, h, 0)),
        ],
        out_specs=pl.BlockSpec((1, s, LANES), lambda b, h: (b, 0, h)),
        out_shape=jax.ShapeDtypeStruct((bsz, s, d), BF16),
        scratch_shapes=[
            pltpu.VMEM((KEY_PAD + s, LANES), BF16),
            pltpu.VMEM((LANES, KEY_PAD + s + V_WINDOW - WINDOW), BF16),
            pltpu.VMEM((2, 2, WINDOW, n_heads * Q_BLOCK), F32),
            pltpu.VMEM((2, 2, n_heads, WINDOW, Q_BLOCK), F32),
            pltpu.VMEM((2, 2, n_heads, V_WINDOW, Q_BLOCK), BF16),
            pltpu.VMEM((n_heads, Q_BLOCK, Q_BLOCK), BF16),
            pltpu.VMEM((n_blocks, LANES, Q_BLOCK), F32),
            pltpu.VMEM((n_blocks, n_heads, SUBLANES, Q_BLOCK), F32),
            pltpu.SMEM((n_blocks,), jnp.int32),
        ],
        compiler_params=_params("arbitrary", "arbitrary"),
        name="stickbreak_attention",
    )(q, k_perm, vt_perm)


def _proj_residual_kernel(a_ref, w_ref, x_ref, mod_ref, o_ref, *, gate_row):
    y = jnp.dot(a_ref[0], w_ref[...], preferred_element_type=F32)
    o_ref[0] = x_ref[0] + mod_ref[0][gate_row:gate_row + 1] * y


def _proj_residual(a, w, x, mod, gate_row):
    bsz, s, d = x.shape
    tm = min(ROW_TILE, s)
    return pl.pallas_call(
        functools.partial(_proj_residual_kernel, gate_row=gate_row),
        grid=(bsz, s // tm),
        in_specs=[
            pl.BlockSpec((1, tm, a.shape[-1]), lambda b, j: (b, j, 0)),
            _resident(w.shape),
            pl.BlockSpec((1, tm, d), lambda b, j: (b, j, 0)),
            pl.BlockSpec((1,) + mod.shape[1:], lambda b, j: (b, 0, 0)),
        ],
        out_specs=pl.BlockSpec((1, tm, d), lambda b, j: (b, j, 0)),
        out_shape=jax.ShapeDtypeStruct(x.shape, F32),
        compiler_params=_params("arbitrary", "arbitrary"),
        name="proj_residual",
    )(a, w, x, mod)


def _ssm_kernel(u_ref, bmat_ref, cmat_ref, lam_ref, d_ref, z_ref, xs_ref, st_ref, *, nb, row_chunk):
    rows = u_ref.shape[0]
    half = st_ref.shape[1] // 2

    @pl.when(pl.program_id(1) == 0)
    def _():
        st_ref[...] = jnp.zeros_like(st_ref)

    for r0 in range(0, rows, row_chunk):
        xs_ref[r0:r0 + row_chunk, :] = jnp.dot(u_ref[r0:r0 + row_chunk, :].astype(BF16), bmat_ref[0],
                                               preferred_element_type=F32)

    a_re = jnp.broadcast_to(lam_ref[0, 0:1, :], (nb, half))
    a_im = jnp.broadcast_to(lam_ref[0, 1:2, :], (nb, half))

    def step(t, st):
        x_re, x_im = st
        r = pl.multiple_of(t * nb, nb)
        n_re = a_re * x_re - a_im * x_im + xs_ref[pl.ds(r, nb), :half]
        n_im = a_re * x_im + a_im * x_re + xs_ref[pl.ds(r, nb), half:]
        xs_ref[pl.ds(r, nb), :half] = n_re
        xs_ref[pl.ds(r, nb), half:] = n_im
        return n_re, n_im

    x_re, x_im = lax.fori_loop(0, rows // nb, step, (st_ref[:, :half], st_ref[:, half:]), unroll=2)
    st_ref[:, :half] = x_re
    st_ref[:, half:] = x_im

    for r0 in range(0, rows, row_chunk):
        y = jnp.dot(xs_ref[r0:r0 + row_chunk, :].astype(BF16), cmat_ref[0], preferred_element_type=F32)
        z_ref[r0:r0 + row_chunk, :] = jax.nn.gelu(y + d_ref[...] * u_ref[r0:r0 + row_chunk, :])


def _ssm(u_tm, bmat, cmat, lam, d_skip, nb):
    rows_total, d = u_tm.shape
    s = rows_total // nb
    tt = min(SSM_TIME_TILE, s)
    rows = tt * nb
    n_state = bmat.shape[-1]
    return pl.pallas_call(
        functools.partial(_ssm_kernel, nb=nb, row_chunk=min(256, rows)),
        grid=(d // SSM_CHUNK, s // tt),
        in_specs=[
            pl.BlockSpec((rows, SSM_CHUNK), lambda c, i: (i, c)),
            pl.BlockSpec((1, SSM_CHUNK, n_state), lambda c, i: (c, 0, 0)),
            pl.BlockSpec((1, n_state, SSM_CHUNK), lambda c, i: (c, 0, 0)),
            pl.BlockSpec((1, 2, n_state // 2), lambda c, i: (c, 0, 0)),
            pl.BlockSpec((1, SSM_CHUNK), lambda c, i: (0, c)),
        ],
        out_specs=pl.BlockSpec((rows, SSM_CHUNK), lambda c, i: (i, c)),
        out_shape=jax.ShapeDtypeStruct(u_tm.shape, F32),
        scratch_shapes=[pltpu.VMEM((rows, n_state), F32), pltpu.VMEM((nb, n_state), F32)],
        compiler_params=_params("arbitrary", "arbitrary"),
        name="s5_scan",
    )(u_tm, bmat, cmat, lam, d_skip)


def _ssm_matrices(a_re, a_im, log_dt, b_re, b_im, c_re, c_im):
    g, p = a_re.shape
    hch = b_re.shape[-1]
    gpc = SSM_CHUNK // hch
    nc = g // gpc
    lam = lax.complex(a_re.astype(F32), a_im.astype(F32))
    dt = jnp.exp(log_dt.astype(F32))[:, None]
    lam_bar = jnp.exp(lam * dt)
    b_bar = ((lam_bar - 1) / lam)[..., None] * lax.complex(b_re.astype(F32), b_im.astype(F32))
    eye = jnp.eye(gpc, dtype=F32)

    def in_blocks(m):
        m = m.reshape(nc, gpc, p, hch).transpose(0, 1, 3, 2)
        return jnp.einsum('cghp,gk->cghkp', m, eye).reshape(nc, gpc * hch, gpc * p)

    def out_blocks(m):
        m = m.reshape(nc, gpc, hch, p).transpose(0, 1, 3, 2)
        return jnp.einsum('cgph,gk->cgpkh', m, eye).reshape(nc, gpc * p, gpc * hch)

    bmat = jnp.concatenate([in_blocks(jnp.real(b_bar)), in_blocks(jnp.imag(b_bar))], axis=-1)
    cmat = jnp.concatenate([out_blocks(c_re.astype(F32)), out_blocks(-c_im.astype(F32))], axis=1)
    lam_ri = jnp.stack([jnp.real(lam_bar).reshape(nc, gpc * p), jnp.imag(lam_bar).reshape(nc, gpc * p)], axis=1)
    return bmat.astype(BF16), cmat.astype(BF16), lam_ri


def _glu_kernel(z_ref, wg_ref, bg_ref, wo_ref, x_ref, mod_ref, o_ref, slab_ref, zb_ref):
    bsz, tt, d = x_ref.shape
    pitch = slab_ref.shape[1] // tt
    for c in range(d // LANES):
        for t in range(tt):
            slab_ref[c, t * pitch:t * pitch + bsz, :] = z_ref[t * bsz:(t + 1) * bsz, c * LANES:(c + 1) * LANES]
        for b in range(bsz):
            zb_ref[b * tt:(b + 1) * tt, c * LANES:(c + 1) * LANES] = slab_ref[c, pl.ds(b, tt, stride=pitch), :]
    z = zb_ref[...]
    a = jnp.dot(z.astype(BF16), wg_ref[...], preferred_element_type=F32) + bg_ref[...]
    g = z * jax.nn.sigmoid(a)
    y = jnp.dot(g.astype(BF16), wo_ref[...], preferred_element_type=F32)
    for b in range(bsz):
        o_ref[b] = x_ref[b] + mod_ref[b, 2:3] * y[b * tt:(b + 1) * tt, :]


def _glu_residual(z_tm, w_glu, b_glu, w_o, x, mod):
    bsz, s, d = x.shape
    tt = min(TIME_TILE, s)
    pitch = _pitch(bsz)
    return pl.pallas_call(
        _glu_kernel,
        grid=(s // tt,),
        in_specs=[
            pl.BlockSpec((tt * bsz, d), lambda j: (j, 0)),
            _resident(w_glu.shape),
            _resident(b_glu.shape),
            _resident(w_o.shape),
            pl.BlockSpec((bsz, tt, d), lambda j: (0, j, 0)),
            _resident(mod.shape),
        ],
        out_specs=pl.BlockSpec((bsz, tt, d), lambda j: (0, j, 0)),
        out_shape=jax.ShapeDtypeStruct(x.shape, F32),
        scratch_shapes=[pltpu.VMEM((d // LANES, tt * pitch, LANES), F32), pltpu.VMEM((bsz * tt, d), F32)],
        compiler_params=_params("arbitrary"),
        name="glu_residual",
    )(z_tm, w_glu, b_glu, w_o, x, mod)


def _ffn_chunks(f):
    assert f % MXU_COLS == 0
    tiles = f // MXU_COLS
    n = min(FFN_CHUNKS, tiles)
    sizes = [(tiles // n + (1 if c < tiles % n else 0)) * MXU_COLS for c in range(n)]
    starts = [sum(sizes[:c]) for c in range(n)]
    return list(zip(starts, sizes))


def _ffn_kernel(x_ref, xh_ref, mod_ref, gain_ref, wup_ref, cw_ref, cb_ref, wdn_ref, *rest, final):
    if final:
        fmod_ref, fgain_ref = rest[:2]
        rest = rest[2:]
    o_ref, hs_ref, acc_ref = rest[0], rest[1], rest[2]
    u_refs = rest[3:]
    tm = x_ref.shape[1]
    f = wdn_ref.shape[0]
    m = mod_ref[0]
    gain = gain_ref[...]
    x = x_ref[0]
    h_halo = _modnorm(xh_ref[0], gain, m[3:4], m[4:5])
    h_halo = jnp.where(pl.program_id(1) > 0, h_halo, 0.0)
    hs_ref[0:HALO, :] = h_halo.astype(BF16)
    hs_ref[HALO:, :] = _modnorm(x, gain, m[3:4], m[4:5]).astype(BF16)

    def conv(u_ref, c0, tf):
        w = cw_ref[:, c0:c0 + tf]
        return (w[2:3] * u_ref[pl.ds(HALO, tm), :] + w[1:2] * u_ref[pl.ds(HALO - 1, tm), :]
                + w[0:1] * u_ref[pl.ds(HALO - 2, tm), :] + cb_ref[:, c0:c0 + tf])

    for c, (g0, tf) in enumerate(_ffn_chunks(f)):
        ug_ref, uv_ref = u_refs[2 * c], u_refs[2 * c + 1]
        v0 = f + g0
        hs = hs_ref[...]
        ug_ref[...] = jnp.dot(hs, wup_ref[:, g0:g0 + tf], preferred_element_type=F32)
        uv_ref[...] = jnp.dot(hs, wup_ref[:, v0:v0 + tf], preferred_element_type=F32)
        act = (jax.nn.silu(conv(ug_ref, g0, tf)) * conv(uv_ref, v0, tf)).astype(BF16)
        y = jnp.dot(act, wdn_ref[g0:g0 + tf, :], preferred_element_type=F32)
        if c == 0:
            acc_ref[...] = y
        else:
            acc_ref[...] += y
    out = x + m[5:6] * acc_ref[...]
    if final:
        fm = fmod_ref[0]
        out = _modnorm(out, fgain_ref[...], fm[0:1], fm[1:2])
    o_ref[0] = out


def _conv_ffn(x, mod, gain, w_up, conv_w, conv_b, w_down, final=None):
    bsz, s, d = x.shape
    f = w_down.shape[0]
    tm = min(ROW_TILE, s)
    halo_blocks = tm // HALO
    in_specs = [
        pl.BlockSpec((1, tm, d), lambda b, j: (b, j, 0)),
        pl.BlockSpec((1, HALO, d), lambda b, j: (b, jnp.maximum(j * halo_blocks - 1, 0), 0)),
        pl.BlockSpec((1,) + mod.shape[1:], lambda b, j: (b, 0, 0)),
        _resident(gain.shape),
        _resident(w_up.shape),
        _resident(conv_w.shape),
        _resident(conv_b.shape),
        _resident(w_down.shape),
    ]
    args = [x, x, mod, gain, w_up, conv_w, conv_b, w_down]
    if final is not None:
        in_specs += [pl.BlockSpec((1,) + final[0].shape[1:], lambda b, j: (b, 0, 0)), _resident(final[1].shape)]
        args += list(final)
    u_scratch = [pltpu.VMEM((tm + HALO, tf), F32) for _, tf in _ffn_chunks(f) for _ in range(2)]
    return pl.pallas_call(
        functools.partial(_ffn_kernel, final=final is not None),
        grid=(bsz, s // tm),
        in_specs=in_specs,
        out_specs=pl.BlockSpec((1, tm, d), lambda b, j: (b, j, 0)),
        out_shape=jax.ShapeDtypeStruct(x.shape, F32),
        scratch_shapes=[pltpu.VMEM((tm + HALO, d), BF16), pltpu.VMEM((tm, d), F32)] + u_scratch,
        compiler_params=_params("arbitrary", "arbitrary"),
        name="conv_ffn",
    )(*args)


def kernel(x, c, norm_mix, norm_ffn, w_mod, b_mod, w_qkv, w_o_attn, w_in_ssm, a_re, a_im, log_dt, b_re, b_im, c_re, c_im, d_skip, w_glu, b_glu, w_o_ssm, w_up, conv_w, conv_b, w_down, norm_out, w_fin, b_fin):
    bsz, s, d = x.shape
    depth = w_mod.shape[0]
    assert s % (4 * Q_BLOCK) == 0 and d % LANES == 0 and bsz % SUBLANES == 0

    mods = _mod_project(c, w_mod, b_mod).reshape(depth, bsz, 6, d)
    fin_mod = _mod_project(c, w_fin[None], b_fin[None]).reshape(bsz, 2, d)

    for i in range(depth):
        mod = mods[i]
        j = i // 2
        gain_mix = norm_mix[i][None]
        if i % 2 == 0:
            wq = (w_qkv[j][:, :d] * (-math.log2(math.e) * HEAD_DIM ** -0.5)).astype(BF16)
            wk = w_qkv[j][:, d:2 * d].astype(BF16)
            wvt = w_qkv[j][:, 2 * d:].T.astype(BF16)
            q, k_perm, vt_perm = _qkv_project(x, mod, gain_mix, wq, wk, wvt)
            o = _attention(q, k_perm, vt_perm)
            x = _proj_residual(o, w_o_attn[j].astype(BF16), x, mod, gate_row=2)
        else:
            u_tm = _norm_project_time_major(x, mod, gain_mix, w_in_ssm[j].astype(BF16))
            bmat, cmat, lam = _ssm_matrices(a_re[j], a_im[j], log_dt[j], b_re[j], b_im[j], c_re[j], c_im[j])
            z_tm = _ssm(u_tm, bmat, cmat, lam, d_skip[j][None], nb=bsz)
            x = _glu_residual(z_tm, w_glu[j].astype(BF16), b_glu[j][None], w_o_ssm[j].astype(BF16), x, mod)
        final = (fin_mod, norm_out[None]) if i == depth - 1 else None
        x = _conv_ffn(x, mod, norm_ffn[i][None], w_up[i].astype(BF16), conv_w[i], conv_b[i][None],
                      w_down[i].astype(BF16), final=final)
    return x
```

```python
import functools
import math

import jax
import jax.numpy as jnp
from jax import lax
from jax.experimental import pallas as pl
from jax.experimental.pallas import tpu as pltpu

F32 = jnp.float32
BF16 = jnp.bfloat16

HEAD_DIM = 64
SSM_GROUP = 16
STATE = 64
CONV_W = 3
EPS = 1e-6

LANES = 128
SUBLANES = 8
MXU_COLS = 256
Q_BLOCK = LANES
KEY_BLOCK = 64
KEY_ROWS = KEY_BLOCK // SUBLANES
PAST_BLOCKS = 3
WINDOW_BLOCKS = PAST_BLOCKS + Q_BLOCK // KEY_BLOCK
KEY_PAD = PAST_BLOCKS * KEY_BLOCK
WINDOW = WINDOW_BLOCKS * KEY_BLOCK
V_WINDOW = -(-WINDOW // LANES) * LANES
HALO = 16
SSM_CHUNK = LANES
VMEM_LIMIT = 56 * 1024 * 1024
MASKED_SCORE = 1e30
SATURATED_SCORE = 64.0

ROW_TILE = 512
FFN_CHUNKS = 2
TIME_TILE = 32
SSM_TIME_TILE = 128
MOD_COL_TILE = 2048


def _params(*sem):
    return pltpu.CompilerParams(dimension_semantics=sem, vmem_limit_bytes=VMEM_LIMIT)


def _resident(shape):
    nd = len(shape)
    return pl.BlockSpec(shape, lambda *_: (0,) * nd, pipeline_mode=pl.Buffered(1))


def _modnorm(x, gain, shift, scale):
    ms = jnp.mean(x * x, axis=-1, keepdims=True)
    y = x * lax.rsqrt(ms + EPS) * gain
    return y * (1.0 + scale) + shift


def _mod_kernel(c_ref, w_ref, b_ref, o_ref):
    ca = jax.nn.silu(c_ref[...])
    o_ref[0] = jnp.dot(ca, w_ref[0], preferred_element_type=F32) + b_ref[0]


def _mod_project(c, w, b):
    nl, d, n = w.shape
    bsz = c.shape[0]
    tn = min(MOD_COL_TILE, n)
    return pl.pallas_call(
        _mod_kernel,
        grid=(nl, n // tn),
        in_specs=[
            pl.BlockSpec((bsz, d), lambda l, j: (0, 0)),
            pl.BlockSpec((1, d, tn), lambda l, j: (l, 0, j)),
            pl.BlockSpec((1, 1, tn), lambda l, j: (l, 0, j)),
        ],
        out_specs=pl.BlockSpec((1, bsz, tn), lambda l, j: (l, 0, j)),
        out_shape=jax.ShapeDtypeStruct((nl, bsz, n), F32),
        compiler_params=_params("arbitrary", "arbitrary"),
        name="adaln_project",
    )(c, w, b.reshape(nl, 1, n))


def _qkv_kernel(x_ref, mod_ref, gain_ref, wq_ref, wk_ref, wvt_ref, q_ref, k_ref, vt_ref, h_ref, hp_ref):
    tm = x_ref.shape[1]
    m = mod_ref[0]
    h = _modnorm(x_ref[0], gain_ref[...], m[0:1], m[1:2])
    q_ref[0] = jnp.dot(h.astype(BF16), wq_ref[...], preferred_element_type=F32).astype(q_ref.dtype)
    for c in range(h_ref.shape[0]):
        h_ref[c] = h[:, c * LANES:(c + 1) * LANES]
        for blk in range(tm // KEY_BLOCK):
            for r2 in range(KEY_ROWS // 2):
                base = blk * KEY_BLOCK
                lo = h_ref[c, pl.ds(base + 2 * r2, SUBLANES, stride=KEY_ROWS), :]
                hi = h_ref[c, pl.ds(base + 2 * r2 + 1, SUBLANES, stride=KEY_ROWS), :]
                hp_ref[base + 2 * SUBLANES * r2:base + 2 * SUBLANES * (r2 + 1), c * LANES:(c + 1) * LANES] = (
                    jnp.concatenate([lo, hi], axis=0).astype(BF16))
    hp = hp_ref[...]
    k_ref[0] = jnp.dot(hp, wk_ref[...], preferred_element_type=F32).astype(k_ref.dtype)
    vt_ref[0] = lax.dot_general(wvt_ref[...], hp, (((1,), (1,)), ((), ())),
                                preferred_element_type=F32).astype(vt_ref.dtype)


def _qkv_project(x, mod, gain, wq, wk, wvt):
    bsz, s, d = x.shape
    tm = min(ROW_TILE, s)
    row_spec = pl.BlockSpec((1, tm, d), lambda b, j: (b, j, 0))
    return pl.pallas_call(
        _qkv_kernel,
        grid=(bsz, s // tm),
        in_specs=[
            row_spec,
            pl.BlockSpec((1,) + mod.shape[1:], lambda b, j: (b, 0, 0)),
            _resident(gain.shape),
            _resident(wq.shape),
            _resident(wk.shape),
            _resident(wvt.shape),
        ],
        out_specs=[row_spec, row_spec, pl.BlockSpec((1, d, tm), lambda b, j: (b, 0, j))],
        out_shape=[jax.ShapeDtypeStruct((bsz, s, d), BF16), jax.ShapeDtypeStruct((bsz, s, d), BF16),
                   jax.ShapeDtypeStruct((bsz, d, s), BF16)],
        scratch_shapes=[pltpu.VMEM((d // LANES, tm, LANES), F32), pltpu.VMEM((tm, d), BF16)],
        compiler_params=_params("arbitrary", "arbitrary"),
        name="qkv_project",
    )(x, mod, gain, wq, wk, wvt)


def _pitch(rows):
    p = -(-rows // SUBLANES)
    return SUBLANES * (p if p % 2 else p + 1)


def _norm_matmul_kernel(x_ref, mod_ref, gain_ref, w_ref, o_ref, hs_ref, slab_ref):
    bsz, tt, d = x_ref.shape
    pitch = slab_ref.shape[1] // bsz
    for b in range(bsz):
        h = _modnorm(x_ref[b], gain_ref[...], mod_ref[b, 0:1], mod_ref[b, 1:2])
        hs_ref[b * tt:(b + 1) * tt, :] = h.astype(BF16)
    u = jnp.dot(hs_ref[...], w_ref[...], preferred_element_type=F32)
    for c in range(d // LANES):
        for b in range(bsz):
            slab_ref[c, b * pitch:b * pitch + tt, :] = u[b * tt:(b + 1) * tt, c * LANES:(c + 1) * LANES]
        for t in range(tt):
            o_ref[t * bsz:(t + 1) * bsz, c * LANES:(c + 1) * LANES] = slab_ref[c, pl.ds(t, bsz, stride=pitch), :]


def _norm_project_time_major(x, mod, gain, w):
    bsz, s, d = x.shape
    n = w.shape[1]
    tt = min(TIME_TILE, s)
    pitch = _pitch(tt)
    return pl.pallas_call(
        _norm_matmul_kernel,
        grid=(s // tt,),
        in_specs=[
            pl.BlockSpec((bsz, tt, d), lambda j: (0, j, 0)),
            _resident(mod.shape),
            _resident(gain.shape),
            _resident(w.shape),
        ],
        out_specs=pl.BlockSpec((tt * bsz, n), lambda j: (j, 0)),
        out_shape=jax.ShapeDtypeStruct((s * bsz, n), F32),
        scratch_shapes=[pltpu.VMEM((bsz * tt, d), BF16), pltpu.VMEM((n // LANES, bsz * pitch, LANES), F32)],
        compiler_params=_params("arbitrary"),
        name="norm_project",
    )(x, mod, gain, w)


def _streams(z_view, col, base, kq, s_view):
    run = jnp.ones((SUBLANES, Q_BLOCK), F32)
    for r in range(KEY_ROWS - 1, -1, -1):
        lo = base + SUBLANES * r
        zn = z_view[lo:lo + SUBLANES, col:col + Q_BLOCK]
        if kq is not None:
            zn = jnp.where(kq > r, zn, MASKED_SCORE)
        e = jnp.exp2(zn)
        a = run * (1.0 / (1.0 + e))
        s_view[lo:lo + SUBLANES, :] = a
        run = jnp.where(zn > SATURATED_SCORE, run, e * a)
    return run


def _suffix(tot, row):
    p = tot
    for d in (1, 2, 4):
        p = jnp.where(row + d < SUBLANES, p * pltpu.roll(p, SUBLANES - d, 0), p)
    excl = jnp.where(row < SUBLANES - 1, pltpu.roll(p, SUBLANES - 1, 0), 1.0)
    return excl, jnp.broadcast_to(p[0:1, :], p.shape)


def _apply_offsets(s_view, a_view, base, off):
    off2 = jnp.concatenate([off, off], axis=0)
    for r2 in range(KEY_BLOCK // 16):
        lo = base + 16 * r2
        a_view[lo:lo + 16, :] = (s_view[lo:lo + 16, :] * off2).astype(BF16)


def _nt_dot(a, b):
    return lax.dot_general(a, b, (((1,), (1,)), ((), ())), preferred_element_type=F32)


def _attn_kernel(q_ref, k_ref, vt_ref, o_ref, kp_ref, vp_ref, z_ref, s_ref, a_ref, t_ref, oacc_ref, c_ref,
                 flag_ref):
    s_len = q_ref.shape[1]
    n_heads = LANES // HEAD_DIM
    row = lax.broadcasted_iota(jnp.int32, (SUBLANES, Q_BLOCK), 0)
    lane = lax.broadcasted_iota(jnp.int32, (SUBLANES, Q_BLOCK), 1)
    kqs = [lane - KEY_ROWS * row - (b - PAST_BLOCKS) * KEY_BLOCK if b >= PAST_BLOCKS else None
           for b in range(WINDOW_BLOCKS)]
    q_lane = lax.broadcasted_iota(jnp.int32, (Q_BLOCK, LANES), 1)

    n_blocks = s_len // Q_BLOCK

    kp_ref[0:KEY_PAD, :] = jnp.zeros((KEY_PAD, LANES), kp_ref.dtype)
    kp_ref[KEY_PAD:, :] = k_ref[0]
    vp_ref[:, 0:KEY_PAD] = jnp.zeros((LANES, KEY_PAD), vp_ref.dtype)
    vp_ref[:, KEY_PAD:KEY_PAD + s_len] = vt_ref[0]
    vp_ref[:, KEY_PAD + s_len:] = jnp.zeros((LANES, V_WINDOW - WINDOW), vp_ref.dtype)

    def stacked_queries(w):
        q = q_ref[0, pl.ds(pl.multiple_of(w * Q_BLOCK, Q_BLOCK), Q_BLOCK), :]
        zero = jnp.zeros_like(q)
        return jnp.concatenate([jnp.where((q_lane >= h * HEAD_DIM) & (q_lane < (h + 1) * HEAD_DIM), q, zero)
                                for h in range(n_heads)], axis=0)

    def store_out(w):
        o_ref[0, pl.ds(pl.multiple_of(w * Q_BLOCK, Q_BLOCK), Q_BLOCK), :] = (
            oacc_ref[w].T.astype(o_ref.dtype))

    def scores(p, slot, i):
        w = jnp.minimum(2 * p + i, n_blocks - 1)
        k_win = kp_ref[pl.ds(pl.multiple_of(w * Q_BLOCK, Q_BLOCK), WINDOW), :]
        z_ref[slot, i] = _nt_dot(k_win, stacked_queries(w))

    def weights(p, slot, i):
        w = 2 * p + i
        for h in range(n_heads):
            z_view, s_view, a_view = z_ref.at[slot, i], s_ref.at[slot, i, h], a_ref.at[slot, i, h]
            tots = [_streams(z_view, h * Q_BLOCK, b * KEY_BLOCK, kqs[b], s_view)
                    for b in range(WINDOW_BLOCKS)]
            carry = None
            for b in range(WINDOW_BLOCKS - 1, -1, -1):
                excl, total = _suffix(tots[b], row)
                _apply_offsets(s_view, a_view, b * KEY_BLOCK, excl if carry is None else carry * excl)
                carry = total if carry is None else carry * total
            c_ref[w, h] = carry
        alive = jnp.max(jnp.maximum(c_ref[w, 0], c_ref[w, 1])) > 0.0
        flag_ref[w] = alive.astype(jnp.int32)

    def values(p, slot, i):
        w = jnp.maximum(2 * p + i, 0)
        k0 = pl.multiple_of(w * Q_BLOCK, Q_BLOCK)
        for h in range(n_heads):
            v_win = vp_ref[h * HEAD_DIM:(h + 1) * HEAD_DIM, pl.ds(k0, V_WINDOW)]
            oacc_ref[w, h * HEAD_DIM:(h + 1) * HEAD_DIM, :] = jnp.dot(v_win, a_ref[slot, i, h],
                                                                      preferred_element_type=F32)
        store_out(w)

    for i in range(2):
        scores(0, 0, i)
    a_ref[...] = jnp.zeros(a_ref.shape, a_ref.dtype)

    def two_pairs(m, _):
        p = 2 * m
        for p_w, slot in ((p, 0), (p + 1, 1)):
            for i in range(2):
                scores(p_w + 1, 1 - slot, i)
                weights(p_w, slot, i)
                values(p_w - 1, 1 - slot, i)
        return 0

    lax.fori_loop(0, n_blocks // 4, two_pairs, 0)
    for i in range(2):
        values(n_blocks // 2 - 1, 1, i)

    def older_keys(w):
        q2 = stacked_queries(w)
        n_sub = Q_BLOCK // KEY_BLOCK

        def cond(st):
            return jnp.logical_and(st[0] >= 1, st[1] > 0)

        def body(st):
            j = st[0]
            k0 = pl.multiple_of(j * Q_BLOCK, Q_BLOCK)
            z_ref[0, 0, 0:Q_BLOCK, :] = _nt_dot(kp_ref[pl.ds(k0, Q_BLOCK), :], q2)
            carries = []
            for h in range(n_heads):
                s_view, a_view = s_ref.at[0, 0, h], t_ref.at[h]
                tots = [_streams(z_ref.at[0, 0], h * Q_BLOCK, b * KEY_BLOCK, None, s_view) for b in range(n_sub)]
                carry = c_ref[w, h]
                for b in range(n_sub - 1, -1, -1):
                    excl, total = _suffix(tots[b], row)
                    _apply_offsets(s_view, a_view, b * KEY_BLOCK, carry * excl)
                    carry = carry * total
                v_blk = vp_ref[h * HEAD_DIM:(h + 1) * HEAD_DIM, pl.ds(k0, Q_BLOCK)]
                oacc_ref[w, h * HEAD_DIM:(h + 1) * HEAD_DIM, :] += jnp.dot(
                    v_blk, t_ref[h], preferred_element_type=F32)
                carries.append(carry)
                c_ref[w, h] = carry
            alive = jnp.max(functools.reduce(jnp.maximum, carries)) > 0.0
            return (j - 1, alive.astype(jnp.int32))

        lax.while_loop(cond, body, (w - 1, jnp.int32(1)))

    def finish(w, _):
        @pl.when(flag_ref[w] > 0)
        def _():
            older_keys(w)
            store_out(w)

        return 0

    lax.fori_loop(2, n_blocks, finish, 0)


def _attention(q, k_perm, vt_perm):
    bsz, s, d = q.shape
    n_heads = LANES // HEAD_DIM
    n_blocks = s // Q_BLOCK
    return pl.pallas_call(
        _attn_kernel,
        grid=(bsz, d // LANES),
        in_specs=[
            pl.BlockSpec((1, s, LANES), lambda b, h: (b, 0, h)),
            pl.BlockSpec((1, s, LANES), lambda b, h: (b, 0, h)),
            pl.BlockSpec((1, LANES, s), lambda b, h: (b, h, 0)),
        ],
        out_specs=pl.BlockSpec((1, s, LANES), lambda b, h: (b, 0, h)),
        out_shape=jax.ShapeDtypeStruct((bsz, s, d), BF16),
        scratch_shapes=[
            pltpu.VMEM((KEY_PAD + s, LANES), BF16),
            pltpu.VMEM((LANES, KEY_PAD + s + V_WINDOW - WINDOW), BF16),
            pltpu.VMEM((2, 2, WINDOW, n_heads * Q_BLOCK), F32),
            pltpu.VMEM((2, 2, n_heads, WINDOW, Q_BLOCK), F32),
            pltpu.VMEM((2, 2, n_heads, V_WINDOW, Q_BLOCK), BF16),
            pltpu.VMEM((n_heads, Q_BLOCK, Q_BLOCK), BF16),
            pltpu.VMEM((n_blocks, LANES, Q_BLOCK), F32),
            pltpu.VMEM((n_blocks, n_heads, SUBLANES, Q_BLOCK), F32),
            pltpu.SMEM((n_blocks,), jnp.int32),
        ],
        compiler_params=_params("arbitrary", "arbitrary"),
        name="stickbreak_attention",
    )(q, k_perm, vt_perm)


def _ssm_kernel(u_ref, bmat_ref, cmat_ref, lam_ref, d_ref, z_ref, xs_ref, st_ref, *, nb, row_chunk):
    rows = u_ref.shape[0]
    half = st_ref.shape[1] // 2

    @pl.when(pl.program_id(1) == 0)
    def _():
        st_ref[...] = jnp.zeros_like(st_ref)

    a_re = jnp.broadcast_to(lam_ref[0, 0:1, :], (nb, half))
    a_im = jnp.broadcast_to(lam_ref[0, 1:2, :], (nb, half))
    x_re, x_im = st_ref[:, :half], st_ref[:, half:]

    for r0 in range(0, rows, row_chunk):
        xs_ref[r0:r0 + row_chunk, :] = jnp.dot(u_ref[r0:r0 + row_chunk, :].astype(BF16), bmat_ref[0],
                                               preferred_element_type=F32)
        for r in range(r0, r0 + row_chunk, nb):
            n_re = a_re * x_re - a_im * x_im + xs_ref[r:r + nb, :half]
            n_im = a_re * x_im + a_im * x_re + xs_ref[r:r + nb, half:]
            xs_ref[r:r + nb, :half] = n_re
            xs_ref[r:r + nb, half:] = n_im
            x_re, x_im = n_re, n_im
        y = jnp.dot(xs_ref[r0:r0 + row_chunk, :].astype(BF16), cmat_ref[0], preferred_element_type=F32)
        z_ref[r0:r0 + row_chunk, :] = jax.nn.gelu(y + d_ref[...] * u_ref[r0:r0 + row_chunk, :])

    st_ref[:, :half] = x_re
    st_ref[:, half:] = x_im


def _ssm(u_tm, bmat, cmat, lam, d_skip, nb):
    rows_total, d = u_tm.shape
    s = rows_total // nb
    tt = min(SSM_TIME_TILE, s)
    rows = tt * nb
    n_state = bmat.shape[-1]
    return pl.pallas_call(
        functools.partial(_ssm_kernel, nb=nb, row_chunk=min(256, rows)),
        grid=(d // SSM_CHUNK, s // tt),
        in_specs=[
            pl.BlockSpec((rows, SSM_CHUNK), lambda c, i: (i, c)),
            pl.BlockSpec((1, SSM_CHUNK, n_state), lambda c, i: (c, 0, 0)),
            pl.BlockSpec((1, n_state, SSM_CHUNK), lambda c, i: (c, 0, 0)),
            pl.BlockSpec((1, 2, n_state // 2), lambda c, i: (c, 0, 0)),
            pl.BlockSpec((1, SSM_CHUNK), lambda c, i: (0, c)),
        ],
        out_specs=pl.BlockSpec((rows, SSM_CHUNK), lambda c, i: (i, c)),
        out_shape=jax.ShapeDtypeStruct(u_tm.shape, F32),
        scratch_shapes=[pltpu.VMEM((rows, n_state), F32), pltpu.VMEM((nb, n_state), F32)],
        compiler_params=_params("arbitrary", "arbitrary"),
        name="s5_scan",
    )(u_tm, bmat, cmat, lam, d_skip)


def _ssm_matrices(a_re, a_im, log_dt, b_re, b_im, c_re, c_im):
    g, p = a_re.shape
    hch = b_re.shape[-1]
    gpc = SSM_CHUNK // hch
    nc = g // gpc
    lam = lax.complex(a_re.astype(F32), a_im.astype(F32))
    dt = jnp.exp(log_dt.astype(F32))[:, None]
    lam_bar = jnp.exp(lam * dt)
    b_bar = ((lam_bar - 1) / lam)[..., None] * lax.complex(b_re.astype(F32), b_im.astype(F32))
    eye = jnp.eye(gpc, dtype=F32)

    def in_blocks(m):
        m = m.reshape(nc, gpc, p, hch).transpose(0, 1, 3, 2)
        return jnp.einsum('cghp,gk->cghkp', m, eye).reshape(nc, gpc * hch, gpc * p)

    def out_blocks(m):
        m = m.reshape(nc, gpc, hch, p).transpose(0, 1, 3, 2)
        return jnp.einsum('cgph,gk->cgpkh', m, eye).reshape(nc, gpc * p, gpc * hch)

    bmat = jnp.concatenate([in_blocks(jnp.real(b_bar)), in_blocks(jnp.imag(b_bar))], axis=-1)
    cmat = jnp.concatenate([out_blocks(c_re.astype(F32)), out_blocks(-c_im.astype(F32))], axis=1)
    lam_ri = jnp.stack([jnp.real(lam_bar).reshape(nc, gpc * p), jnp.imag(lam_bar).reshape(nc, gpc * p)], axis=1)
    return bmat.astype(BF16), cmat.astype(BF16), lam_ri


def _glu_kernel(z_ref, wg_ref, bg_ref, wo_ref, x_ref, mod_ref, o_ref, slab_ref, zb_ref):
    bsz, tt, d = x_ref.shape
    pitch = slab_ref.shape[1] // tt
    for c in range(d // LANES):
        for t in range(tt):
            slab_ref[c, t * pitch:t * pitch + bsz, :] = z_ref[t * bsz:(t + 1) * bsz, c * LANES:(c + 1) * LANES]
        for b in range(bsz):
            zb_ref[b * tt:(b + 1) * tt, c * LANES:(c + 1) * LANES] = slab_ref[c, pl.ds(b, tt, stride=pitch), :]
    z = zb_ref[...]
    a = jnp.dot(z.astype(BF16), wg_ref[...], preferred_element_type=F32) + bg_ref[...]
    g = z * jax.nn.sigmoid(a)
    y = jnp.dot(g.astype(BF16), wo_ref[...], preferred_element_type=F32)
    for b in range(bsz):
        o_ref[b] = x_ref[b] + mod_ref[b, 2:3] * y[b * tt:(b + 1) * tt, :]


def _glu_residual(z_tm, w_glu, b_glu, w_o, x, mod):
    bsz, s, d = x.shape
    tt = min(TIME_TILE, s)
    pitch = _pitch(bsz)
    return pl.pallas_call(
        _glu_kernel,
        grid=(s // tt,),
        in_specs=[
            pl.BlockSpec((tt * bsz, d), lambda j: (j, 0)),
            _resident(w_glu.shape),
            _resident(b_glu.shape),
            _resident(w_o.shape),
            pl.BlockSpec((bsz, tt, d), lambda j: (0, j, 0)),
            _resident(mod.shape),
        ],
        out_specs=pl.BlockSpec((bsz, tt, d), lambda j: (0, j, 0)),
        out_shape=jax.ShapeDtypeStruct(x.shape, F32),
        scratch_shapes=[pltpu.VMEM((d // LANES, tt * pitch, LANES), F32), pltpu.VMEM((bsz * tt, d), F32)],
        compiler_params=_params("arbitrary"),
        name="glu_residual",
    )(z_tm, w_glu, b_glu, w_o, x, mod)


def _ffn_chunks(f):
    assert f % MXU_COLS == 0
    tiles = f // MXU_COLS
    n = min(FFN_CHUNKS, tiles)
    sizes = [(tiles // n + (1 if c < tiles % n else 0)) * MXU_COLS for c in range(n)]
    starts = [sum(sizes[:c]) for c in range(n)]
    return list(zip(starts, sizes))


def _ffn_kernel(x_ref, xh_ref, mod_ref, gain_ref, wup_ref, cw_ref, cb_ref, wdn_ref, *rest, mixer, final):
    if mixer:
        a_ref, ah_ref, wo_ref = rest[:3]
        rest = rest[3:]
    if final:
        fmod_ref, fgain_ref = rest[:2]
        rest = rest[2:]
    o_ref, hs_ref, acc_ref, act_ref = rest[:4]
    u_refs = rest[4:]
    tm = x_ref.shape[1]
    f = wdn_ref.shape[0]
    m = mod_ref[0]
    gain = gain_ref[...]
    x, x_halo = x_ref[0], xh_ref[0]
    if mixer:
        a_cat = jnp.concatenate([ah_ref[0], a_ref[0]], axis=0)
        y_mix = m[2:3] * jnp.dot(a_cat, wo_ref[...], preferred_element_type=F32)
        x, x_halo = x + y_mix[HALO:], x_halo + y_mix[:HALO]
    h_halo = _modnorm(x_halo, gain, m[3:4], m[4:5])
    h_halo = jnp.where(pl.program_id(1) > 0, h_halo, 0.0)
    hs_ref[0:HALO, :] = h_halo.astype(BF16)
    hs_ref[HALO:, :] = _modnorm(x, gain, m[3:4], m[4:5]).astype(BF16)

    def conv(u_ref, l0, c0):
        w = cw_ref[:, c0:c0 + LANES]
        return (w[2:3] * u_ref[pl.ds(HALO, tm), l0:l0 + LANES] + w[1:2] * u_ref[pl.ds(HALO - 1, tm), l0:l0 + LANES]
                + w[0:1] * u_ref[pl.ds(HALO - 2, tm), l0:l0 + LANES] + cb_ref[:, c0:c0 + LANES])

    for c, (g0, tf) in enumerate(_ffn_chunks(f)):
        ug_ref, uv_ref = u_refs[2 * c], u_refs[2 * c + 1]
        v0 = f + g0
        hs = hs_ref[...]
        ug_ref[...] = jnp.dot(hs, wup_ref[:, g0:g0 + tf], preferred_element_type=F32)
        uv_ref[...] = jnp.dot(hs, wup_ref[:, v0:v0 + tf], preferred_element_type=F32)
        for l0 in range(0, tf, LANES):
            act_ref[:, l0:l0 + LANES] = (jax.nn.silu(conv(ug_ref, l0, g0 + l0))
                                         * conv(uv_ref, l0, v0 + l0)).astype(BF16)
        y = jnp.dot(act_ref[:, 0:tf], wdn_ref[g0:g0 + tf, :], preferred_element_type=F32)
        if c == 0:
            acc_ref[...] = y
        else:
            acc_ref[...] += y
    out = x + m[5:6] * acc_ref[...]
    if final:
        fm = fmod_ref[0]
        out = _modnorm(out, fgain_ref[...], fm[0:1], fm[1:2])
    o_ref[0] = out


def _conv_ffn(x, mod, gain, w_up, conv_w, conv_b, w_down, mixer=None, final=None):
    bsz, s, d = x.shape
    f = w_down.shape[0]
    tm = min(ROW_TILE, s)
    halo_blocks = tm // HALO
    tile_spec = pl.BlockSpec((1, tm, d), lambda b, j: (b, j, 0))
    halo_spec = pl.BlockSpec((1, HALO, d), lambda b, j: (b, jnp.maximum(j * halo_blocks - 1, 0), 0))
    in_specs = [
        tile_spec,
        halo_spec,
        pl.BlockSpec((1,) + mod.shape[1:], lambda b, j: (b, 0, 0)),
        _resident(gain.shape),
        _resident(w_up.shape),
        _resident(conv_w.shape),
        _resident(conv_b.shape),
        _resident(w_down.shape),
    ]
    args = [x, x, mod, gain, w_up, conv_w, conv_b, w_down]
    if mixer is not None:
        in_specs += [tile_spec, halo_spec, _resident(mixer[1].shape)]
        args += [mixer[0], mixer[0], mixer[1]]
    if final is not None:
        in_specs += [pl.BlockSpec((1,) + final[0].shape[1:], lambda b, j: (b, 0, 0)), _resident(final[1].shape)]
        args += list(final)
    u_scratch = [pltpu.VMEM((tm + HALO, tf), F32) for _, tf in _ffn_chunks(f) for _ in range(2)]
    return pl.pallas_call(
        functools.partial(_ffn_kernel, mixer=mixer is not None, final=final is not None),
        grid=(bsz, s // tm),
        in_specs=in_specs,
        out_specs=pl.BlockSpec((1, tm, d), lambda b, j: (b, j, 0)),
        out_shape=jax.ShapeDtypeStruct(x.shape, F32),
        scratch_shapes=[pltpu.VMEM((tm + HALO, d), BF16), pltpu.VMEM((tm, d), F32),
                        pltpu.VMEM((tm, max(tf for _, tf in _ffn_chunks(f))), BF16)] + u_scratch,
        compiler_params=_params("arbitrary", "arbitrary"),
        name="conv_ffn",
    )(*args)


def kernel(x, c, norm_mix, norm_ffn, w_mod, b_mod, w_qkv, w_o_attn, w_in_ssm, a_re, a_im, log_dt, b_re, b_im, c_re, c_im, d_skip, w_glu, b_glu, w_o_ssm, w_up, conv_w, conv_b, w_down, norm_out, w_fin, b_fin):
    bsz, s, d = x.shape
    depth = w_mod.shape[0]
    assert s % (4 * Q_BLOCK) == 0 and d % LANES == 0 and bsz % SUBLANES == 0

    mods = _mod_project(c, w_mod, b_mod).reshape(depth, bsz, 6, d)
    fin_mod = _mod_project(c, w_fin[None], b_fin[None]).reshape(bsz, 2, d)

    for i in range(depth):
        mod = mods[i]
        j = i // 2
        gain_mix = norm_mix[i][None]
        if i % 2 == 0:
            wq = (w_qkv[j][:, :d] * (-math.log2(math.e) * HEAD_DIM ** -0.5)).astype(BF16)
            wk = w_qkv[j][:, d:2 * d].astype(BF16)
            wvt = w_qkv[j][:, 2 * d:].T.astype(BF16)
            q, k_perm, vt_perm = _qkv_project(x, mod, gain_mix, wq, wk, wvt)
            mixer = (_attention(q, k_perm, vt_perm), w_o_attn[j].astype(BF16))
        else:
            mixer = None
            u_tm = _norm_project_time_major(x, mod, gain_mix, w_in_ssm[j].astype(BF16))
            bmat, cmat, lam = _ssm_matrices(a_re[j], a_im[j], log_dt[j], b_re[j], b_im[j], c_re[j], c_im[j])
            z_tm = _ssm(u_tm, bmat, cmat, lam, d_skip[j][None], nb=bsz)
            x = _glu_residual(z_tm, w_glu[j].astype(BF16), b_glu[j][None], w_o_ssm[j].astype(BF16), x, mod)
        final = (fin_mod, norm_out[None]) if i == depth - 1 else None
        x = _conv_ffn(x, mod, norm_ffn[i][None], w_up[i].astype(BF16), conv_w[i], conv_b[i][None],
                      w_down[i].astype(BF16), mixer=mixer, final=final)
    return x
```

```python
import functools
import math

import jax
import jax.numpy as jnp
from jax import lax
from jax.experimental import pallas as pl
from jax.experimental.pallas import tpu as pltpu

F32 = jnp.float32
BF16 = jnp.bfloat16

HEAD_DIM = 64
SSM_GROUP = 16
STATE = 64
CONV_W = 3
EPS = 1e-6

LANES = 128
SUBLANES = 8
MXU_COLS = 256
Q_BLOCK = LANES
KEY_BLOCK = 64
KEY_ROWS = KEY_BLOCK // SUBLANES
PAST_BLOCKS = 3
WINDOW_BLOCKS = PAST_BLOCKS + Q_BLOCK // KEY_BLOCK
KEY_PAD = PAST_BLOCKS * KEY_BLOCK
WINDOW = WINDOW_BLOCKS * KEY_BLOCK
V_WINDOW = -(-WINDOW // LANES) * LANES
HALO = 16
SSM_CHUNK = LANES
VMEM_LIMIT = 56 * 1024 * 1024
MASKED_SCORE = 1e30
SATURATED_SCORE = 64.0

ROW_TILE = 512
FFN_CHUNKS = 2
TIME_TILE = 32
SSM_TIME_TILE = 128
MOD_COL_TILE = 2048


def _params(*sem):
    return pltpu.CompilerParams(dimension_semantics=sem, vmem_limit_bytes=VMEM_LIMIT)


def _resident(shape):
    nd = len(shape)
    return pl.BlockSpec(shape, lambda *_: (0,) * nd, pipeline_mode=pl.Buffered(1))


def _modnorm(x, gain, shift, scale):
    ms = jnp.mean(x * x, axis=-1, keepdims=True)
    y = x * lax.rsqrt(ms + EPS) * gain
    return y * (1.0 + scale) + shift


def _mod_kernel(c_ref, w_ref, b_ref, o_ref):
    ca = jax.nn.silu(c_ref[...])
    o_ref[0] = jnp.dot(ca, w_ref[0], preferred_element_type=F32) + b_ref[0]


def _mod_project(c, w, b):
    nl, d, n = w.shape
    bsz = c.shape[0]
    tn = min(MOD_COL_TILE, n)
    return pl.pallas_call(
        _mod_kernel,
        grid=(nl, n // tn),
        in_specs=[
            pl.BlockSpec((bsz, d), lambda l, j: (0, 0)),
            pl.BlockSpec((1, d, tn), lambda l, j: (l, 0, j)),
            pl.BlockSpec((1, 1, tn), lambda l, j: (l, 0, j)),
        ],
        out_specs=pl.BlockSpec((1, bsz, tn), lambda l, j: (l, 0, j)),
        out_shape=jax.ShapeDtypeStruct((nl, bsz, n), F32),
        compiler_params=_params("arbitrary", "arbitrary"),
        name="adaln_project",
    )(c, w, b.reshape(nl, 1, n))


def _qkv_kernel(x_ref, mod_ref, gain_ref, wq_ref, wk_ref, wvt_ref, q_ref, k_ref, vt_ref, h_ref, hp_ref):
    tm = x_ref.shape[1]
    m = mod_ref[0]
    h = _modnorm(x_ref[0], gain_ref[...], m[0:1], m[1:2])
    q_ref[0] = jnp.dot(h.astype(BF16), wq_ref[...], preferred_element_type=F32).astype(q_ref.dtype)
    for c in range(h_ref.shape[0]):
        h_ref[c] = h[:, c * LANES:(c + 1) * LANES]
        for blk in range(tm // KEY_BLOCK):
            for r2 in range(KEY_ROWS // 2):
                base = blk * KEY_BLOCK
                lo = h_ref[c, pl.ds(base + 2 * r2, SUBLANES, stride=KEY_ROWS), :]
                hi = h_ref[c, pl.ds(base + 2 * r2 + 1, SUBLANES, stride=KEY_ROWS), :]
                hp_ref[base + 2 * SUBLANES * r2:base + 2 * SUBLANES * (r2 + 1), c * LANES:(c + 1) * LANES] = (
                    jnp.concatenate([lo, hi], axis=0).astype(BF16))
    hp = hp_ref[...]
    k_ref[0] = jnp.dot(hp, wk_ref[...], preferred_element_type=F32).astype(k_ref.dtype)
    vt_ref[0] = lax.dot_general(wvt_ref[...], hp, (((1,), (1,)), ((), ())),
                                preferred_element_type=F32).astype(vt_ref.dtype)


def _qkv_project(x, mod, gain, wq, wk, wvt):
    bsz, s, d = x.shape
    tm = min(ROW_TILE, s)
    row_spec = pl.BlockSpec((1, tm, d), lambda b, j: (b, j, 0))
    return pl.pallas_call(
        _qkv_kernel,
        grid=(bsz, s // tm),
        in_specs=[
            row_spec,
            pl.BlockSpec((1,) + mod.shape[1:], lambda b, j: (b, 0, 0)),
            _resident(gain.shape),
            _resident(wq.shape),
            _resident(wk.shape),
            _resident(wvt.shape),
        ],
        out_specs=[row_spec, row_spec, pl.BlockSpec((1, d, tm), lambda b, j: (b, 0, j))],
        out_shape=[jax.ShapeDtypeStruct((bsz, s, d), BF16), jax.ShapeDtypeStruct((bsz, s, d), BF16),
                   jax.ShapeDtypeStruct((bsz, d, s), BF16)],
        scratch_shapes=[pltpu.VMEM((d // LANES, tm, LANES), F32), pltpu.VMEM((tm, d), BF16)],
        compiler_params=_params("arbitrary", "arbitrary"),
        name="qkv_project",
    )(x, mod, gain, wq, wk, wvt)


def _pitch(rows):
    p = -(-rows // SUBLANES)
    return SUBLANES * (p if p % 2 else p + 1)


def _norm_matmul_kernel(x_ref, mod_ref, gain_ref, w_ref, o_ref, hs_ref, slab_ref):
    bsz, tt, d = x_ref.shape
    pitch = slab_ref.shape[1] // bsz
    for b in range(bsz):
        h = _modnorm(x_ref[b], gain_ref[...], mod_ref[b, 0:1], mod_ref[b, 1:2])
        hs_ref[b * tt:(b + 1) * tt, :] = h.astype(BF16)
    u = jnp.dot(hs_ref[...], w_ref[...], preferred_element_type=F32)
    for c in range(d // LANES):
        for b in range(bsz):
            slab_ref[c, b * pitch:b * pitch + tt, :] = u[b * tt:(b + 1) * tt, c * LANES:(c + 1) * LANES]
        for t in range(tt):
            o_ref[t * bsz:(t + 1) * bsz, c * LANES:(c + 1) * LANES] = slab_ref[c, pl.ds(t, bsz, stride=pitch), :]


def _norm_project_time_major(x, mod, gain, w):
    bsz, s, d = x.shape
    n = w.shape[1]
    tt = min(TIME_TILE, s)
    pitch = _pitch(tt)
    return pl.pallas_call(
        _norm_matmul_kernel,
        grid=(s // tt,),
        in_specs=[
            pl.BlockSpec((bsz, tt, d), lambda j: (0, j, 0)),
            _resident(mod.shape),
            _resident(gain.shape),
            _resident(w.shape),
        ],
        out_specs=pl.BlockSpec((tt * bsz, n), lambda j: (j, 0)),
        out_shape=jax.ShapeDtypeStruct((s * bsz, n), F32),
        scratch_shapes=[pltpu.VMEM((bsz * tt, d), BF16), pltpu.VMEM((n // LANES, bsz * pitch, LANES), F32)],
        compiler_params=_params("arbitrary"),
        name="norm_project",
    )(x, mod, gain, w)


def _streams(z_view, col, base, kq, s_view):
    run = jnp.ones((SUBLANES, Q_BLOCK), F32)
    for r in range(KEY_ROWS - 1, -1, -1):
        lo = base + SUBLANES * r
        zn = z_view[lo:lo + SUBLANES, col:col + Q_BLOCK]
        if kq is not None:
            zn = jnp.where(kq > r, zn, MASKED_SCORE)
        e = jnp.exp2(zn)
        a = run * (1.0 / (1.0 + e))
        s_view[lo:lo + SUBLANES, :] = a
        run = jnp.where(zn > SATURATED_SCORE, run, e * a)
    return run


def _suffix(tot, row):
    p = tot
    for d in (1, 2, 4):
        p = jnp.where(row + d < SUBLANES, p * pltpu.roll(p, SUBLANES - d, 0), p)
    excl = jnp.where(row < SUBLANES - 1, pltpu.roll(p, SUBLANES - 1, 0), 1.0)
    return excl, jnp.broadcast_to(p[0:1, :], p.shape)


def _apply_offsets(s_view, a_view, base, off):
    off2 = jnp.concatenate([off, off], axis=0)
    for r2 in range(KEY_BLOCK // 16):
        lo = base + 16 * r2
        a_view[lo:lo + 16, :] = (s_view[lo:lo + 16, :] * off2).astype(BF16)


def _nt_dot(a, b):
    return lax.dot_general(a, b, (((1,), (1,)), ((), ())), preferred_element_type=F32)


def _attn_kernel(q_ref, k_ref, vt_ref, o_ref, kp_ref, vp_ref, z_ref, s_ref, a_ref, t_ref, oacc_ref, c_ref,
                 flag_ref):
    s_len = q_ref.shape[1]
    n_heads = LANES // HEAD_DIM
    row = lax.broadcasted_iota(jnp.int32, (SUBLANES, Q_BLOCK), 0)
    lane = lax.broadcasted_iota(jnp.int32, (SUBLANES, Q_BLOCK), 1)
    kqs = [lane - KEY_ROWS * row - (b - PAST_BLOCKS) * KEY_BLOCK if b >= PAST_BLOCKS else None
           for b in range(WINDOW_BLOCKS)]
    q_lane = lax.broadcasted_iota(jnp.int32, (Q_BLOCK, LANES), 1)

    n_blocks = s_len // Q_BLOCK

    kp_ref[0:KEY_PAD, :] = jnp.zeros((KEY_PAD, LANES), kp_ref.dtype)
    kp_ref[KEY_PAD:, :] = k_ref[0]
    vp_ref[:, 0:KEY_PAD] = jnp.zeros((LANES, KEY_PAD), vp_ref.dtype)
    vp_ref[:, KEY_PAD:KEY_PAD + s_len] = vt_ref[0]
    vp_ref[:, KEY_PAD + s_len:] = jnp.zeros((LANES, V_WINDOW - WINDOW), vp_ref.dtype)

    def stacked_queries(w):
        q = q_ref[0, pl.ds(pl.multiple_of(w * Q_BLOCK, Q_BLOCK), Q_BLOCK), :]
        zero = jnp.zeros_like(q)
        return jnp.concatenate([jnp.where((q_lane >= h * HEAD_DIM) & (q_lane < (h + 1) * HEAD_DIM), q, zero)
                                for h in range(n_heads)], axis=0)

    def store_out(w):
        o_ref[0, pl.ds(pl.multiple_of(w * Q_BLOCK, Q_BLOCK), Q_BLOCK), :] = (
            oacc_ref[w].T.astype(o_ref.dtype))

    def scores(p, slot, i):
        w = jnp.minimum(2 * p + i, n_blocks - 1)
        k_win = kp_ref[pl.ds(pl.multiple_of(w * Q_BLOCK, Q_BLOCK), WINDOW), :]
        z_ref[slot, i] = _nt_dot(k_win, stacked_queries(w))

    def weights(p, slot, i):
        w = 2 * p + i
        for h in range(n_heads):
            z_view, s_view, a_view = z_ref.at[slot, i], s_ref.at[slot, i, h], a_ref.at[slot, i, h]
            tots = [_streams(z_view, h * Q_BLOCK, b * KEY_BLOCK, kqs[b], s_view)
                    for b in range(WINDOW_BLOCKS)]
            carry = None
            for b in range(WINDOW_BLOCKS - 1, -1, -1):
                excl, total = _suffix(tots[b], row)
                _apply_offsets(s_view, a_view, b * KEY_BLOCK, excl if carry is None else carry * excl)
                carry = total if carry is None else carry * total
            c_ref[w, h] = carry
        alive = jnp.max(jnp.maximum(c_ref[w, 0], c_ref[w, 1])) > 0.0
        flag_ref[w] = alive.astype(jnp.int32)

    def values(p, slot, i):
        w = jnp.maximum(2 * p + i, 0)
        k0 = pl.multiple_of(w * Q_BLOCK, Q_BLOCK)
        for h in range(n_heads):
            v_win = vp_ref[h * HEAD_DIM:(h + 1) * HEAD_DIM, pl.ds(k0, V_WINDOW)]
            oacc_ref[w, h * HEAD_DIM:(h + 1) * HEAD_DIM, :] = jnp.dot(v_win, a_ref[slot, i, h],
                                                                      preferred_element_type=F32)
        store_out(w)

    for i in range(2):
        scores(0, 0, i)
    a_ref[...] = jnp.zeros(a_ref.shape, a_ref.dtype)

    def two_pairs(m, _):
        p = 2 * m
        for p_w, slot in ((p, 0), (p + 1, 1)):
            for i in range(2):
                scores(p_w + 1, 1 - slot, i)
                weights(p_w, slot, i)
                values(p_w - 1, 1 - slot, i)
        return 0

    lax.fori_loop(0, n_blocks // 4, two_pairs, 0)
    for i in range(2):
        values(n_blocks // 2 - 1, 1, i)

    def older_keys(w):
        q2 = stacked_queries(w)
        n_sub = Q_BLOCK // KEY_BLOCK

        def cond(st):
            return jnp.logical_and(st[0] >= 1, st[1] > 0)

        def body(st):
            j = st[0]
            k0 = pl.multiple_of(j * Q_BLOCK, Q_BLOCK)
            z_ref[0, 0, 0:Q_BLOCK, :] = _nt_dot(kp_ref[pl.ds(k0, Q_BLOCK), :], q2)
            carries = []
            for h in range(n_heads):
                s_view, a_view = s_ref.at[0, 0, h], t_ref.at[h]
                tots = [_streams(z_ref.at[0, 0], h * Q_BLOCK, b * KEY_BLOCK, None, s_view) for b in range(n_sub)]
                carry = c_ref[w, h]
                for b in range(n_sub - 1, -1, -1):
                    excl, total = _suffix(tots[b], row)
                    _apply_offsets(s_view, a_view, b * KEY_BLOCK, carry * excl)
                    carry = carry * total
                v_blk = vp_ref[h * HEAD_DIM:(h + 1) * HEAD_DIM, pl.ds(k0, Q_BLOCK)]
                oacc_ref[w, h * HEAD_DIM:(h + 1) * HEAD_DIM, :] += jnp.dot(
                    v_blk, t_ref[h], preferred_element_type=F32)
                carries.append(carry)
                c_ref[w, h] = carry
            alive = jnp.max(functools.reduce(jnp.maximum, carries)) > 0.0
            return (j - 1, alive.astype(jnp.int32))

        lax.while_loop(cond, body, (w - 1, jnp.int32(1)))

    def finish(w, _):
        @pl.when(flag_ref[w] > 0)
        def _():
            older_keys(w)
            store_out(w)

        return 0

    lax.fori_loop(2, n_blocks, finish, 0)


def _attention(q, k_perm, vt_perm):
    bsz, s, d = q.shape
    n_heads = LANES // HEAD_DIM
    n_blocks = s // Q_BLOCK
    return pl.pallas_call(
        _attn_kernel,
        grid=(bsz, d // LANES),
        in_specs=[
            pl.BlockSpec((1, s, LANES), lambda b, h: (b, 0, h)),
            pl.BlockSpec((1, s, LANES), lambda b, h: (b, 0, h)),
            pl.BlockSpec((1, LANES, s), lambda b, h: (b, h, 0)),
        ],
        out_specs=pl.BlockSpec((1, s, LANES), lambda b, h: (b, 0, h)),
        out_shape=jax.ShapeDtypeStruct((bsz, s, d), BF16),
        scratch_shapes=[
            pltpu.VMEM((KEY_PAD + s, LANES), BF16),
            pltpu.VMEM((LANES, KEY_PAD + s + V_WINDOW - WINDOW), BF16),
            pltpu.VMEM((2, 2, WINDOW, n_heads * Q_BLOCK), F32),
            pltpu.VMEM((2, 2, n_heads, WINDOW, Q_BLOCK), F32),
            pltpu.VMEM((2, 2, n_heads, V_WINDOW, Q_BLOCK), BF16),
            pltpu.VMEM((n_heads, Q_BLOCK, Q_BLOCK), BF16),
            pltpu.VMEM((n_blocks, LANES, Q_BLOCK), F32),
            pltpu.VMEM((n_blocks, n_heads, SUBLANES, Q_BLOCK), F32),
            pltpu.SMEM((n_blocks,), jnp.int32),
        ],
        compiler_params=_params("arbitrary", "arbitrary"),
        name="stickbreak_attention",
    )(q, k_perm, vt_perm)


def _ssm_kernel(u_ref, bmat_ref, cmat_ref, lam_ref, d_ref, z_ref, xs_ref, st_ref, *, nb, row_chunk):
    rows = u_ref.shape[0]
    half = st_ref.shape[1] // 2

    @pl.when(pl.program_id(1) == 0)
    def _():
        st_ref[...] = jnp.zeros_like(st_ref)

    a_re = jnp.broadcast_to(lam_ref[0, 0:1, :], (nb, half))
    a_im = jnp.broadcast_to(lam_ref[0, 1:2, :], (nb, half))
    x_re, x_im = st_ref[:, :half], st_ref[:, half:]

    for r0 in range(0, rows, row_chunk):
        xs_ref[r0:r0 + row_chunk, :] = jnp.dot(u_ref[r0:r0 + row_chunk, :].astype(BF16), bmat_ref[0],
                                               preferred_element_type=F32)
        for r in range(r0, r0 + row_chunk, nb):
            n_re = a_re * x_re - a_im * x_im + xs_ref[r:r + nb, :half]
            n_im = a_re * x_im + a_im * x_re + xs_ref[r:r + nb, half:]
            xs_ref[r:r + nb, :half] = n_re
            xs_ref[r:r + nb, half:] = n_im
            x_re, x_im = n_re, n_im
        y = jnp.dot(xs_ref[r0:r0 + row_chunk, :].astype(BF16), cmat_ref[0], preferred_element_type=F32)
        z_ref[r0:r0 + row_chunk, :] = jax.nn.gelu(y + d_ref[...] * u_ref[r0:r0 + row_chunk, :])

    st_ref[:, :half] = x_re
    st_ref[:, half:] = x_im


def _ssm(u_tm, bmat, cmat, lam, d_skip, nb):
    rows_total, d = u_tm.shape
    s = rows_total // nb
    tt = min(SSM_TIME_TILE, s)
    rows = tt * nb
    n_state = bmat.shape[-1]
    return pl.pallas_call(
        functools.partial(_ssm_kernel, nb=nb, row_chunk=min(256, rows)),
        grid=(d // SSM_CHUNK, s // tt),
        in_specs=[
            pl.BlockSpec((rows, SSM_CHUNK), lambda c, i: (i, c)),
            pl.BlockSpec((1, SSM_CHUNK, n_state), lambda c, i: (c, 0, 0)),
            pl.BlockSpec((1, n_state, SSM_CHUNK), lambda c, i: (c, 0, 0)),
            pl.BlockSpec((1, 2, n_state // 2), lambda c, i: (c, 0, 0)),
            pl.BlockSpec((1, SSM_CHUNK), lambda c, i: (0, c)),
        ],
        out_specs=pl.BlockSpec((rows, SSM_CHUNK), lambda c, i: (i, c)),
        out_shape=jax.ShapeDtypeStruct(u_tm.shape, F32),
        scratch_shapes=[pltpu.VMEM((rows, n_state), F32), pltpu.VMEM((nb, n_state), F32)],
        compiler_params=_params("arbitrary", "arbitrary"),
        name="s5_scan",
    )(u_tm, bmat, cmat, lam, d_skip)


def _ssm_matrices(a_re, a_im, log_dt, b_re, b_im, c_re, c_im):
    g, p = a_re.shape
    hch = b_re.shape[-1]
    gpc = SSM_CHUNK // hch
    nc = g // gpc
    lam = lax.complex(a_re.astype(F32), a_im.astype(F32))
    dt = jnp.exp(log_dt.astype(F32))[:, None]
    lam_bar = jnp.exp(lam * dt)
    b_bar = ((lam_bar - 1) / lam)[..., None] * lax.complex(b_re.astype(F32), b_im.astype(F32))
    eye = jnp.eye(gpc, dtype=F32)

    def in_blocks(m):
        m = m.reshape(nc, gpc, p, hch).transpose(0, 1, 3, 2)
        return jnp.einsum('cghp,gk->cghkp', m, eye).reshape(nc, gpc * hch, gpc * p)

    def out_blocks(m):
        m = m.reshape(nc, gpc, hch, p).transpose(0, 1, 3, 2)
        return jnp.einsum('cgph,gk->cgpkh', m, eye).reshape(nc, gpc * p, gpc * hch)

    bmat = jnp.concatenate([in_blocks(jnp.real(b_bar)), in_blocks(jnp.imag(b_bar))], axis=-1)
    cmat = jnp.concatenate([out_blocks(c_re.astype(F32)), out_blocks(-c_im.astype(F32))], axis=1)
    lam_ri = jnp.stack([jnp.real(lam_bar).reshape(nc, gpc * p), jnp.imag(lam_bar).reshape(nc, gpc * p)], axis=1)
    return bmat.astype(BF16), cmat.astype(BF16), lam_ri


def _glu_kernel(z_ref, wg_ref, bg_ref, wo_ref, x_ref, mod_ref, o_ref, slab_ref, zb_ref):
    bsz, tt, d = x_ref.shape
    pitch = slab_ref.shape[1] // tt
    for c in range(d // LANES):
        for t in range(tt):
            slab_ref[c, t * pitch:t * pitch + bsz, :] = z_ref[t * bsz:(t + 1) * bsz, c * LANES:(c + 1) * LANES]
        for b in range(bsz):
            zb_ref[b * tt:(b + 1) * tt, c * LANES:(c + 1) * LANES] = slab_ref[c, pl.ds(b, tt, stride=pitch), :]
    z = zb_ref[...]
    a = jnp.dot(z.astype(BF16), wg_ref[...], preferred_element_type=F32) + bg_ref[...]
    g = z * jax.nn.sigmoid(a)
    y = jnp.dot(g.astype(BF16), wo_ref[...], preferred_element_type=F32)
    for b in range(bsz):
        o_ref[b] = x_ref[b] + mod_ref[b, 2:3] * y[b * tt:(b + 1) * tt, :]


def _glu_residual(z_tm, w_glu, b_glu, w_o, x, mod):
    bsz, s, d = x.shape
    tt = min(TIME_TILE, s)
    pitch = _pitch(bsz)
    return pl.pallas_call(
        _glu_kernel,
        grid=(s // tt,),
        in_specs=[
            pl.BlockSpec((tt * bsz, d), lambda j: (j, 0)),
            _resident(w_glu.shape),
            _resident(b_glu.shape),
            _resident(w_o.shape),
            pl.BlockSpec((bsz, tt, d), lambda j: (0, j, 0)),
            _resident(mod.shape),
        ],
        out_specs=pl.BlockSpec((bsz, tt, d), lambda j: (0, j, 0)),
        out_shape=jax.ShapeDtypeStruct(x.shape, F32),
        scratch_shapes=[pltpu.VMEM((d // LANES, tt * pitch, LANES), F32), pltpu.VMEM((bsz * tt, d), F32)],
        compiler_params=_params("arbitrary"),
        name="glu_residual",
    )(z_tm, w_glu, b_glu, w_o, x, mod)


def _ffn_chunks(f):
    assert f % MXU_COLS == 0
    tiles = f // MXU_COLS
    n = min(FFN_CHUNKS, tiles)
    sizes = [(tiles // n + (1 if c < tiles % n else 0)) * MXU_COLS for c in range(n)]
    starts = [sum(sizes[:c]) for c in range(n)]
    return list(zip(starts, sizes))


def _ffn_kernel(x_ref, xh_ref, mod_ref, gain_ref, wup_ref, cw_ref, cb_ref, wdn_ref, *rest, mixer, final):
    if mixer:
        a_ref, ah_ref, wo_ref = rest[:3]
        rest = rest[3:]
    if final:
        fmod_ref, fgain_ref = rest[:2]
        rest = rest[2:]
    o_ref, hs_ref, acc_ref = rest[:3]
    u_refs = rest[3:]
    tm = x_ref.shape[1]
    f = wdn_ref.shape[0]
    m = mod_ref[0]
    gain = gain_ref[...]
    x, x_halo = x_ref[0], xh_ref[0]
    if mixer:
        a_cat = jnp.concatenate([ah_ref[0], a_ref[0]], axis=0)
        y_mix = m[2:3] * jnp.dot(a_cat, wo_ref[...], preferred_element_type=F32)
        x, x_halo = x + y_mix[HALO:], x_halo + y_mix[:HALO]
    h_halo = _modnorm(x_halo, gain, m[3:4], m[4:5])
    h_halo = jnp.where(pl.program_id(1) > 0, h_halo, 0.0)
    hs_ref[0:HALO, :] = h_halo.astype(BF16)
    hs_ref[HALO:, :] = _modnorm(x, gain, m[3:4], m[4:5]).astype(BF16)

    def conv(u_ref, c0, tf):
        w = cw_ref[:, c0:c0 + tf]
        return (w[2:3] * u_ref[pl.ds(HALO, tm), :] + w[1:2] * u_ref[pl.ds(HALO - 1, tm), :]
                + w[0:1] * u_ref[pl.ds(HALO - 2, tm), :] + cb_ref[:, c0:c0 + tf])

    for c, (g0, tf) in enumerate(_ffn_chunks(f)):
        ug_ref, uv_ref = u_refs[2 * c], u_refs[2 * c + 1]
        v0 = f + g0
        hs = hs_ref[...]
        ug_ref[...] = jnp.dot(hs, wup_ref[:, g0:g0 + tf], preferred_element_type=F32)
        uv_ref[...] = jnp.dot(hs, wup_ref[:, v0:v0 + tf], preferred_element_type=F32)
        act = (jax.nn.silu(conv(ug_ref, g0, tf)) * conv(uv_ref, v0, tf)).astype(BF16)
        y = jnp.dot(act, wdn_ref[g0:g0 + tf, :], preferred_element_type=F32)
        if c == 0:
            acc_ref[...] = y
        else:
            acc_ref[...] += y
    out = x + m[5:6] * acc_ref[...]
    if final:
        fm = fmod_ref[0]
        out = _modnorm(out, fgain_ref[...], fm[0:1], fm[1:2])
    o_ref[0] = out


def _conv_ffn(x, mod, gain, w_up, conv_w, conv_b, w_down, mixer=None, final=None):
    bsz, s, d = x.shape
    f = w_down.shape[0]
    tm = min(ROW_TILE, s)
    halo_blocks = tm // HALO
    tile_spec = pl.BlockSpec((1, tm, d), lambda b, j: (b, j, 0))
    halo_spec = pl.BlockSpec((1, HALO, d), lambda b, j: (b, jnp.maximum(j * halo_blocks - 1, 0), 0))
    in_specs = [
        tile_spec,
        halo_spec,
        pl.BlockSpec((1,) + mod.shape[1:], lambda b, j: (b, 0, 0)),
        _resident(gain.shape),
        _resident(w_up.shape),
        _resident(conv_w.shape),
        _resident(conv_b.shape),
        _resident(w_down.shape),
    ]
    args = [x, x, mod, gain, w_up, conv_w, conv_b, w_down]
    if mixer is not None:
        in_specs += [tile_spec, halo_spec, _resident(mixer[1].shape)]
        args += [mixer[0], mixer[0], mixer[1]]
    if final is not None:
        in_specs += [pl.BlockSpec((1,) + final[0].shape[1:], lambda b, j: (b, 0, 0)), _resident(final[1].shape)]
        args += list(final)
    u_scratch = [pltpu.VMEM((tm + HALO, tf), F32) for _, tf in _ffn_chunks(f) for _ in range(2)]
    return pl.pallas_call(
        functools.partial(_ffn_kernel, mixer=mixer is not None, final=final is not None),
        grid=(bsz, s // tm),
        in_specs=in_specs,
        out_specs=pl.BlockSpec((1, tm, d), lambda b, j: (b, j, 0)),
        out_shape=jax.ShapeDtypeStruct(x.shape, F32),
        scratch_shapes=[pltpu.VMEM((tm + HALO, d), BF16), pltpu.VMEM((tm, d), F32)] + u_scratch,
        compiler_params=_params("arbitrary", "arbitrary"),
        name="conv_ffn",
    )(*args)


def kernel(x, c, norm_mix, norm_ffn, w_mod, b_mod, w_qkv, w_o_attn, w_in_ssm, a_re, a_im, log_dt, b_re, b_im, c_re, c_im, d_skip, w_glu, b_glu, w_o_ssm, w_up, conv_w, conv_b, w_down, norm_out, w_fin, b_fin):
    bsz, s, d = x.shape
    depth = w_mod.shape[0]
    assert s % (4 * Q_BLOCK) == 0 and d % LANES == 0 and bsz % SUBLANES == 0

    mods = _mod_project(c, w_mod, b_mod).reshape(depth, bsz, 6, d)
    fin_mod = _mod_project(c, w_fin[None], b_fin[None]).reshape(bsz, 2, d)

    for i in range(depth):
        mod = mods[i]
        j = i // 2
        gain_mix = norm_mix[i][None]
        if i % 2 == 0:
            wq = (w_qkv[j][:, :d] * (-math.log2(math.e) * HEAD_DIM ** -0.5)).astype(BF16)
            wk = w_qkv[j][:, d:2 * d].astype(BF16)
            wvt = w_qkv[j][:, 2 * d:].T.astype(BF16)
            q, k_perm, vt_perm = _qkv_project(x, mod, gain_mix, wq, wk, wvt)
            mixer = (_attention(q, k_perm, vt_perm), w_o_attn[j].astype(BF16))
        else:
            mixer = None
            u_tm = _norm_project_time_major(x, mod, gain_mix, w_in_ssm[j].astype(BF16))
            bmat, cmat, lam = _ssm_matrices(a_re[j], a_im[j], log_dt[j], b_re[j], b_im[j], c_re[j], c_im[j])
            z_tm = _ssm(u_tm, bmat, cmat, lam, d_skip[j][None], nb=bsz)
            x = _glu_residual(z_tm, w_glu[j].astype(BF16), b_glu[j][None], w_o_ssm[j].astype(BF16), x, mod)
        final = (fin_mod, norm_out[None]) if i == depth - 1 else None
        x = _conv_ffn(x, mod, norm_ffn[i][None], w_up[i].astype(BF16), conv_w[i], conv_b[i][None],
                      w_down[i].astype(BF16), mixer=mixer, final=final)
    return x
```

```python
import functools
import math

import jax
import jax.numpy as jnp
from jax import lax
from jax.experimental import pallas as pl
from jax.experimental.pallas import tpu as pltpu

F32 = jnp.float32
BF16 = jnp.bfloat16

HEAD_DIM = 64
SSM_GROUP = 16
STATE = 64
CONV_W = 3
EPS = 1e-6

LANES = 128
SUBLANES = 8
MXU_COLS = 256
Q_BLOCK = LANES
KEY_BLOCK = 64
KEY_ROWS = KEY_BLOCK // SUBLANES
PAST_BLOCKS = 3
WINDOW_BLOCKS = PAST_BLOCKS + Q_BLOCK // KEY_BLOCK
KEY_PAD = PAST_BLOCKS * KEY_BLOCK
WINDOW = WINDOW_BLOCKS * KEY_BLOCK
V_WINDOW = -(-WINDOW // LANES) * LANES
HALO = 16
SSM_CHUNK = LANES
VMEM_LIMIT = 56 * 1024 * 1024
MASKED_SCORE = 1e30
SATURATED_SCORE = 64.0

ROW_TILE = 512
FFN_CHUNKS = 2
TIME_TILE = 32
SSM_TIME_TILE = 128
MOD_COL_TILE = 2048


def _params(*sem):
    return pltpu.CompilerParams(dimension_semantics=sem, vmem_limit_bytes=VMEM_LIMIT)


def _resident(shape):
    nd = len(shape)
    return pl.BlockSpec(shape, lambda *_: (0,) * nd, pipeline_mode=pl.Buffered(1))


def _modnorm(x, gain, shift, scale):
    ms = jnp.mean(x * x, axis=-1, keepdims=True)
    y = x * lax.rsqrt(ms + EPS) * gain
    return y * (1.0 + scale) + shift


def _mod_kernel(c_ref, w_ref, b_ref, o_ref):
    ca = jax.nn.silu(c_ref[...])
    o_ref[0] = jnp.dot(ca, w_ref[0], preferred_element_type=F32) + b_ref[0]


def _mod_project(c, w, b):
    nl, d, n = w.shape
    bsz = c.shape[0]
    tn = min(MOD_COL_TILE, n)
    return pl.pallas_call(
        _mod_kernel,
        grid=(nl, n // tn),
        in_specs=[
            pl.BlockSpec((bsz, d), lambda l, j: (0, 0)),
            pl.BlockSpec((1, d, tn), lambda l, j: (l, 0, j)),
            pl.BlockSpec((1, 1, tn), lambda l, j: (l, 0, j)),
        ],
        out_specs=pl.BlockSpec((1, bsz, tn), lambda l, j: (l, 0, j)),
        out_shape=jax.ShapeDtypeStruct((nl, bsz, n), F32),
        compiler_params=_params("arbitrary", "arbitrary"),
        name="adaln_project",
    )(c, w, b.reshape(nl, 1, n))


def _store_lane_groups(o_ref, y):
    for g in range(o_ref.shape[1]):
        o_ref[0, g] = y[:, g * LANES:(g + 1) * LANES].astype(o_ref.dtype)


def _load_lane_groups(a_ref):
    return jnp.concatenate([a_ref[0, g] for g in range(a_ref.shape[1])], axis=1)


def _qkv_kernel(x_ref, mod_ref, gain_ref, wq_ref, wk_ref, wvt_ref, q_ref, k_ref, vt_ref, h_ref, hp_ref):
    tm = x_ref.shape[1]
    m = mod_ref[0]
    h = _modnorm(x_ref[0], gain_ref[...], m[0:1], m[1:2])
    _store_lane_groups(q_ref, jnp.dot(h.astype(BF16), wq_ref[...], preferred_element_type=F32))
    for c in range(h_ref.shape[0]):
        h_ref[c] = h[:, c * LANES:(c + 1) * LANES]
        for blk in range(tm // KEY_BLOCK):
            for r2 in range(KEY_ROWS // 2):
                base = blk * KEY_BLOCK
                lo = h_ref[c, pl.ds(base + 2 * r2, SUBLANES, stride=KEY_ROWS), :]
                hi = h_ref[c, pl.ds(base + 2 * r2 + 1, SUBLANES, stride=KEY_ROWS), :]
                hp_ref[base + 2 * SUBLANES * r2:base + 2 * SUBLANES * (r2 + 1), c * LANES:(c + 1) * LANES] = (
                    jnp.concatenate([lo, hi], axis=0).astype(BF16))
    hp = hp_ref[...]
    _store_lane_groups(k_ref, jnp.dot(hp, wk_ref[...], preferred_element_type=F32))
    vt_ref[0] = lax.dot_general(wvt_ref[...], hp, (((1,), (1,)), ((), ())),
                                preferred_element_type=F32).astype(vt_ref.dtype)


def _qkv_project(x, mod, gain, wq, wk, wvt):
    bsz, s, d = x.shape
    tm = min(ROW_TILE, s)
    row_spec = pl.BlockSpec((1, tm, d), lambda b, j: (b, j, 0))
    group_spec = pl.BlockSpec((1, d // LANES, tm, LANES), lambda b, j: (b, 0, j, 0))
    return pl.pallas_call(
        _qkv_kernel,
        grid=(bsz, s // tm),
        in_specs=[
            row_spec,
            pl.BlockSpec((1,) + mod.shape[1:], lambda b, j: (b, 0, 0)),
            _resident(gain.shape),
            _resident(wq.shape),
            _resident(wk.shape),
            _resident(wvt.shape),
        ],
        out_specs=[group_spec, group_spec, pl.BlockSpec((1, d, tm), lambda b, j: (b, 0, j))],
        out_shape=[jax.ShapeDtypeStruct((bsz, d // LANES, s, LANES), BF16),
                   jax.ShapeDtypeStruct((bsz, d // LANES, s, LANES), BF16),
                   jax.ShapeDtypeStruct((bsz, d, s), BF16)],
        scratch_shapes=[pltpu.VMEM((d // LANES, tm, LANES), F32), pltpu.VMEM((tm, d), BF16)],
        compiler_params=_params("arbitrary", "arbitrary"),
        name="qkv_project",
    )(x, mod, gain, wq, wk, wvt)


def _pitch(rows):
    p = -(-rows // SUBLANES)
    return SUBLANES * (p if p % 2 else p + 1)


def _norm_matmul_kernel(x_ref, mod_ref, gain_ref, w_ref, o_ref, hs_ref, slab_ref):
    bsz, tt, d = x_ref.shape
    pitch = slab_ref.shape[1] // bsz
    for b in range(bsz):
        h = _modnorm(x_ref[b], gain_ref[...], mod_ref[b, 0:1], mod_ref[b, 1:2])
        hs_ref[b * tt:(b + 1) * tt, :] = h.astype(BF16)
    u = jnp.dot(hs_ref[...], w_ref[...], preferred_element_type=F32)
    for c in range(d // LANES):
        for b in range(bsz):
            slab_ref[c, b * pitch:b * pitch + tt, :] = u[b * tt:(b + 1) * tt, c * LANES:(c + 1) * LANES]
        for t in range(tt):
            o_ref[t * bsz:(t + 1) * bsz, c * LANES:(c + 1) * LANES] = slab_ref[c, pl.ds(t, bsz, stride=pitch), :]


def _norm_project_time_major(x, mod, gain, w):
    bsz, s, d = x.shape
    n = w.shape[1]
    tt = min(TIME_TILE, s)
    pitch = _pitch(tt)
    return pl.pallas_call(
        _norm_matmul_kernel,
        grid=(s // tt,),
        in_specs=[
            pl.BlockSpec((bsz, tt, d), lambda j: (0, j, 0)),
            _resident(mod.shape),
            _resident(gain.shape),
            _resident(w.shape),
        ],
        out_specs=pl.BlockSpec((tt * bsz, n), lambda j: (j, 0)),
        out_shape=jax.ShapeDtypeStruct((s * bsz, n), F32),
        scratch_shapes=[pltpu.VMEM((bsz * tt, d), BF16), pltpu.VMEM((n // LANES, bsz * pitch, LANES), F32)],
        compiler_params=_params("arbitrary"),
        name="norm_project",
    )(x, mod, gain, w)


def _streams(z_view, col, base, kq, s_view):
    run = jnp.ones((SUBLANES, Q_BLOCK), F32)
    for r in range(KEY_ROWS - 1, -1, -1):
        lo = base + SUBLANES * r
        zn = z_view[lo:lo + SUBLANES, col:col + Q_BLOCK]
        if kq is not None:
            zn = jnp.where(kq > r, zn, MASKED_SCORE)
        e = jnp.exp2(zn)
        a = run * (1.0 / (1.0 + e))
        s_view[lo:lo + SUBLANES, :] = a
        run = jnp.where(zn > SATURATED_SCORE, run, e * a)
    return run


def _suffix(tot, row):
    p = tot
    for d in (1, 2, 4):
        p = jnp.where(row + d < SUBLANES, p * pltpu.roll(p, SUBLANES - d, 0), p)
    excl = jnp.where(row < SUBLANES - 1, pltpu.roll(p, SUBLANES - 1, 0), 1.0)
    return excl, jnp.broadcast_to(p[0:1, :], p.shape)


def _apply_offsets(s_view, a_view, base, off):
    off2 = jnp.concatenate([off, off], axis=0)
    for r2 in range(KEY_BLOCK // 16):
        lo = base + 16 * r2
        a_view[lo:lo + 16, :] = (s_view[lo:lo + 16, :] * off2).astype(BF16)


def _nt_dot(a, b):
    return lax.dot_general(a, b, (((1,), (1,)), ((), ())), preferred_element_type=F32)


def _attn_kernel(q_ref, k_ref, vt_ref, o_ref, kp_ref, vp_ref, z_ref, s_ref, a_ref, t_ref, oacc_ref, c_ref,
                 flag_ref):
    s_len = q_ref.shape[0]
    n_heads = LANES // HEAD_DIM
    row = lax.broadcasted_iota(jnp.int32, (SUBLANES, Q_BLOCK), 0)
    lane = lax.broadcasted_iota(jnp.int32, (SUBLANES, Q_BLOCK), 1)
    kqs = [lane - KEY_ROWS * row - (b - PAST_BLOCKS) * KEY_BLOCK if b >= PAST_BLOCKS else None
           for b in range(WINDOW_BLOCKS)]
    q_lane = lax.broadcasted_iota(jnp.int32, (Q_BLOCK, LANES), 1)

    n_blocks = s_len // Q_BLOCK

    kp_ref[0:KEY_PAD, :] = jnp.zeros((KEY_PAD, LANES), kp_ref.dtype)
    kp_ref[KEY_PAD:, :] = k_ref[...]
    vp_ref[:, 0:KEY_PAD] = jnp.zeros((LANES, KEY_PAD), vp_ref.dtype)
    vp_ref[:, KEY_PAD:KEY_PAD + s_len] = vt_ref[...]
    vp_ref[:, KEY_PAD + s_len:] = jnp.zeros((LANES, V_WINDOW - WINDOW), vp_ref.dtype)

    def stacked_queries(w):
        q = q_ref[pl.ds(pl.multiple_of(w * Q_BLOCK, Q_BLOCK), Q_BLOCK), :]
        zero = jnp.zeros_like(q)
        return jnp.concatenate([jnp.where((q_lane >= h * HEAD_DIM) & (q_lane < (h + 1) * HEAD_DIM), q, zero)
                                for h in range(n_heads)], axis=0)

    def store_out(w):
        o_ref[pl.ds(pl.multiple_of(w * Q_BLOCK, Q_BLOCK), Q_BLOCK), :] = (
            oacc_ref[w].T.astype(o_ref.dtype))

    def scores(p, slot, i):
        w = jnp.minimum(2 * p + i, n_blocks - 1)
        k_win = kp_ref[pl.ds(pl.multiple_of(w * Q_BLOCK, Q_BLOCK), WINDOW), :]
        z_ref[slot, i] = _nt_dot(k_win, stacked_queries(w))

    def weights(p, slot, i):
        w = 2 * p + i
        for h in range(n_heads):
            z_view, s_view, a_view = z_ref.at[slot, i], s_ref.at[slot, i, h], a_ref.at[slot, i, h]
            tots = [_streams(z_view, h * Q_BLOCK, b * KEY_BLOCK, kqs[b], s_view)
                    for b in range(WINDOW_BLOCKS)]
            carry = None
            for b in range(WINDOW_BLOCKS - 1, -1, -1):
                excl, total = _suffix(tots[b], row)
                _apply_offsets(s_view, a_view, b * KEY_BLOCK, excl if carry is None else carry * excl)
                carry = total if carry is None else carry * total
            c_ref[w, h] = carry
        alive = jnp.max(jnp.maximum(c_ref[w, 0], c_ref[w, 1])) > 0.0
        flag_ref[w] = alive.astype(jnp.int32)

    def values(p, slot, i):
        w = jnp.maximum(2 * p + i, 0)
        k0 = pl.multiple_of(w * Q_BLOCK, Q_BLOCK)
        for h in range(n_heads):
            v_win = vp_ref[h * HEAD_DIM:(h + 1) * HEAD_DIM, pl.ds(k0, V_WINDOW)]
            oacc_ref[w, h * HEAD_DIM:(h + 1) * HEAD_DIM, :] = jnp.dot(v_win, a_ref[slot, i, h],
                                                                      preferred_element_type=F32)
        store_out(w)

    for i in range(2):
        scores(0, 0, i)
    a_ref[...] = jnp.zeros(a_ref.shape, a_ref.dtype)

    def two_pairs(m, _):
        p = 2 * m
        for p_w, slot in ((p, 0), (p + 1, 1)):
            for i in range(2):
                scores(p_w + 1, 1 - slot, i)
                weights(p_w, slot, i)
                values(p_w - 1, 1 - slot, i)
        return 0

    lax.fori_loop(0, n_blocks // 4, two_pairs, 0)
    for i in range(2):
        values(n_blocks // 2 - 1, 1, i)

    def older_keys(w):
        q2 = stacked_queries(w)
        n_sub = Q_BLOCK // KEY_BLOCK

        def cond(st):
            return jnp.logical_and(st[0] >= 1, st[1] > 0)

        def body(st):
            j = st[0]
            k0 = pl.multiple_of(j * Q_BLOCK, Q_BLOCK)
            z_ref[0, 0, 0:Q_BLOCK, :] = _nt_dot(kp_ref[pl.ds(k0, Q_BLOCK), :], q2)
            carries = []
            for h in range(n_heads):
                s_view, a_view = s_ref.at[0, 0, h], t_ref.at[h]
                tots = [_streams(z_ref.at[0, 0], h * Q_BLOCK, b * KEY_BLOCK, None, s_view) for b in range(n_sub)]
                carry = c_ref[w, h]
                for b in range(n_sub - 1, -1, -1):
                    excl, total = _suffix(tots[b], row)
                    _apply_offsets(s_view, a_view, b * KEY_BLOCK, carry * excl)
                    carry = carry * total
                v_blk = vp_ref[h * HEAD_DIM:(h + 1) * HEAD_DIM, pl.ds(k0, Q_BLOCK)]
                oacc_ref[w, h * HEAD_DIM:(h + 1) * HEAD_DIM, :] += jnp.dot(
                    v_blk, t_ref[h], preferred_element_type=F32)
                carries.append(carry)
                c_ref[w, h] = carry
            alive = jnp.max(functools.reduce(jnp.maximum, carries)) > 0.0
            return (j - 1, alive.astype(jnp.int32))

        lax.while_loop(cond, body, (w - 1, jnp.int32(1)))

    def finish(w, _):
        @pl.when(flag_ref[w] > 0)
        def _():
            older_keys(w)
            store_out(w)

        return 0

    lax.fori_loop(2, n_blocks, finish, 0)


def _attention(q, k_perm, vt_perm):
    bsz, n_groups, s, _ = q.shape
    n_heads = LANES // HEAD_DIM
    n_blocks = s // Q_BLOCK
    group_spec = pl.BlockSpec((None, None, s, LANES), lambda b, h: (b, h, 0, 0))
    return pl.pallas_call(
        _attn_kernel,
        grid=(bsz, n_groups),
        in_specs=[group_spec, group_spec, pl.BlockSpec((None, LANES, s), lambda b, h: (b, h, 0))],
        out_specs=group_spec,
        out_shape=jax.ShapeDtypeStruct(q.shape, BF16),
        scratch_shapes=[
            pltpu.VMEM((KEY_PAD + s, LANES), BF16),
            pltpu.VMEM((LANES, KEY_PAD + s + V_WINDOW - WINDOW), BF16),
            pltpu.VMEM((2, 2, WINDOW, n_heads * Q_BLOCK), F32),
            pltpu.VMEM((2, 2, n_heads, WINDOW, Q_BLOCK), F32),
            pltpu.VMEM((2, 2, n_heads, V_WINDOW, Q_BLOCK), BF16),
            pltpu.VMEM((n_heads, Q_BLOCK, Q_BLOCK), BF16),
            pltpu.VMEM((n_blocks, LANES, Q_BLOCK), F32),
            pltpu.VMEM((n_blocks, n_heads, SUBLANES, Q_BLOCK), F32),
            pltpu.SMEM((n_blocks,), jnp.int32),
        ],
        compiler_params=_params("arbitrary", "arbitrary"),
        name="stickbreak_attention",
    )(q, k_perm, vt_perm)


def _ssm_kernel(u_ref, bmat_ref, cmat_ref, lam_ref, d_ref, z_ref, xs_ref, st_ref, *, nb, row_chunk):
    rows = u_ref.shape[0]
    half = st_ref.shape[1] // 2

    @pl.when(pl.program_id(1) == 0)
    def _():
        st_ref[...] = jnp.zeros_like(st_ref)

    a_re = jnp.broadcast_to(lam_ref[0, 0:1, :], (nb, half))
    a_im = jnp.broadcast_to(lam_ref[0, 1:2, :], (nb, half))
    x_re, x_im = st_ref[:, :half], st_ref[:, half:]

    for r0 in range(0, rows, row_chunk):
        xs_ref[r0:r0 + row_chunk, :] = jnp.dot(u_ref[r0:r0 + row_chunk, :].astype(BF16), bmat_ref[0],
                                               preferred_element_type=F32)
        for r in range(r0, r0 + row_chunk, nb):
            n_re = a_re * x_re - a_im * x_im + xs_ref[r:r + nb, :half]
            n_im = a_re * x_im + a_im * x_re + xs_ref[r:r + nb, half:]
            xs_ref[r:r + nb, :half] = n_re
            xs_ref[r:r + nb, half:] = n_im
            x_re, x_im = n_re, n_im
        y = jnp.dot(xs_ref[r0:r0 + row_chunk, :].astype(BF16), cmat_ref[0], preferred_element_type=F32)
        z_ref[r0:r0 + row_chunk, :] = jax.nn.gelu(y + d_ref[...] * u_ref[r0:r0 + row_chunk, :])

    st_ref[:, :half] = x_re
    st_ref[:, half:] = x_im


def _ssm(u_tm, bmat, cmat, lam, d_skip, nb):
    rows_total, d = u_tm.shape
    s = rows_total // nb
    tt = min(SSM_TIME_TILE, s)
    rows = tt * nb
    n_state = bmat.shape[-1]
    return pl.pallas_call(
        functools.partial(_ssm_kernel, nb=nb, row_chunk=min(256, rows)),
        grid=(d // SSM_CHUNK, s // tt),
        in_specs=[
            pl.BlockSpec((rows, SSM_CHUNK), lambda c, i: (i, c)),
            pl.BlockSpec((1, SSM_CHUNK, n_state), lambda c, i: (c, 0, 0)),
            pl.BlockSpec((1, n_state, SSM_CHUNK), lambda c, i: (c, 0, 0)),
            pl.BlockSpec((1, 2, n_state // 2), lambda c, i: (c, 0, 0)),
            pl.BlockSpec((1, SSM_CHUNK), lambda c, i: (0, c)),
        ],
        out_specs=pl.BlockSpec((rows, SSM_CHUNK), lambda c, i: (i, c)),
        out_shape=jax.ShapeDtypeStruct(u_tm.shape, F32),
        scratch_shapes=[pltpu.VMEM((rows, n_state), F32), pltpu.VMEM((nb, n_state), F32)],
        compiler_params=_params("arbitrary", "arbitrary"),
        name="s5_scan",
    )(u_tm, bmat, cmat, lam, d_skip)


def _ssm_matrices(a_re, a_im, log_dt, b_re, b_im, c_re, c_im):
    g, p = a_re.shape
    hch = b_re.shape[-1]
    gpc = SSM_CHUNK // hch
    nc = g // gpc
    lam = lax.complex(a_re.astype(F32), a_im.astype(F32))
    dt = jnp.exp(log_dt.astype(F32))[:, None]
    lam_bar = jnp.exp(lam * dt)
    b_bar = ((lam_bar - 1) / lam)[..., None] * lax.complex(b_re.astype(F32), b_im.astype(F32))
    eye = jnp.eye(gpc, dtype=F32)

    def in_blocks(m):
        m = m.reshape(nc, gpc, p, hch).transpose(0, 1, 3, 2)
        return jnp.einsum('cghp,gk->cghkp', m, eye).reshape(nc, gpc * hch, gpc * p)

    def out_blocks(m):
        m = m.reshape(nc, gpc, hch, p).transpose(0, 1, 3, 2)
        return jnp.einsum('cgph,gk->cgpkh', m, eye).reshape(nc, gpc * p, gpc * hch)

    bmat = jnp.concatenate([in_blocks(jnp.real(b_bar)), in_blocks(jnp.imag(b_bar))], axis=-1)
    cmat = jnp.concatenate([out_blocks(c_re.astype(F32)), out_blocks(-c_im.astype(F32))], axis=1)
    lam_ri = jnp.stack([jnp.real(lam_bar).reshape(nc, gpc * p), jnp.imag(lam_bar).reshape(nc, gpc * p)], axis=1)
    return bmat.astype(BF16), cmat.astype(BF16), lam_ri


def _glu_kernel(z_ref, wg_ref, bg_ref, wo_ref, x_ref, mod_ref, o_ref, slab_ref, zb_ref):
    bsz, tt, d = x_ref.shape
    pitch = slab_ref.shape[1] // tt
    for c in range(d // LANES):
        for t in range(tt):
            slab_ref[c, t * pitch:t * pitch + bsz, :] = z_ref[t * bsz:(t + 1) * bsz, c * LANES:(c + 1) * LANES]
        for b in range(bsz):
            zb_ref[b * tt:(b + 1) * tt, c * LANES:(c + 1) * LANES] = slab_ref[c, pl.ds(b, tt, stride=pitch), :]
    z = zb_ref[...]
    a = jnp.dot(z.astype(BF16), wg_ref[...], preferred_element_type=F32) + bg_ref[...]
    g = z * jax.nn.sigmoid(a)
    y = jnp.dot(g.astype(BF16), wo_ref[...], preferred_element_type=F32)
    for b in range(bsz):
        o_ref[b] = x_ref[b] + mod_ref[b, 2:3] * y[b * tt:(b + 1) * tt, :]


def _glu_residual(z_tm, w_glu, b_glu, w_o, x, mod):
    bsz, s, d = x.shape
    tt = min(TIME_TILE, s)
    pitch = _pitch(bsz)
    return pl.pallas_call(
        _glu_kernel,
        grid=(s // tt,),
        in_specs=[
            pl.BlockSpec((tt * bsz, d), lambda j: (j, 0)),
            _resident(w_glu.shape),
            _resident(b_glu.shape),
            _resident(w_o.shape),
            pl.BlockSpec((bsz, tt, d), lambda j: (0, j, 0)),
            _resident(mod.shape),
        ],
        out_specs=pl.BlockSpec((bsz, tt, d), lambda j: (0, j, 0)),
        out_shape=jax.ShapeDtypeStruct(x.shape, F32),
        scratch_shapes=[pltpu.VMEM((d // LANES, tt * pitch, LANES), F32), pltpu.VMEM((bsz * tt, d), F32)],
        compiler_params=_params("arbitrary"),
        name="glu_residual",
    )(z_tm, w_glu, b_glu, w_o, x, mod)


def _ffn_chunks(f):
    assert f % MXU_COLS == 0
    tiles = f // MXU_COLS
    n = min(FFN_CHUNKS, tiles)
    sizes = [(tiles // n + (1 if c < tiles % n else 0)) * MXU_COLS for c in range(n)]
    starts = [sum(sizes[:c]) for c in range(n)]
    return list(zip(starts, sizes))


def _ffn_kernel(x_ref, xh_ref, mod_ref, gain_ref, wup_ref, cw_ref, cb_ref, wdn_ref, *rest, mixer, final):
    if mixer:
        a_ref, ah_ref, wo_ref = rest[:3]
        rest = rest[3:]
    if final:
        fmod_ref, fgain_ref = rest[:2]
        rest = rest[2:]
    o_ref, hs_ref, acc_ref = rest[:3]
    u_refs = rest[3:]
    tm = x_ref.shape[1]
    f = wdn_ref.shape[0]
    m = mod_ref[0]
    gain = gain_ref[...]
    x, x_halo = x_ref[0], xh_ref[0]
    if mixer:
        a_cat = jnp.concatenate([_load_lane_groups(ah_ref), _load_lane_groups(a_ref)], axis=0)
        y_mix = m[2:3] * jnp.dot(a_cat, wo_ref[...], preferred_element_type=F32)
        x, x_halo = x + y_mix[HALO:], x_halo + y_mix[:HALO]
    h_halo = _modnorm(x_halo, gain, m[3:4], m[4:5])
    h_halo = jnp.where(pl.program_id(1) > 0, h_halo, 0.0)
    hs_ref[0:HALO, :] = h_halo.astype(BF16)
    hs_ref[HALO:, :] = _modnorm(x, gain, m[3:4], m[4:5]).astype(BF16)

    def conv(u_ref, c0, tf):
        w = cw_ref[:, c0:c0 + tf]
        return (w[2:3] * u_ref[pl.ds(HALO, tm), :] + w[1:2] * u_ref[pl.ds(HALO - 1, tm), :]
                + w[0:1] * u_ref[pl.ds(HALO - 2, tm), :] + cb_ref[:, c0:c0 + tf])

    for c, (g0, tf) in enumerate(_ffn_chunks(f)):
        ug_ref, uv_ref = u_refs[2 * c], u_refs[2 * c + 1]
        v0 = f + g0
        hs = hs_ref[...]
        ug_ref[...] = jnp.dot(hs, wup_ref[:, g0:g0 + tf], preferred_element_type=F32)
        uv_ref[...] = jnp.dot(hs, wup_ref[:, v0:v0 + tf], preferred_element_type=F32)
        act = (jax.nn.silu(conv(ug_ref, g0, tf)) * conv(uv_ref, v0, tf)).astype(BF16)
        y = jnp.dot(act, wdn_ref[g0:g0 + tf, :], preferred_element_type=F32)
        if c == 0:
            acc_ref[...] = y
        else:
            acc_ref[...] += y
    out = x + m[5:6] * acc_ref[...]
    if final:
        fm = fmod_ref[0]
        out = _modnorm(out, fgain_ref[...], fm[0:1], fm[1:2])
    o_ref[0] = out


def _conv_ffn(x, mod, gain, w_up, conv_w, conv_b, w_down, mixer=None, final=None):
    bsz, s, d = x.shape
    f = w_down.shape[0]
    tm = min(ROW_TILE, s)
    halo_blocks = tm // HALO
    tile_spec = pl.BlockSpec((1, tm, d), lambda b, j: (b, j, 0))
    halo_spec = pl.BlockSpec((1, HALO, d), lambda b, j: (b, jnp.maximum(j * halo_blocks - 1, 0), 0))
    in_specs = [
        tile_spec,
        halo_spec,
        pl.BlockSpec((1,) + mod.shape[1:], lambda b, j: (b, 0, 0)),
        _resident(gain.shape),
        _resident(w_up.shape),
        _resident(conv_w.shape),
        _resident(conv_b.shape),
        _resident(w_down.shape),
    ]
    args = [x, x, mod, gain, w_up, conv_w, conv_b, w_down]
    if mixer is not None:
        n_groups = mixer[0].shape[1]
        in_specs += [pl.BlockSpec((1, n_groups, tm, LANES), lambda b, j: (b, 0, j, 0)),
                     pl.BlockSpec((1, n_groups, HALO, LANES),
                                  lambda b, j: (b, 0, jnp.maximum(j * halo_blocks - 1, 0), 0)),
                     _resident(mixer[1].shape)]
        args += [mixer[0], mixer[0], mixer[1]]
    if final is not None:
        in_specs += [pl.BlockSpec((1,) + final[0].shape[1:], lambda b, j: (b, 0, 0)), _resident(final[1].shape)]
        args += list(final)
    u_scratch = [pltpu.VMEM((tm + HALO, tf), F32) for _, tf in _ffn_chunks(f) for _ in range(2)]
    return pl.pallas_call(
        functools.partial(_ffn_kernel, mixer=mixer is not None, final=final is not None),
        grid=(bsz, s // tm),
        in_specs=in_specs,
        out_specs=pl.BlockSpec((1, tm, d), lambda b, j: (b, j, 0)),
        out_shape=jax.ShapeDtypeStruct(x.shape, F32),
        scratch_shapes=[pltpu.VMEM((tm + HALO, d), BF16), pltpu.VMEM((tm, d), F32)] + u_scratch,
        compiler_params=_params("arbitrary", "arbitrary"),
        name="conv_ffn",
    )(*args)


def kernel(x, c, norm_mix, norm_ffn, w_mod, b_mod, w_qkv, w_o_attn, w_in_ssm, a_re, a_im, log_dt, b_re, b_im, c_re, c_im, d_skip, w_glu, b_glu, w_o_ssm, w_up, conv_w, conv_b, w_down, norm_out, w_fin, b_fin):
    bsz, s, d = x.shape
    depth = w_mod.shape[0]
    assert s % (4 * Q_BLOCK) == 0 and d % LANES == 0 and bsz % SUBLANES == 0

    mods = _mod_project(c, w_mod, b_mod).reshape(depth, bsz, 6, d)
    fin_mod = _mod_project(c, w_fin[None], b_fin[None]).reshape(bsz, 2, d)

    for i in range(depth):
        mod = mods[i]
        j = i // 2
        gain_mix = norm_mix[i][None]
        if i % 2 == 0:
            wq = (w_qkv[j][:, :d] * (-math.log2(math.e) * HEAD_DIM ** -0.5)).astype(BF16)
            wk = w_qkv[j][:, d:2 * d].astype(BF16)
            wvt = w_qkv[j][:, 2 * d:].T.astype(BF16)
            q, k_perm, vt_perm = _qkv_project(x, mod, gain_mix, wq, wk, wvt)
            mixer = (_attention(q, k_perm, vt_perm), w_o_attn[j].astype(BF16))
        else:
            mixer = None
            u_tm = _norm_project_time_major(x, mod, gain_mix, w_in_ssm[j].astype(BF16))
            bmat, cmat, lam = _ssm_matrices(a_re[j], a_im[j], log_dt[j], b_re[j], b_im[j], c_re[j], c_im[j])
            z_tm = _ssm(u_tm, bmat, cmat, lam, d_skip[j][None], nb=bsz)
            x = _glu_residual(z_tm, w_glu[j].astype(BF16), b_glu[j][None], w_o_ssm[j].astype(BF16), x, mod)
        final = (fin_mod, norm_out[None]) if i == depth - 1 else None
        x = _conv_ffn(x, mod, norm_ffn[i][None], w_up[i].astype(BF16), conv_w[i], conv_b[i][None],
                      w_down[i].astype(BF16), mixer=mixer, final=final)
    return x
```

```python
import functools
import math

import jax
import jax.numpy as jnp
from jax import lax
from jax.experimental import pallas as pl
from jax.experimental.pallas import tpu as pltpu

F32 = jnp.float32
BF16 = jnp.bfloat16

HEAD_DIM = 64
SSM_GROUP = 16
STATE = 64
CONV_W = 3
EPS = 1e-6

LANES = 128
SUBLANES = 8
MXU_COLS = 256
Q_BLOCK = LANES
KEY_BLOCK = 64
KEY_ROWS = KEY_BLOCK // SUBLANES
PAST_BLOCKS = 4
WINDOW_BLOCKS = PAST_BLOCKS + Q_BLOCK // KEY_BLOCK
KEY_PAD = PAST_BLOCKS * KEY_BLOCK
WINDOW = WINDOW_BLOCKS * KEY_BLOCK
V_WINDOW = -(-WINDOW // LANES) * LANES
HALO = 16
SSM_CHUNK = LANES
VMEM_LIMIT = 56 * 1024 * 1024
MASKED_SCORE = 1e30
SATURATED_SCORE = 64.0

ROW_TILE = 512
FFN_CHUNKS = 2
TIME_TILE = 32
SSM_TIME_TILE = 128
MOD_COL_TILE = 2048


def _params(*sem):
    return pltpu.CompilerParams(dimension_semantics=sem, vmem_limit_bytes=VMEM_LIMIT)


def _resident(shape):
    nd = len(shape)
    return pl.BlockSpec(shape, lambda *_: (0,) * nd, pipeline_mode=pl.Buffered(1))


def _modnorm(x, gain, shift, scale):
    ms = jnp.mean(x * x, axis=-1, keepdims=True)
    y = x * lax.rsqrt(ms + EPS) * gain
    return y * (1.0 + scale) + shift


def _mod_kernel(c_ref, w_ref, b_ref, o_ref):
    ca = jax.nn.silu(c_ref[...])
    o_ref[0] = jnp.dot(ca, w_ref[0], preferred_element_type=F32) + b_ref[0]


def _mod_project(c, w, b):
    nl, d, n = w.shape
    bsz = c.shape[0]
    tn = min(MOD_COL_TILE, n)
    return pl.pallas_call(
        _mod_kernel,
        grid=(nl, n // tn),
        in_specs=[
            pl.BlockSpec((bsz, d), lambda l, j: (0, 0)),
            pl.BlockSpec((1, d, tn), lambda l, j: (l, 0, j)),
            pl.BlockSpec((1, 1, tn), lambda l, j: (l, 0, j)),
        ],
        out_specs=pl.BlockSpec((1, bsz, tn), lambda l, j: (l, 0, j)),
        out_shape=jax.ShapeDtypeStruct((nl, bsz, n), F32),
        compiler_params=_params("arbitrary", "arbitrary"),
        name="adaln_project",
    )(c, w, b.reshape(nl, 1, n))


def _store_lane_groups(o_ref, y):
    for g in range(o_ref.shape[1]):
        o_ref[0, g] = y[:, g * LANES:(g + 1) * LANES].astype(o_ref.dtype)


def _load_lane_groups(a_ref):
    return jnp.concatenate([a_ref[0, g] for g in range(a_ref.shape[1])], axis=1)


def _qkv_kernel(x_ref, mod_ref, gain_ref, wq_ref, wk_ref, wvt_ref, q_ref, k_ref, vt_ref, h_ref, hp_ref):
    tm = x_ref.shape[1]
    m = mod_ref[0]
    h = _modnorm(x_ref[0], gain_ref[...], m[0:1], m[1:2])
    _store_lane_groups(q_ref, jnp.dot(h.astype(BF16), wq_ref[...], preferred_element_type=F32))
    for c in range(h_ref.shape[0]):
        h_ref[c] = h[:, c * LANES:(c + 1) * LANES]
        for blk in range(tm // KEY_BLOCK):
            for r2 in range(KEY_ROWS // 2):
                base = blk * KEY_BLOCK
                lo = h_ref[c, pl.ds(base + 2 * r2, SUBLANES, stride=KEY_ROWS), :]
                hi = h_ref[c, pl.ds(base + 2 * r2 + 1, SUBLANES, stride=KEY_ROWS), :]
                hp_ref[base + 2 * SUBLANES * r2:base + 2 * SUBLANES * (r2 + 1), c * LANES:(c + 1) * LANES] = (
                    jnp.concatenate([lo, hi], axis=0).astype(BF16))
    hp = hp_ref[...]
    _store_lane_groups(k_ref, jnp.dot(hp, wk_ref[...], preferred_element_type=F32))
    vt_ref[0] = lax.dot_general(wvt_ref[...], hp, (((1,), (1,)), ((), ())),
                                preferred_element_type=F32).astype(vt_ref.dtype)


def _qkv_project(x, mod, gain, wq, wk, wvt):
    bsz, s, d = x.shape
    tm = min(ROW_TILE, s)
    row_spec = pl.BlockSpec((1, tm, d), lambda b, j: (b, j, 0))
    group_spec = pl.BlockSpec((1, d // LANES, tm, LANES), lambda b, j: (b, 0, j, 0))
    return pl.pallas_call(
        _qkv_kernel,
        grid=(bsz, s // tm),
        in_specs=[
            row_spec,
            pl.BlockSpec((1,) + mod.shape[1:], lambda b, j: (b, 0, 0)),
            _resident(gain.shape),
            _resident(wq.shape),
            _resident(wk.shape),
            _resident(wvt.shape),
        ],
        out_specs=[group_spec, group_spec, pl.BlockSpec((1, d, tm), lambda b, j: (b, 0, j))],
        out_shape=[jax.ShapeDtypeStruct((bsz, d // LANES, s, LANES), BF16),
                   jax.ShapeDtypeStruct((bsz, d // LANES, s, LANES), BF16),
                   jax.ShapeDtypeStruct((bsz, d, s), BF16)],
        scratch_shapes=[pltpu.VMEM((d // LANES, tm, LANES), F32), pltpu.VMEM((tm, d), BF16)],
        compiler_params=_params("arbitrary", "arbitrary"),
        name="qkv_project",
    )(x, mod, gain, wq, wk, wvt)


def _pitch(rows):
    p = -(-rows // SUBLANES)
    return SUBLANES * (p if p % 2 else p + 1)


def _norm_matmul_kernel(x_ref, mod_ref, gain_ref, w_ref, o_ref, hs_ref, slab_ref):
    bsz, tt, d = x_ref.shape
    pitch = slab_ref.shape[1] // bsz
    for b in range(bsz):
        h = _modnorm(x_ref[b], gain_ref[...], mod_ref[b, 0:1], mod_ref[b, 1:2])
        hs_ref[b * tt:(b + 1) * tt, :] = h.astype(BF16)
    u = jnp.dot(hs_ref[...], w_ref[...], preferred_element_type=F32)
    for c in range(d // LANES):
        for b in range(bsz):
            slab_ref[c, b * pitch:b * pitch + tt, :] = u[b * tt:(b + 1) * tt, c * LANES:(c + 1) * LANES]
        for t in range(tt):
            o_ref[t * bsz:(t + 1) * bsz, c * LANES:(c + 1) * LANES] = slab_ref[c, pl.ds(t, bsz, stride=pitch), :]


def _norm_project_time_major(x, mod, gain, w):
    bsz, s, d = x.shape
    n = w.shape[1]
    tt = min(TIME_TILE, s)
    pitch = _pitch(tt)
    return pl.pallas_call(
        _norm_matmul_kernel,
        grid=(s // tt,),
        in_specs=[
            pl.BlockSpec((bsz, tt, d), lambda j: (0, j, 0)),
            _resident(mod.shape),
            _resident(gain.shape),
            _resident(w.shape),
        ],
        out_specs=pl.BlockSpec((tt * bsz, n), lambda j: (j, 0)),
        out_shape=jax.ShapeDtypeStruct((s * bsz, n), F32),
        scratch_shapes=[pltpu.VMEM((bsz * tt, d), BF16), pltpu.VMEM((n // LANES, bsz * pitch, LANES), F32)],
        compiler_params=_params("arbitrary"),
        name="norm_project",
    )(x, mod, gain, w)


def _streams(z_view, col, base, kq, s_view):
    run = jnp.ones((SUBLANES, Q_BLOCK), F32)
    for r in range(KEY_ROWS - 1, -1, -1):
        lo = base + SUBLANES * r
        zn = z_view[lo:lo + SUBLANES, col:col + Q_BLOCK]
        if kq is not None:
            zn = jnp.where(kq > r, zn, MASKED_SCORE)
        e = jnp.exp2(zn)
        a = run * (1.0 / (1.0 + e))
        s_view[lo:lo + SUBLANES, :] = a
        run = jnp.where(zn > SATURATED_SCORE, run, e * a)
    return run


def _suffix(tot, row):
    p = tot
    for d in (1, 2, 4):
        p = jnp.where(row + d < SUBLANES, p * pltpu.roll(p, SUBLANES - d, 0), p)
    excl = jnp.where(row < SUBLANES - 1, pltpu.roll(p, SUBLANES - 1, 0), 1.0)
    return excl, jnp.broadcast_to(p[0:1, :], p.shape)


def _apply_offsets(s_view, a_view, base, off):
    off2 = jnp.concatenate([off, off], axis=0)
    for r2 in range(KEY_BLOCK // 16):
        lo = base + 16 * r2
        a_view[lo:lo + 16, :] = (s_view[lo:lo + 16, :] * off2).astype(BF16)


def _nt_dot(a, b):
    return lax.dot_general(a, b, (((1,), (1,)), ((), ())), preferred_element_type=F32)


def _attn_kernel(q_ref, k_ref, vt_ref, o_ref, kp_ref, vp_ref, z_ref, s_ref, a_ref, t_ref, oacc_ref, c_ref,
                 flag_ref):
    s_len = q_ref.shape[0]
    n_heads = LANES // HEAD_DIM
    row = lax.broadcasted_iota(jnp.int32, (SUBLANES, Q_BLOCK), 0)
    lane = lax.broadcasted_iota(jnp.int32, (SUBLANES, Q_BLOCK), 1)
    kqs = [lane - KEY_ROWS * row - (b - PAST_BLOCKS) * KEY_BLOCK if b >= PAST_BLOCKS else None
           for b in range(WINDOW_BLOCKS)]
    q_lane = lax.broadcasted_iota(jnp.int32, (Q_BLOCK, LANES), 1)

    n_blocks = s_len // Q_BLOCK

    kp_ref[0:KEY_PAD, :] = jnp.zeros((KEY_PAD, LANES), kp_ref.dtype)
    kp_ref[KEY_PAD:, :] = k_ref[...]
    vp_ref[:, 0:KEY_PAD] = jnp.zeros((LANES, KEY_PAD), vp_ref.dtype)
    vp_ref[:, KEY_PAD:KEY_PAD + s_len] = vt_ref[...]
    if V_WINDOW > WINDOW:
        vp_ref[:, KEY_PAD + s_len:] = jnp.zeros((LANES, V_WINDOW - WINDOW), vp_ref.dtype)

    def stacked_queries(w):
        q = q_ref[pl.ds(pl.multiple_of(w * Q_BLOCK, Q_BLOCK), Q_BLOCK), :]
        zero = jnp.zeros_like(q)
        return jnp.concatenate([jnp.where((q_lane >= h * HEAD_DIM) & (q_lane < (h + 1) * HEAD_DIM), q, zero)
                                for h in range(n_heads)], axis=0)

    def store_out(w):
        o_ref[pl.ds(pl.multiple_of(w * Q_BLOCK, Q_BLOCK), Q_BLOCK), :] = (
            oacc_ref[w].T.astype(o_ref.dtype))

    def scores(p, slot, i):
        w = jnp.minimum(2 * p + i, n_blocks - 1)
        k_win = kp_ref[pl.ds(pl.multiple_of(w * Q_BLOCK, Q_BLOCK), WINDOW), :]
        z_ref[slot, i] = _nt_dot(k_win, stacked_queries(w))

    def weights(p, slot, i):
        w = 2 * p + i
        for h in range(n_heads):
            z_view, s_view, a_view = z_ref.at[slot, i], s_ref.at[slot, i, h], a_ref.at[slot, i, h]
            tots = [_streams(z_view, h * Q_BLOCK, b * KEY_BLOCK, kqs[b], s_view)
                    for b in range(WINDOW_BLOCKS)]
            carry = None
            for b in range(WINDOW_BLOCKS - 1, -1, -1):
                excl, total = _suffix(tots[b], row)
                _apply_offsets(s_view, a_view, b * KEY_BLOCK, excl if carry is None else carry * excl)
                carry = total if carry is None else carry * total
            c_ref[w, h] = carry
        alive = jnp.max(jnp.maximum(c_ref[w, 0], c_ref[w, 1])) > 0.0
        flag_ref[w] = alive.astype(jnp.int32)

    def values(p, slot, i):
        w = jnp.maximum(2 * p + i, 0)
        k0 = pl.multiple_of(w * Q_BLOCK, Q_BLOCK)
        for h in range(n_heads):
            v_win = vp_ref[h * HEAD_DIM:(h + 1) * HEAD_DIM, pl.ds(k0, V_WINDOW)]
            oacc_ref[w, h * HEAD_DIM:(h + 1) * HEAD_DIM, :] = jnp.dot(v_win, a_ref[slot, i, h],
                                                                      preferred_element_type=F32)
        store_out(w)

    for i in range(2):
        scores(0, 0, i)
    a_ref[...] = jnp.zeros(a_ref.shape, a_ref.dtype)

    def two_pairs(m, _):
        p = 2 * m
        for p_w, slot in ((p, 0), (p + 1, 1)):
            for i in range(2):
                scores(p_w + 1, 1 - slot, i)
                weights(p_w, slot, i)
                values(p_w - 1, 1 - slot, i)
        return 0

    lax.fori_loop(0, n_blocks // 4, two_pairs, 0)
    for i in range(2):
        values(n_blocks // 2 - 1, 1, i)

    def older_keys(w):
        q2 = stacked_queries(w)
        n_sub = Q_BLOCK // KEY_BLOCK

        def cond(st):
            return jnp.logical_and(st[0] >= 1, st[1] > 0)

        def body(st):
            j = st[0]
            k0 = pl.multiple_of(j * Q_BLOCK, Q_BLOCK)
            z_ref[0, 0, 0:Q_BLOCK, :] = _nt_dot(kp_ref[pl.ds(k0, Q_BLOCK), :], q2)
            carries = []
            for h in range(n_heads):
                s_view, a_view = s_ref.at[0, 0, h], t_ref.at[h]
                tots = [_streams(z_ref.at[0, 0], h * Q_BLOCK, b * KEY_BLOCK, None, s_view) for b in range(n_sub)]
                carry = c_ref[w, h]
                for b in range(n_sub - 1, -1, -1):
                    excl, total = _suffix(tots[b], row)
                    _apply_offsets(s_view, a_view, b * KEY_BLOCK, carry * excl)
                    carry = carry * total
                v_blk = vp_ref[h * HEAD_DIM:(h + 1) * HEAD_DIM, pl.ds(k0, Q_BLOCK)]
                oacc_ref[w, h * HEAD_DIM:(h + 1) * HEAD_DIM, :] += jnp.dot(
                    v_blk, t_ref[h], preferred_element_type=F32)
                carries.append(carry)
                c_ref[w, h] = carry
            alive = jnp.max(functools.reduce(jnp.maximum, carries)) > 0.0
            return (j - 1, alive.astype(jnp.int32))

        lax.while_loop(cond, body, (w - 1, jnp.int32(1)))

    def finish(w, _):
        @pl.when(flag_ref[w] > 0)
        def _():
            older_keys(w)
            store_out(w)

        return 0

    lax.fori_loop(2, n_blocks, finish, 0)


def _attention(q, k_perm, vt_perm):
    bsz, n_groups, s, _ = q.shape
    n_heads = LANES // HEAD_DIM
    n_blocks = s // Q_BLOCK
    group_spec = pl.BlockSpec((None, None, s, LANES), lambda b, h: (b, h, 0, 0))
    return pl.pallas_call(
        _attn_kernel,
        grid=(bsz, n_groups),
        in_specs=[group_spec, group_spec, pl.BlockSpec((None, LANES, s), lambda b, h: (b, h, 0))],
        out_specs=group_spec,
        out_shape=jax.ShapeDtypeStruct(q.shape, BF16),
        scratch_shapes=[
            pltpu.VMEM((KEY_PAD + s, LANES), BF16),
            pltpu.VMEM((LANES, KEY_PAD + s + V_WINDOW - WINDOW), BF16),
            pltpu.VMEM((2, 2, WINDOW, n_heads * Q_BLOCK), F32),
            pltpu.VMEM((2, 2, n_heads, WINDOW, Q_BLOCK), F32),
            pltpu.VMEM((2, 2, n_heads, V_WINDOW, Q_BLOCK), BF16),
            pltpu.VMEM((n_heads, Q_BLOCK, Q_BLOCK), BF16),
            pltpu.VMEM((n_blocks, LANES, Q_BLOCK), F32),
            pltpu.VMEM((n_blocks, n_heads, SUBLANES, Q_BLOCK), F32),
            pltpu.SMEM((n_blocks,), jnp.int32),
        ],
        compiler_params=_params("arbitrary", "arbitrary"),
        name="stickbreak_attention",
    )(q, k_perm, vt_perm)


def _ssm_kernel(u_ref, bmat_ref, cmat_ref, lam_ref, d_ref, z_ref, xs_ref, st_ref, *, nb, row_chunk):
    rows = u_ref.shape[0]
    half = st_ref.shape[1] // 2

    @pl.when(pl.program_id(1) == 0)
    def _():
        st_ref[...] = jnp.zeros_like(st_ref)

    a_re = jnp.broadcast_to(lam_ref[0, 0:1, :], (nb, half))
    a_im = jnp.broadcast_to(lam_ref[0, 1:2, :], (nb, half))
    x_re, x_im = st_ref[:, :half], st_ref[:, half:]

    for r0 in range(0, rows, row_chunk):
        xs_ref[r0:r0 + row_chunk, :] = jnp.dot(u_ref[r0:r0 + row_chunk, :].astype(BF16), bmat_ref[0],
                                               preferred_element_type=F32)
        for r in range(r0, r0 + row_chunk, nb):
            n_re = a_re * x_re - a_im * x_im + xs_ref[r:r + nb, :half]
            n_im = a_re * x_im + a_im * x_re + xs_ref[r:r + nb, half:]
            xs_ref[r:r + nb, :half] = n_re
            xs_ref[r:r + nb, half:] = n_im
            x_re, x_im = n_re, n_im
        y = jnp.dot(xs_ref[r0:r0 + row_chunk, :].astype(BF16), cmat_ref[0], preferred_element_type=F32)
        z_ref[r0:r0 + row_chunk, :] = jax.nn.gelu(y + d_ref[...] * u_ref[r0:r0 + row_chunk, :])

    st_ref[:, :half] = x_re
    st_ref[:, half:] = x_im


def _ssm(u_tm, bmat, cmat, lam, d_skip, nb):
    rows_total, d = u_tm.shape
    s = rows_total // nb
    tt = min(SSM_TIME_TILE, s)
    rows = tt * nb
    n_state = bmat.shape[-1]
    return pl.pallas_call(
        functools.partial(_ssm_kernel, nb=nb, row_chunk=min(256, rows)),
        grid=(d // SSM_CHUNK, s // tt),
        in_specs=[
            pl.BlockSpec((rows, SSM_CHUNK), lambda c, i: (i, c)),
            pl.BlockSpec((1, SSM_CHUNK, n_state), lambda c, i: (c, 0, 0)),
            pl.BlockSpec((1, n_state, SSM_CHUNK), lambda c, i: (c, 0, 0)),
            pl.BlockSpec((1, 2, n_state // 2), lambda c, i: (c, 0, 0)),
            pl.BlockSpec((1, SSM_CHUNK), lambda c, i: (0, c)),
        ],
        out_specs=pl.BlockSpec((rows, SSM_CHUNK), lambda c, i: (i, c)),
        out_shape=jax.ShapeDtypeStruct(u_tm.shape, F32),
        scratch_shapes=[pltpu.VMEM((rows, n_state), F32), pltpu.VMEM((nb, n_state), F32)],
        compiler_params=_params("arbitrary", "arbitrary"),
        name="s5_scan",
    )(u_tm, bmat, cmat, lam, d_skip)


def _ssm_matrices(a_re, a_im, log_dt, b_re, b_im, c_re, c_im):
    g, p = a_re.shape
    hch = b_re.shape[-1]
    gpc = SSM_CHUNK // hch
    nc = g // gpc
    lam = lax.complex(a_re.astype(F32), a_im.astype(F32))
    dt = jnp.exp(log_dt.astype(F32))[:, None]
    lam_bar = jnp.exp(lam * dt)
    b_bar = ((lam_bar - 1) / lam)[..., None] * lax.complex(b_re.astype(F32), b_im.astype(F32))
    eye = jnp.eye(gpc, dtype=F32)

    def in_blocks(m):
        m = m.reshape(nc, gpc, p, hch).transpose(0, 1, 3, 2)
        return jnp.einsum('cghp,gk->cghkp', m, eye).reshape(nc, gpc * hch, gpc * p)

    def out_blocks(m):
        m = m.reshape(nc, gpc, hch, p).transpose(0, 1, 3, 2)
        return jnp.einsum('cgph,gk->cgpkh', m, eye).reshape(nc, gpc * p, gpc * hch)

    bmat = jnp.concatenate([in_blocks(jnp.real(b_bar)), in_blocks(jnp.imag(b_bar))], axis=-1)
    cmat = jnp.concatenate([out_blocks(c_re.astype(F32)), out_blocks(-c_im.astype(F32))], axis=1)
    lam_ri = jnp.stack([jnp.real(lam_bar).reshape(nc, gpc * p), jnp.imag(lam_bar).reshape(nc, gpc * p)], axis=1)
    return bmat.astype(BF16), cmat.astype(BF16), lam_ri


def _glu_kernel(z_ref, wg_ref, bg_ref, wo_ref, x_ref, mod_ref, o_ref, slab_ref, zb_ref):
    bsz, tt, d = x_ref.shape
    pitch = slab_ref.shape[1] // tt
    for c in range(d // LANES):
        for t in range(tt):
            slab_ref[c, t * pitch:t * pitch + bsz, :] = z_ref[t * bsz:(t + 1) * bsz, c * LANES:(c + 1) * LANES]
        for b in range(bsz):
            zb_ref[b * tt:(b + 1) * tt, c * LANES:(c + 1) * LANES] = slab_ref[c, pl.ds(b, tt, stride=pitch), :]
    z = zb_ref[...]
    a = jnp.dot(z.astype(BF16), wg_ref[...], preferred_element_type=F32) + bg_ref[...]
    g = z * jax.nn.sigmoid(a)
    y = jnp.dot(g.astype(BF16), wo_ref[...], preferred_element_type=F32)
    for b in range(bsz):
        o_ref[b] = x_ref[b] + mod_ref[b, 2:3] * y[b * tt:(b + 1) * tt, :]


def _glu_residual(z_tm, w_glu, b_glu, w_o, x, mod):
    bsz, s, d = x.shape
    tt = min(TIME_TILE, s)
    pitch = _pitch(bsz)
    return pl.pallas_call(
        _glu_kernel,
        grid=(s // tt,),
        in_specs=[
            pl.BlockSpec((tt * bsz, d), lambda j: (j, 0)),
            _resident(w_glu.shape),
            _resident(b_glu.shape),
            _resident(w_o.shape),
            pl.BlockSpec((bsz, tt, d), lambda j: (0, j, 0)),
            _resident(mod.shape),
        ],
        out_specs=pl.BlockSpec((bsz, tt, d), lambda j: (0, j, 0)),
        out_shape=jax.ShapeDtypeStruct(x.shape, F32),
        scratch_shapes=[pltpu.VMEM((d // LANES, tt * pitch, LANES), F32), pltpu.VMEM((bsz * tt, d), F32)],
        compiler_params=_params("arbitrary"),
        name="glu_residual",
    )(z_tm, w_glu, b_glu, w_o, x, mod)


def _ffn_chunks(f):
    assert f % MXU_COLS == 0
    tiles = f // MXU_COLS
    n = min(FFN_CHUNKS, tiles)
    sizes = [(tiles // n + (1 if c < tiles % n else 0)) * MXU_COLS for c in range(n)]
    starts = [sum(sizes[:c]) for c in range(n)]
    return list(zip(starts, sizes))


def _ffn_kernel(x_ref, xh_ref, mod_ref, gain_ref, wup_ref, cw_ref, cb_ref, wdn_ref, *rest, mixer, final):
    if mixer:
        a_ref, ah_ref, wo_ref = rest[:3]
        rest = rest[3:]
    if final:
        fmod_ref, fgain_ref = rest[:2]
        rest = rest[2:]
    o_ref, hs_ref, acc_ref = rest[:3]
    u_refs = rest[3:]
    tm = x_ref.shape[1]
    f = wdn_ref.shape[0]
    m = mod_ref[0]
    gain = gain_ref[...]
    x, x_halo = x_ref[0], xh_ref[0]
    if mixer:
        a_cat = jnp.concatenate([_load_lane_groups(ah_ref), _load_lane_groups(a_ref)], axis=0)
        y_mix = m[2:3] * jnp.dot(a_cat, wo_ref[...], preferred_element_type=F32)
        x, x_halo = x + y_mix[HALO:], x_halo + y_mix[:HALO]
    h_halo = _modnorm(x_halo, gain, m[3:4], m[4:5])
    h_halo = jnp.where(pl.program_id(1) > 0, h_halo, 0.0)
    hs_ref[0:HALO, :] = h_halo.astype(BF16)
    hs_ref[HALO:, :] = _modnorm(x, gain, m[3:4], m[4:5]).astype(BF16)

    def conv(u_ref, c0, tf):
        w = cw_ref[:, c0:c0 + tf]
        return (w[2:3] * u_ref[pl.ds(HALO, tm), :] + w[1:2] * u_ref[pl.ds(HALO - 1, tm), :]
                + w[0:1] * u_ref[pl.ds(HALO - 2, tm), :] + cb_ref[:, c0:c0 + tf])

    for c, (g0, tf) in enumerate(_ffn_chunks(f)):
        ug_ref, uv_ref = u_refs[2 * c], u_refs[2 * c + 1]
        v0 = f + g0
        hs = hs_ref[...]
        ug_ref[...] = jnp.dot(hs, wup_ref[:, g0:g0 + tf], preferred_element_type=F32)
        uv_ref[...] = jnp.dot(hs, wup_ref[:, v0:v0 + tf], preferred_element_type=F32)
        act = (jax.nn.silu(conv(ug_ref, g0, tf)) * conv(uv_ref, v0, tf)).astype(BF16)
        y = jnp.dot(act, wdn_ref[g0:g0 + tf, :], preferred_element_type=F32)
        if c == 0:
            acc_ref[...] = y
        else:
            acc_ref[...] += y
    out = x + m[5:6] * acc_ref[...]
    if final:
        fm = fmod_ref[0]
        out = _modnorm(out, fgain_ref[...], fm[0:1], fm[1:2])
    o_ref[0] = out


def _conv_ffn(x, mod, gain, w_up, conv_w, conv_b, w_down, mixer=None, final=None):
    bsz, s, d = x.shape
    f = w_down.shape[0]
    tm = min(ROW_TILE, s)
    halo_blocks = tm // HALO
    tile_spec = pl.BlockSpec((1, tm, d), lambda b, j: (b, j, 0))
    halo_spec = pl.BlockSpec((1, HALO, d), lambda b, j: (b, jnp.maximum(j * halo_blocks - 1, 0), 0))
    in_specs = [
        tile_spec,
        halo_spec,
        pl.BlockSpec((1,) + mod.shape[1:], lambda b, j: (b, 0, 0)),
        _resident(gain.shape),
        _resident(w_up.shape),
        _resident(conv_w.shape),
        _resident(conv_b.shape),
        _resident(w_down.shape),
    ]
    args = [x, x, mod, gain, w_up, conv_w, conv_b, w_down]
    if mixer is not None:
        n_groups = mixer[0].shape[1]
        in_specs += [pl.BlockSpec((1, n_groups, tm, LANES), lambda b, j: (b, 0, j, 0)),
                     pl.BlockSpec((1, n_groups, HALO, LANES),
                                  lambda b, j: (b, 0, jnp.maximum(j * halo_blocks - 1, 0), 0)),
                     _resident(mixer[1].shape)]
        args += [mixer[0], mixer[0], mixer[1]]
    if final is not None:
        in_specs += [pl.BlockSpec((1,) + final[0].shape[1:], lambda b, j: (b, 0, 0)), _resident(final[1].shape)]
        args += list(final)
    u_scratch = [pltpu.VMEM((tm + HALO, tf), F32) for _, tf in _ffn_chunks(f) for _ in range(2)]
    return pl.pallas_call(
        functools.partial(_ffn_kernel, mixer=mixer is not None, final=final is not None),
        grid=(bsz, s // tm),
        in_specs=in_specs,
        out_specs=pl.BlockSpec((1, tm, d), lambda b, j: (b, j, 0)),
        out_shape=jax.ShapeDtypeStruct(x.shape, F32),
        scratch_shapes=[pltpu.VMEM((tm + HALO, d), BF16), pltpu.VMEM((tm, d), F32)] + u_scratch,
        compiler_params=_params("arbitrary", "arbitrary"),
        name="conv_ffn",
    )(*args)


def kernel(x, c, norm_mix, norm_ffn, w_mod, b_mod, w_qkv, w_o_attn, w_in_ssm, a_re, a_im, log_dt, b_re, b_im, c_re, c_im, d_skip, w_glu, b_glu, w_o_ssm, w_up, conv_w, conv_b, w_down, norm_out, w_fin, b_fin):
    bsz, s, d = x.shape
    depth = w_mod.shape[0]
    assert s % (4 * Q_BLOCK) == 0 and d % LANES == 0 and bsz % SUBLANES == 0

    mods = _mod_project(c, w_mod, b_mod).reshape(depth, bsz, 6, d)
    fin_mod = _mod_project(c, w_fin[None], b_fin[None]).reshape(bsz, 2, d)

    for i in range(depth):
        mod = mods[i]
        j = i // 2
        gain_mix = norm_mix[i][None]
        if i % 2 == 0:
            wq = (w_qkv[j][:, :d] * (-math.log2(math.e) * HEAD_DIM ** -0.5)).astype(BF16)
            wk = w_qkv[j][:, d:2 * d].astype(BF16)
            wvt = w_qkv[j][:, 2 * d:].T.astype(BF16)
            q, k_perm, vt_perm = _qkv_project(x, mod, gain_mix, wq, wk, wvt)
            mixer = (_attention(q, k_perm, vt_perm), w_o_attn[j].astype(BF16))
        else:
            mixer = None
            u_tm = _norm_project_time_major(x, mod, gain_mix, w_in_ssm[j].astype(BF16))
            bmat, cmat, lam = _ssm_matrices(a_re[j], a_im[j], log_dt[j], b_re[j], b_im[j], c_re[j], c_im[j])
            z_tm = _ssm(u_tm, bmat, cmat, lam, d_skip[j][None], nb=bsz)
            x = _glu_residual(z_tm, w_glu[j].astype(BF16), b_glu[j][None], w_o_ssm[j].astype(BF16), x, mod)
        final = (fin_mod, norm_out[None]) if i == depth - 1 else None
        x = _conv_ffn(x, mod, norm_ffn[i][None], w_up[i].astype(BF16), conv_w[i], conv_b[i][None],
                      w_down[i].astype(BF16), mixer=mixer, final=final)
    return x
```

```python
import functools
import math

import jax
import jax.numpy as jnp
from jax import lax
from jax.experimental import pallas as pl
from jax.experimental.pallas import tpu as pltpu

F32 = jnp.float32
BF16 = jnp.bfloat16

HEAD_DIM = 64
SSM_GROUP = 16
STATE = 64
CONV_W = 3
EPS = 1e-6

LANES = 128
SUBLANES = 8
MXU_COLS = 256
Q_BLOCK = LANES
KEY_BLOCK = 64
KEY_ROWS = KEY_BLOCK // SUBLANES
PAST_BLOCKS = 4
WINDOW_BLOCKS = PAST_BLOCKS + Q_BLOCK // KEY_BLOCK
KEY_PAD = PAST_BLOCKS * KEY_BLOCK
WINDOW = WINDOW_BLOCKS * KEY_BLOCK
V_WINDOW = -(-WINDOW // LANES) * LANES
HALO = 16
SSM_CHUNK = LANES
VMEM_LIMIT = 56 * 1024 * 1024
MASKED_SCORE = 1e30

ROW_TILE = 512
FFN_CHUNKS = 2
TIME_TILE = 32
SSM_TIME_TILE = 128
MOD_COL_TILE = 2048


def _params(*sem):
    return pltpu.CompilerParams(dimension_semantics=sem, vmem_limit_bytes=VMEM_LIMIT)


def _resident(shape):
    nd = len(shape)
    return pl.BlockSpec(shape, lambda *_: (0,) * nd, pipeline_mode=pl.Buffered(1))


def _modnorm(x, gain, shift, scale):
    ms = jnp.mean(x * x, axis=-1, keepdims=True)
    y = x * lax.rsqrt(ms + EPS) * gain
    return y * (1.0 + scale) + shift


def _mod_kernel(c_ref, w_ref, b_ref, o_ref):
    ca = jax.nn.silu(c_ref[...])
    o_ref[0] = jnp.dot(ca, w_ref[0], preferred_element_type=F32) + b_ref[0]


def _mod_project(c, w, b):
    nl, d, n = w.shape
    bsz = c.shape[0]
    tn = min(MOD_COL_TILE, n)
    return pl.pallas_call(
        _mod_kernel,
        grid=(nl, n // tn),
        in_specs=[
            pl.BlockSpec((bsz, d), lambda l, j: (0, 0)),
            pl.BlockSpec((1, d, tn), lambda l, j: (l, 0, j)),
            pl.BlockSpec((1, 1, tn), lambda l, j: (l, 0, j)),
        ],
        out_specs=pl.BlockSpec((1, bsz, tn), lambda l, j: (l, 0, j)),
        out_shape=jax.ShapeDtypeStruct((nl, bsz, n), F32),
        compiler_params=_params("arbitrary", "arbitrary"),
        name="adaln_project",
    )(c, w, b.reshape(nl, 1, n))


def _store_lane_groups(o_ref, y):
    for g in range(o_ref.shape[1]):
        o_ref[0, g] = y[:, g * LANES:(g + 1) * LANES].astype(o_ref.dtype)


def _load_lane_groups(a_ref):
    return jnp.concatenate([a_ref[0, g] for g in range(a_ref.shape[1])], axis=1)


def _qkv_kernel(x_ref, mod_ref, gain_ref, wq_ref, wk_ref, wvt_ref, q_ref, k_ref, vt_ref, h_ref, hp_ref):
    tm = x_ref.shape[1]
    m = mod_ref[0]
    h = _modnorm(x_ref[0], gain_ref[...], m[0:1], m[1:2])
    _store_lane_groups(q_ref, jnp.dot(h.astype(BF16), wq_ref[...], preferred_element_type=F32))
    for c in range(h_ref.shape[0]):
        h_ref[c] = h[:, c * LANES:(c + 1) * LANES]
        for blk in range(tm // KEY_BLOCK):
            for r2 in range(KEY_ROWS // 2):
                base = blk * KEY_BLOCK
                lo = h_ref[c, pl.ds(base + 2 * r2, SUBLANES, stride=KEY_ROWS), :]
                hi = h_ref[c, pl.ds(base + 2 * r2 + 1, SUBLANES, stride=KEY_ROWS), :]
                hp_ref[base + 2 * SUBLANES * r2:base + 2 * SUBLANES * (r2 + 1), c * LANES:(c + 1) * LANES] = (
                    jnp.concatenate([lo, hi], axis=0).astype(BF16))
    hp = hp_ref[...]
    _store_lane_groups(k_ref, jnp.dot(hp, wk_ref[...], preferred_element_type=F32))
    vt_ref[0] = lax.dot_general(wvt_ref[...], hp, (((1,), (1,)), ((), ())),
                                preferred_element_type=F32).astype(vt_ref.dtype)


def _qkv_project(x, mod, gain, wq, wk, wvt):
    bsz, s, d = x.shape
    tm = min(ROW_TILE, s)
    row_spec = pl.BlockSpec((1, tm, d), lambda b, j: (b, j, 0))
    group_spec = pl.BlockSpec((1, d // LANES, tm, LANES), lambda b, j: (b, 0, j, 0))
    return pl.pallas_call(
        _qkv_kernel,
        grid=(bsz, s // tm),
        in_specs=[
            row_spec,
            pl.BlockSpec((1,) + mod.shape[1:], lambda b, j: (b, 0, 0)),
            _resident(gain.shape),
            _resident(wq.shape),
            _resident(wk.shape),
            _resident(wvt.shape),
        ],
        out_specs=[group_spec, group_spec, pl.BlockSpec((1, d, tm), lambda b, j: (b, 0, j))],
        out_shape=[jax.ShapeDtypeStruct((bsz, d // LANES, s, LANES), BF16),
                   jax.ShapeDtypeStruct((bsz, d // LANES, s, LANES), BF16),
                   jax.ShapeDtypeStruct((bsz, d, s), BF16)],
        scratch_shapes=[pltpu.VMEM((d // LANES, tm, LANES), F32), pltpu.VMEM((tm, d), BF16)],
        compiler_params=_params("arbitrary", "arbitrary"),
        name="qkv_project",
    )(x, mod, gain, wq, wk, wvt)


def _pitch(rows):
    p = -(-rows // SUBLANES)
    return SUBLANES * (p if p % 2 else p + 1)


def _norm_matmul_kernel(x_ref, mod_ref, gain_ref, w_ref, o_ref, hs_ref, slab_ref):
    bsz, tt, d = x_ref.shape
    pitch = slab_ref.shape[1] // bsz
    for b in range(bsz):
        h = _modnorm(x_ref[b], gain_ref[...], mod_ref[b, 0:1], mod_ref[b, 1:2])
        hs_ref[b * tt:(b + 1) * tt, :] = h.astype(BF16)
    u = jnp.dot(hs_ref[...], w_ref[...], preferred_element_type=F32)
    for c in range(d // LANES):
        for b in range(bsz):
            slab_ref[c, b * pitch:b * pitch + tt, :] = u[b * tt:(b + 1) * tt, c * LANES:(c + 1) * LANES]
        for t in range(tt):
            o_ref[t * bsz:(t + 1) * bsz, c * LANES:(c + 1) * LANES] = slab_ref[c, pl.ds(t, bsz, stride=pitch), :]


def _norm_project_time_major(x, mod, gain, w):
    bsz, s, d = x.shape
    n = w.shape[1]
    tt = min(TIME_TILE, s)
    pitch = _pitch(tt)
    return pl.pallas_call(
        _norm_matmul_kernel,
        grid=(s // tt,),
        in_specs=[
            pl.BlockSpec((bsz, tt, d), lambda j: (0, j, 0)),
            _resident(mod.shape),
            _resident(gain.shape),
            _resident(w.shape),
        ],
        out_specs=pl.BlockSpec((tt * bsz, n), lambda j: (j, 0)),
        out_shape=jax.ShapeDtypeStruct((s * bsz, n), F32),
        scratch_shapes=[pltpu.VMEM((bsz * tt, d), BF16), pltpu.VMEM((n // LANES, bsz * pitch, LANES), F32)],
        compiler_params=_params("arbitrary"),
        name="norm_project",
    )(x, mod, gain, w)


def _streams(z_view, col, base, kq, s_view):
    run = jnp.ones((SUBLANES, Q_BLOCK), F32)
    for r in range(KEY_ROWS - 1, -1, -1):
        lo = base + SUBLANES * r
        zn = z_view[lo:lo + SUBLANES, col:col + Q_BLOCK]
        if kq is not None:
            zn = jnp.where(kq > r, zn, MASKED_SCORE)
        e = jnp.exp2(zn)
        a = run * (1.0 / (1.0 + e))
        s_view[lo:lo + SUBLANES, :] = a
        run = run - a
    return run


def _suffix(tot, row):
    p = tot
    for d in (1, 2, 4):
        p = jnp.where(row + d < SUBLANES, p * pltpu.roll(p, SUBLANES - d, 0), p)
    excl = jnp.where(row < SUBLANES - 1, pltpu.roll(p, SUBLANES - 1, 0), 1.0)
    return excl, jnp.broadcast_to(p[0:1, :], p.shape)


def _apply_offsets(s_view, a_view, base, off):
    off2 = jnp.concatenate([off, off], axis=0)
    for r2 in range(KEY_BLOCK // 16):
        lo = base + 16 * r2
        a_view[lo:lo + 16, :] = (s_view[lo:lo + 16, :] * off2).astype(BF16)


def _nt_dot(a, b):
    return lax.dot_general(a, b, (((1,), (1,)), ((), ())), preferred_element_type=F32)


def _attn_kernel(q_ref, k_ref, vt_ref, o_ref, kp_ref, vp_ref, z_ref, s_ref, a_ref, t_ref, oacc_ref, c_ref,
                 flag_ref):
    s_len = q_ref.shape[0]
    n_heads = LANES // HEAD_DIM
    row = lax.broadcasted_iota(jnp.int32, (SUBLANES, Q_BLOCK), 0)
    lane = lax.broadcasted_iota(jnp.int32, (SUBLANES, Q_BLOCK), 1)
    kqs = [lane - KEY_ROWS * row - (b - PAST_BLOCKS) * KEY_BLOCK if b >= PAST_BLOCKS else None
           for b in range(WINDOW_BLOCKS)]
    q_lane = lax.broadcasted_iota(jnp.int32, (Q_BLOCK, LANES), 1)

    n_blocks = s_len // Q_BLOCK

    kp_ref[0:KEY_PAD, :] = jnp.zeros((KEY_PAD, LANES), kp_ref.dtype)
    kp_ref[KEY_PAD:, :] = k_ref[...]
    vp_ref[:, 0:KEY_PAD] = jnp.zeros((LANES, KEY_PAD), vp_ref.dtype)
    vp_ref[:, KEY_PAD:KEY_PAD + s_len] = vt_ref[...]
    if V_WINDOW > WINDOW:
        vp_ref[:, KEY_PAD + s_len:] = jnp.zeros((LANES, V_WINDOW - WINDOW), vp_ref.dtype)

    def stacked_queries(w):
        q = q_ref[pl.ds(pl.multiple_of(w * Q_BLOCK, Q_BLOCK), Q_BLOCK), :]
        zero = jnp.zeros_like(q)
        return jnp.concatenate([jnp.where((q_lane >= h * HEAD_DIM) & (q_lane < (h + 1) * HEAD_DIM), q, zero)
                                for h in range(n_heads)], axis=0)

    def store_out(w):
        o_ref[pl.ds(pl.multiple_of(w * Q_BLOCK, Q_BLOCK), Q_BLOCK), :] = (
            oacc_ref[w].T.astype(o_ref.dtype))

    def scores(p, slot, i):
        w = jnp.minimum(2 * p + i, n_blocks - 1)
        k_win = kp_ref[pl.ds(pl.multiple_of(w * Q_BLOCK, Q_BLOCK), WINDOW), :]
        z_ref[slot, i] = _nt_dot(k_win, stacked_queries(w))

    def weights(p, slot, i):
        w = 2 * p + i
        for h in range(n_heads):
            z_view, s_view, a_view = z_ref.at[slot, i], s_ref.at[slot, i, h], a_ref.at[slot, i, h]
            tots = [_streams(z_view, h * Q_BLOCK, b * KEY_BLOCK, kqs[b], s_view)
                    for b in range(WINDOW_BLOCKS)]
            carry = None
            for b in range(WINDOW_BLOCKS - 1, -1, -1):
                excl, total = _suffix(tots[b], row)
                _apply_offsets(s_view, a_view, b * KEY_BLOCK, excl if carry is None else carry * excl)
                carry = total if carry is None else carry * total
            c_ref[w, h] = carry
        alive = jnp.max(jnp.maximum(c_ref[w, 0], c_ref[w, 1])) > 0.0
        flag_ref[w] = alive.astype(jnp.int32)

    def values(p, slot, i):
        w = jnp.maximum(2 * p + i, 0)
        k0 = pl.multiple_of(w * Q_BLOCK, Q_BLOCK)
        for h in range(n_heads):
            v_win = vp_ref[h * HEAD_DIM:(h + 1) * HEAD_DIM, pl.ds(k0, V_WINDOW)]
            oacc_ref[w, h * HEAD_DIM:(h + 1) * HEAD_DIM, :] = jnp.dot(v_win, a_ref[slot, i, h],
                                                                      preferred_element_type=F32)
        store_out(w)

    for i in range(2):
        scores(0, 0, i)
    a_ref[...] = jnp.zeros(a_ref.shape, a_ref.dtype)

    def two_pairs(m, _):
        p = 2 * m
        for p_w, slot in ((p, 0), (p + 1, 1)):
            for i in range(2):
                scores(p_w + 1, 1 - slot, i)
                weights(p_w, slot, i)
                values(p_w - 1, 1 - slot, i)
        return 0

    lax.fori_loop(0, n_blocks // 4, two_pairs, 0)
    for i in range(2):
        values(n_blocks // 2 - 1, 1, i)

    def older_keys(w):
        q2 = stacked_queries(w)
        n_sub = Q_BLOCK // KEY_BLOCK

        def cond(st):
            return jnp.logical_and(st[0] >= 1, st[1] > 0)

        def body(st):
            j = st[0]
            k0 = pl.multiple_of(j * Q_BLOCK, Q_BLOCK)
            z_ref[0, 0, 0:Q_BLOCK, :] = _nt_dot(kp_ref[pl.ds(k0, Q_BLOCK), :], q2)
            carries = []
            for h in range(n_heads):
                s_view, a_view = s_ref.at[0, 0, h], t_ref.at[h]
                tots = [_streams(z_ref.at[0, 0], h * Q_BLOCK, b * KEY_BLOCK, None, s_view) for b in range(n_sub)]
                carry = c_ref[w, h]
                for b in range(n_sub - 1, -1, -1):
                    excl, total = _suffix(tots[b], row)
                    _apply_offsets(s_view, a_view, b * KEY_BLOCK, carry * excl)
                    carry = carry * total
                v_blk = vp_ref[h * HEAD_DIM:(h + 1) * HEAD_DIM, pl.ds(k0, Q_BLOCK)]
                oacc_ref[w, h * HEAD_DIM:(h + 1) * HEAD_DIM, :] += jnp.dot(
                    v_blk, t_ref[h], preferred_element_type=F32)
                carries.append(carry)
                c_ref[w, h] = carry
            alive = jnp.max(functools.reduce(jnp.maximum, carries)) > 0.0
            return (j - 1, alive.astype(jnp.int32))

        lax.while_loop(cond, body, (w - 1, jnp.int32(1)))

    def finish(w, _):
        @pl.when(flag_ref[w] > 0)
        def _():
            older_keys(w)
            store_out(w)

        return 0

    lax.fori_loop(2, n_blocks, finish, 0)


def _attention(q, k_perm, vt_perm):
    bsz, n_groups, s, _ = q.shape
    n_heads = LANES // HEAD_DIM
    n_blocks = s // Q_BLOCK
    group_spec = pl.BlockSpec((None, None, s, LANES), lambda b, h: (b, h, 0, 0))
    return pl.pallas_call(
        _attn_kernel,
        grid=(bsz, n_groups),
        in_specs=[group_spec, group_spec, pl.BlockSpec((None, LANES, s), lambda b, h: (b, h, 0))],
        out_specs=group_spec,
        out_shape=jax.ShapeDtypeStruct(q.shape, BF16),
        scratch_shapes=[
            pltpu.VMEM((KEY_PAD + s, LANES), BF16),
            pltpu.VMEM((LANES, KEY_PAD + s + V_WINDOW - WINDOW), BF16),
            pltpu.VMEM((2, 2, WINDOW, n_heads * Q_BLOCK), F32),
            pltpu.VMEM((2, 2, n_heads, WINDOW, Q_BLOCK), F32),
            pltpu.VMEM((2, 2, n_heads, V_WINDOW, Q_BLOCK), BF16),
            pltpu.VMEM((n_heads, Q_BLOCK, Q_BLOCK), BF16),
            pltpu.VMEM((n_blocks, LANES, Q_BLOCK), F32),
            pltpu.VMEM((n_blocks, n_heads, SUBLANES, Q_BLOCK), F32),
            pltpu.SMEM((n_blocks,), jnp.int32),
        ],
        compiler_params=_params("arbitrary", "arbitrary"),
        name="stickbreak_attention",
    )(q, k_perm, vt_perm)


def _ssm_kernel(u_ref, bmat_ref, cmat_ref, lam_ref, d_ref, z_ref, xs_ref, st_ref, *, nb, row_chunk):
    rows = u_ref.shape[0]
    half = st_ref.shape[1] // 2

    @pl.when(pl.program_id(1) == 0)
    def _():
        st_ref[...] = jnp.zeros_like(st_ref)

    a_re = jnp.broadcast_to(lam_ref[0, 0:1, :], (nb, half))
    a_im = jnp.broadcast_to(lam_ref[0, 1:2, :], (nb, half))
    x_re, x_im = st_ref[:, :half], st_ref[:, half:]

    for r0 in range(0, rows, row_chunk):
        xs_ref[r0:r0 + row_chunk, :] = jnp.dot(u_ref[r0:r0 + row_chunk, :].astype(BF16), bmat_ref[0],
                                               preferred_element_type=F32)
        for r in range(r0, r0 + row_chunk, nb):
            n_re = a_re * x_re - a_im * x_im + xs_ref[r:r + nb, :half]
            n_im = a_re * x_im + a_im * x_re + xs_ref[r:r + nb, half:]
            xs_ref[r:r + nb, :half] = n_re
            xs_ref[r:r + nb, half:] = n_im
            x_re, x_im = n_re, n_im
        y = jnp.dot(xs_ref[r0:r0 + row_chunk, :].astype(BF16), cmat_ref[0], preferred_element_type=F32)
        z_ref[r0:r0 + row_chunk, :] = jax.nn.gelu(y + d_ref[...] * u_ref[r0:r0 + row_chunk, :])

    st_ref[:, :half] = x_re
    st_ref[:, half:] = x_im


def _ssm(u_tm, bmat, cmat, lam, d_skip, nb):
    rows_total, d = u_tm.shape
    s = rows_total // nb
    tt = min(SSM_TIME_TILE, s)
    rows = tt * nb
    n_state = bmat.shape[-1]
    return pl.pallas_call(
        functools.partial(_ssm_kernel, nb=nb, row_chunk=min(256, rows)),
        grid=(d // SSM_CHUNK, s // tt),
        in_specs=[
            pl.BlockSpec((rows, SSM_CHUNK), lambda c, i: (i, c)),
            pl.BlockSpec((1, SSM_CHUNK, n_state), lambda c, i: (c, 0, 0)),
            pl.BlockSpec((1, n_state, SSM_CHUNK), lambda c, i: (c, 0, 0)),
            pl.BlockSpec((1, 2, n_state // 2), lambda c, i: (c, 0, 0)),
            pl.BlockSpec((1, SSM_CHUNK), lambda c, i: (0, c)),
        ],
        out_specs=pl.BlockSpec((rows, SSM_CHUNK), lambda c, i: (i, c)),
        out_shape=jax.ShapeDtypeStruct(u_tm.shape, F32),
        scratch_shapes=[pltpu.VMEM((rows, n_state), F32), pltpu.VMEM((nb, n_state), F32)],
        compiler_params=_params("arbitrary", "arbitrary"),
        name="s5_scan",
    )(u_tm, bmat, cmat, lam, d_skip)


def _ssm_matrices(a_re, a_im, log_dt, b_re, b_im, c_re, c_im):
    g, p = a_re.shape
    hch = b_re.shape[-1]
    gpc = SSM_CHUNK // hch
    nc = g // gpc
    lam = lax.complex(a_re.astype(F32), a_im.astype(F32))
    dt = jnp.exp(log_dt.astype(F32))[:, None]
    lam_bar = jnp.exp(lam * dt)
    b_bar = ((lam_bar - 1) / lam)[..., None] * lax.complex(b_re.astype(F32), b_im.astype(F32))
    eye = jnp.eye(gpc, dtype=F32)

    def in_blocks(m):
        m = m.reshape(nc, gpc, p, hch).transpose(0, 1, 3, 2)
        return jnp.einsum('cghp,gk->cghkp', m, eye).reshape(nc, gpc * hch, gpc * p)

    def out_blocks(m):
        m = m.reshape(nc, gpc, hch, p).transpose(0, 1, 3, 2)
        return jnp.einsum('cgph,gk->cgpkh', m, eye).reshape(nc, gpc * p, gpc * hch)

    bmat = jnp.concatenate([in_blocks(jnp.real(b_bar)), in_blocks(jnp.imag(b_bar))], axis=-1)
    cmat = jnp.concatenate([out_blocks(c_re.astype(F32)), out_blocks(-c_im.astype(F32))], axis=1)
    lam_ri = jnp.stack([jnp.real(lam_bar).reshape(nc, gpc * p), jnp.imag(lam_bar).reshape(nc, gpc * p)], axis=1)
    return bmat.astype(BF16), cmat.astype(BF16), lam_ri


def _glu_kernel(z_ref, wg_ref, bg_ref, wo_ref, x_ref, mod_ref, o_ref, slab_ref, zb_ref):
    bsz, tt, d = x_ref.shape
    pitch = slab_ref.shape[1] // tt
    for c in range(d // LANES):
        for t in range(tt):
            slab_ref[c, t * pitch:t * pitch + bsz, :] = z_ref[t * bsz:(t + 1) * bsz, c * LANES:(c + 1) * LANES]
        for b in range(bsz):
            zb_ref[b * tt:(b + 1) * tt, c * LANES:(c + 1) * LANES] = slab_ref[c, pl.ds(b, tt, stride=pitch), :]
    z = zb_ref[...]
    a = jnp.dot(z.astype(BF16), wg_ref[...], preferred_element_type=F32) + bg_ref[...]
    g = z * jax.nn.sigmoid(a)
    y = jnp.dot(g.astype(BF16), wo_ref[...], preferred_element_type=F32)
    for b in range(bsz):
        o_ref[b] = x_ref[b] + mod_ref[b, 2:3] * y[b * tt:(b + 1) * tt, :]


def _glu_residual(z_tm, w_glu, b_glu, w_o, x, mod):
    bsz, s, d = x.shape
    tt = min(TIME_TILE, s)
    pitch = _pitch(bsz)
    return pl.pallas_call(
        _glu_kernel,
        grid=(s // tt,),
        in_specs=[
            pl.BlockSpec((tt * bsz, d), lambda j: (j, 0)),
            _resident(w_glu.shape),
            _resident(b_glu.shape),
            _resident(w_o.shape),
            pl.BlockSpec((bsz, tt, d), lambda j: (0, j, 0)),
            _resident(mod.shape),
        ],
        out_specs=pl.BlockSpec((bsz, tt, d), lambda j: (0, j, 0)),
        out_shape=jax.ShapeDtypeStruct(x.shape, F32),
        scratch_shapes=[pltpu.VMEM((d // LANES, tt * pitch, LANES), F32), pltpu.VMEM((bsz * tt, d), F32)],
        compiler_params=_params("arbitrary"),
        name="glu_residual",
    )(z_tm, w_glu, b_glu, w_o, x, mod)


def _ffn_chunks(f):
    assert f % MXU_COLS == 0
    tiles = f // MXU_COLS
    n = min(FFN_CHUNKS, tiles)
    sizes = [(tiles // n + (1 if c < tiles % n else 0)) * MXU_COLS for c in range(n)]
    starts = [sum(sizes[:c]) for c in range(n)]
    return list(zip(starts, sizes))


def _ffn_kernel(x_ref, xh_ref, mod_ref, gain_ref, wup_ref, cw_ref, cb_ref, wdn_ref, *rest, mixer, final):
    if mixer:
        a_ref, ah_ref, wo_ref = rest[:3]
        rest = rest[3:]
    if final:
        fmod_ref, fgain_ref = rest[:2]
        rest = rest[2:]
    o_ref, hs_ref, acc_ref = rest[:3]
    u_refs = rest[3:]
    tm = x_ref.shape[1]
    f = wdn_ref.shape[0]
    m = mod_ref[0]
    gain = gain_ref[...]
    x, x_halo = x_ref[0], xh_ref[0]
    if mixer:
        a_cat = jnp.concatenate([_load_lane_groups(ah_ref), _load_lane_groups(a_ref)], axis=0)
        y_mix = m[2:3] * jnp.dot(a_cat, wo_ref[...], preferred_element_type=F32)
        x, x_halo = x + y_mix[HALO:], x_halo + y_mix[:HALO]
    h_halo = _modnorm(x_halo, gain, m[3:4], m[4:5])
    h_halo = jnp.where(pl.program_id(1) > 0, h_halo, 0.0)
    hs_ref[0:HALO, :] = h_halo.astype(BF16)
    hs_ref[HALO:, :] = _modnorm(x, gain, m[3:4], m[4:5]).astype(BF16)

    def conv(u_ref, c0, tf):
        w = cw_ref[:, c0:c0 + tf]
        return (w[2:3] * u_ref[pl.ds(HALO, tm), :] + w[1:2] * u_ref[pl.ds(HALO - 1, tm), :]
                + w[0:1] * u_ref[pl.ds(HALO - 2, tm), :] + cb_ref[:, c0:c0 + tf])

    for c, (g0, tf) in enumerate(_ffn_chunks(f)):
        ug_ref, uv_ref = u_refs[2 * c], u_refs[2 * c + 1]
        v0 = f + g0
        hs = hs_ref[...]
        ug_ref[...] = jnp.dot(hs, wup_ref[:, g0:g0 + tf], preferred_element_type=F32)
        uv_ref[...] = jnp.dot(hs, wup_ref[:, v0:v0 + tf], preferred_element_type=F32)
        act = (jax.nn.silu(conv(ug_ref, g0, tf)) * conv(uv_ref, v0, tf)).astype(BF16)
        y = jnp.dot(act, wdn_ref[g0:g0 + tf, :], preferred_element_type=F32)
        if c == 0:
            acc_ref[...] = y
        else:
            acc_ref[...] += y
    out = x + m[5:6] * acc_ref[...]
    if final:
        fm = fmod_ref[0]
        out = _modnorm(out, fgain_ref[...], fm[0:1], fm[1:2])
    o_ref[0] = out


def _conv_ffn(x, mod, gain, w_up, conv_w, conv_b, w_down, mixer=None, final=None):
    bsz, s, d = x.shape
    f = w_down.shape[0]
    tm = min(ROW_TILE, s)
    halo_blocks = tm // HALO
    tile_spec = pl.BlockSpec((1, tm, d), lambda b, j: (b, j, 0))
    halo_spec = pl.BlockSpec((1, HALO, d), lambda b, j: (b, jnp.maximum(j * halo_blocks - 1, 0), 0))
    in_specs = [
        tile_spec,
        halo_spec,
        pl.BlockSpec((1,) + mod.shape[1:], lambda b, j: (b, 0, 0)),
        _resident(gain.shape),
        _resident(w_up.shape),
        _resident(conv_w.shape),
        _resident(conv_b.shape),
        _resident(w_down.shape),
    ]
    args = [x, x, mod, gain, w_up, conv_w, conv_b, w_down]
    if mixer is not None:
        n_groups = mixer[0].shape[1]
        in_specs += [pl.BlockSpec((1, n_groups, tm, LANES), lambda b, j: (b, 0, j, 0)),
                     pl.BlockSpec((1, n_groups, HALO, LANES),
                                  lambda b, j: (b, 0, jnp.maximum(j * halo_blocks - 1, 0), 0)),
                     _resident(mixer[1].shape)]
        args += [mixer[0], mixer[0], mixer[1]]
    if final is not None:
        in_specs += [pl.BlockSpec((1,) + final[0].shape[1:], lambda b, j: (b, 0, 0)), _resident(final[1].shape)]
        args += list(final)
    u_scratch = [pltpu.VMEM((tm + HALO, tf), F32) for _, tf in _ffn_chunks(f) for _ in range(2)]
    return pl.pallas_call(
        functools.partial(_ffn_kernel, mixer=mixer is not None, final=final is not None),
        grid=(bsz, s // tm),
        in_specs=in_specs,
        out_specs=pl.BlockSpec((1, tm, d), lambda b, j: (b, j, 0)),
        out_shape=jax.ShapeDtypeStruct(x.shape, F32),
        scratch_shapes=[pltpu.VMEM((tm + HALO, d), BF16), pltpu.VMEM((tm, d), F32)] + u_scratch,
        compiler_params=_params("arbitrary", "arbitrary"),
        name="conv_ffn",
    )(*args)


def kernel(x, c, norm_mix, norm_ffn, w_mod, b_mod, w_qkv, w_o_attn, w_in_ssm, a_re, a_im, log_dt, b_re, b_im, c_re, c_im, d_skip, w_glu, b_glu, w_o_ssm, w_up, conv_w, conv_b, w_down, norm_out, w_fin, b_fin):
    bsz, s, d = x.shape
    depth = w_mod.shape[0]
    assert s % (4 * Q_BLOCK) == 0 and d % LANES == 0 and bsz % SUBLANES == 0

    mods = _mod_project(c, w_mod, b_mod).reshape(depth, bsz, 6, d)
    fin_mod = _mod_project(c, w_fin[None], b_fin[None]).reshape(bsz, 2, d)

    for i in range(depth):
        mod = mods[i]
        j = i // 2
        gain_mix = norm_mix[i][None]
        if i % 2 == 0:
            wq = (w_qkv[j][:, :d] * (-math.log2(math.e) * HEAD_DIM ** -0.5)).astype(BF16)
            wk = w_qkv[j][:, d:2 * d].astype(BF16)
            wvt = w_qkv[j][:, 2 * d:].T.astype(BF16)
            q, k_perm, vt_perm = _qkv_project(x, mod, gain_mix, wq, wk, wvt)
            mixer = (_attention(q, k_perm, vt_perm), w_o_attn[j].astype(BF16))
        else:
            mixer = None
            u_tm = _norm_project_time_major(x, mod, gain_mix, w_in_ssm[j].astype(BF16))
            bmat, cmat, lam = _ssm_matrices(a_re[j], a_im[j], log_dt[j], b_re[j], b_im[j], c_re[j], c_im[j])
            z_tm = _ssm(u_tm, bmat, cmat, lam, d_skip[j][None], nb=bsz)
            x = _glu_residual(z_tm, w_glu[j].astype(BF16), b_glu[j][None], w_o_ssm[j].astype(BF16), x, mod)
        final = (fin_mod, norm_out[None]) if i == depth - 1 else None
        x = _conv_ffn(x, mod, norm_ffn[i][None], w_up[i].astype(BF16), conv_w[i], conv_b[i][None],
                      w_down[i].astype(BF16), mixer=mixer, final=final)
    return x
```

```python
import functools
import math

import jax
import jax.numpy as jnp
from jax import lax
from jax.experimental import pallas as pl
from jax.experimental.pallas import tpu as pltpu

F32 = jnp.float32
BF16 = jnp.bfloat16

HEAD_DIM = 64
SSM_GROUP = 16
STATE = 64
CONV_W = 3
EPS = 1e-6

LANES = 128
SUBLANES = 8
MXU_COLS = 256
Q_BLOCK = LANES
KEY_BLOCK = 64
KEY_ROWS = KEY_BLOCK // SUBLANES
PAST_BLOCKS = 4
WINDOW_BLOCKS = PAST_BLOCKS + Q_BLOCK // KEY_BLOCK
KEY_PAD = PAST_BLOCKS * KEY_BLOCK
WINDOW = WINDOW_BLOCKS * KEY_BLOCK
V_WINDOW = -(-WINDOW // LANES) * LANES
HALO = 16
SSM_CHUNK = LANES
VMEM_LIMIT = 56 * 1024 * 1024
MASKED_SCORE = 1e30

ROW_TILE = 512
FFN_CHUNKS = 2
TIME_TILE = 32
SSM_TIME_TILE = 256
MOD_COL_TILE = 2048


def _params(*sem):
    return pltpu.CompilerParams(dimension_semantics=sem, vmem_limit_bytes=VMEM_LIMIT)


def _resident(shape):
    nd = len(shape)
    return pl.BlockSpec(shape, lambda *_: (0,) * nd, pipeline_mode=pl.Buffered(1))


def _modnorm(x, gain, shift, scale):
    ms = jnp.mean(x * x, axis=-1, keepdims=True)
    y = x * lax.rsqrt(ms + EPS) * gain
    return y * (1.0 + scale) + shift


def _mod_kernel(c_ref, w_ref, b_ref, o_ref):
    ca = jax.nn.silu(c_ref[...])
    o_ref[0] = jnp.dot(ca, w_ref[0], preferred_element_type=F32) + b_ref[0]


def _mod_project(c, w, b):
    nl, d, n = w.shape
    bsz = c.shape[0]
    tn = min(MOD_COL_TILE, n)
    return pl.pallas_call(
        _mod_kernel,
        grid=(nl, n // tn),
        in_specs=[
            pl.BlockSpec((bsz, d), lambda l, j: (0, 0)),
            pl.BlockSpec((1, d, tn), lambda l, j: (l, 0, j)),
            pl.BlockSpec((1, 1, tn), lambda l, j: (l, 0, j)),
        ],
        out_specs=pl.BlockSpec((1, bsz, tn), lambda l, j: (l, 0, j)),
        out_shape=jax.ShapeDtypeStruct((nl, bsz, n), F32),
        compiler_params=_params("arbitrary", "arbitrary"),
        name="adaln_project",
    )(c, w, b.reshape(nl, 1, n))


def _store_lane_groups(o_ref, y):
    for g in range(o_ref.shape[1]):
        o_ref[0, g] = y[:, g * LANES:(g + 1) * LANES].astype(o_ref.dtype)


def _load_lane_groups(a_ref):
    return jnp.concatenate([a_ref[0, g] for g in range(a_ref.shape[1])], axis=1)


def _qkv_kernel(x_ref, mod_ref, gain_ref, wq_ref, wk_ref, wvt_ref, q_ref, k_ref, vt_ref, h_ref, hp_ref):
    tm = x_ref.shape[1]
    m = mod_ref[0]
    h = _modnorm(x_ref[0], gain_ref[...], m[0:1], m[1:2])
    _store_lane_groups(q_ref, jnp.dot(h.astype(BF16), wq_ref[...], preferred_element_type=F32))
    for c in range(h_ref.shape[0]):
        h_ref[c] = h[:, c * LANES:(c + 1) * LANES]
        for blk in range(tm // KEY_BLOCK):
            for r2 in range(KEY_ROWS // 2):
                base = blk * KEY_BLOCK
                lo = h_ref[c, pl.ds(base + 2 * r2, SUBLANES, stride=KEY_ROWS), :]
                hi = h_ref[c, pl.ds(base + 2 * r2 + 1, SUBLANES, stride=KEY_ROWS), :]
                hp_ref[base + 2 * SUBLANES * r2:base + 2 * SUBLANES * (r2 + 1), c * LANES:(c + 1) * LANES] = (
                    jnp.concatenate([lo, hi], axis=0).astype(BF16))
    hp = hp_ref[...]
    _store_lane_groups(k_ref, jnp.dot(hp, wk_ref[...], preferred_element_type=F32))
    vt_ref[0] = lax.dot_general(wvt_ref[...], hp, (((1,), (1,)), ((), ())),
                                preferred_element_type=F32).astype(vt_ref.dtype)


def _qkv_project(x, mod, gain, wq, wk, wvt):
    bsz, s, d = x.shape
    tm = min(ROW_TILE, s)
    row_spec = pl.BlockSpec((1, tm, d), lambda b, j: (b, j, 0))
    group_spec = pl.BlockSpec((1, d // LANES, tm, LANES), lambda b, j: (b, 0, j, 0))
    return pl.pallas_call(
        _qkv_kernel,
        grid=(bsz, s // tm),
        in_specs=[
            row_spec,
            pl.BlockSpec((1,) + mod.shape[1:], lambda b, j: (b, 0, 0)),
            _resident(gain.shape),
            _resident(wq.shape),
            _resident(wk.shape),
            _resident(wvt.shape),
        ],
        out_specs=[group_spec, group_spec, pl.BlockSpec((1, d, tm), lambda b, j: (b, 0, j))],
        out_shape=[jax.ShapeDtypeStruct((bsz, d // LANES, s, LANES), BF16),
                   jax.ShapeDtypeStruct((bsz, d // LANES, s, LANES), BF16),
                   jax.ShapeDtypeStruct((bsz, d, s), BF16)],
        scratch_shapes=[pltpu.VMEM((d // LANES, tm, LANES), F32), pltpu.VMEM((tm, d), BF16)],
        compiler_params=_params("arbitrary", "arbitrary"),
        name="qkv_project",
    )(x, mod, gain, wq, wk, wvt)


def _pitch(rows):
    p = -(-rows // SUBLANES)
    return SUBLANES * (p if p % 2 else p + 1)


def _norm_matmul_kernel(x_ref, mod_ref, gain_ref, w_ref, o_ref, hs_ref, slab_ref):
    bsz, tt, d = x_ref.shape
    pitch = slab_ref.shape[1] // bsz
    for b in range(bsz):
        h = _modnorm(x_ref[b], gain_ref[...], mod_ref[b, 0:1], mod_ref[b, 1:2])
        hs_ref[b * tt:(b + 1) * tt, :] = h.astype(BF16)
    u = jnp.dot(hs_ref[...], w_ref[...], preferred_element_type=F32)
    for c in range(d // LANES):
        for b in range(bsz):
            slab_ref[c, b * pitch:b * pitch + tt, :] = u[b * tt:(b + 1) * tt, c * LANES:(c + 1) * LANES]
        for t in range(tt):
            o_ref[t * bsz:(t + 1) * bsz, c * LANES:(c + 1) * LANES] = slab_ref[c, pl.ds(t, bsz, stride=pitch), :]


def _norm_project_time_major(x, mod, gain, w):
    bsz, s, d = x.shape
    n = w.shape[1]
    tt = min(TIME_TILE, s)
    pitch = _pitch(tt)
    return pl.pallas_call(
        _norm_matmul_kernel,
        grid=(s // tt,),
        in_specs=[
            pl.BlockSpec((bsz, tt, d), lambda j: (0, j, 0)),
            _resident(mod.shape),
            _resident(gain.shape),
            _resident(w.shape),
        ],
        out_specs=pl.BlockSpec((tt * bsz, n), lambda j: (j, 0)),
        out_shape=jax.ShapeDtypeStruct((s * bsz, n), F32),
        scratch_shapes=[pltpu.VMEM((bsz * tt, d), BF16), pltpu.VMEM((n // LANES, bsz * pitch, LANES), F32)],
        compiler_params=_params("arbitrary"),
        name="norm_project",
    )(x, mod, gain, w)


def _streams(z_view, col, base, kq, s_view):
    run = jnp.ones((SUBLANES, Q_BLOCK), F32)
    for r in range(KEY_ROWS - 1, -1, -1):
        lo = base + SUBLANES * r
        zn = z_view[lo:lo + SUBLANES, col:col + Q_BLOCK]
        if kq is not None:
            zn = jnp.where(kq > r, zn, MASKED_SCORE)
        e = jnp.exp2(zn)
        a = run * (1.0 / (1.0 + e))
        s_view[lo:lo + SUBLANES, :] = a
        run = run - a
    return run


def _suffix(tot, row):
    p = tot
    for d in (1, 2, 4):
        p = jnp.where(row + d < SUBLANES, p * pltpu.roll(p, SUBLANES - d, 0), p)
    excl = jnp.where(row < SUBLANES - 1, pltpu.roll(p, SUBLANES - 1, 0), 1.0)
    return excl, jnp.broadcast_to(p[0:1, :], p.shape)


def _apply_offsets(s_view, a_view, base, off):
    off2 = jnp.concatenate([off, off], axis=0)
    for r2 in range(KEY_BLOCK // 16):
        lo = base + 16 * r2
        a_view[lo:lo + 16, :] = (s_view[lo:lo + 16, :] * off2).astype(BF16)


def _nt_dot(a, b):
    return lax.dot_general(a, b, (((1,), (1,)), ((), ())), preferred_element_type=F32)


def _attn_kernel(q_ref, k_ref, vt_ref, o_ref, kp_ref, vp_ref, z_ref, s_ref, a_ref, t_ref, oacc_ref, c_ref,
                 flag_ref):
    s_len = q_ref.shape[0]
    n_heads = LANES // HEAD_DIM
    row = lax.broadcasted_iota(jnp.int32, (SUBLANES, Q_BLOCK), 0)
    lane = lax.broadcasted_iota(jnp.int32, (SUBLANES, Q_BLOCK), 1)
    kqs = [lane - KEY_ROWS * row - (b - PAST_BLOCKS) * KEY_BLOCK if b >= PAST_BLOCKS else None
           for b in range(WINDOW_BLOCKS)]
    q_lane = lax.broadcasted_iota(jnp.int32, (Q_BLOCK, LANES), 1)

    n_blocks = s_len // Q_BLOCK

    kp_ref[0:KEY_PAD, :] = jnp.zeros((KEY_PAD, LANES), kp_ref.dtype)
    kp_ref[KEY_PAD:, :] = k_ref[...]
    vp_ref[:, 0:KEY_PAD] = jnp.zeros((LANES, KEY_PAD), vp_ref.dtype)
    vp_ref[:, KEY_PAD:KEY_PAD + s_len] = vt_ref[...]
    if V_WINDOW > WINDOW:
        vp_ref[:, KEY_PAD + s_len:] = jnp.zeros((LANES, V_WINDOW - WINDOW), vp_ref.dtype)

    def stacked_queries(w):
        q = q_ref[pl.ds(pl.multiple_of(w * Q_BLOCK, Q_BLOCK), Q_BLOCK), :]
        zero = jnp.zeros_like(q)
        return jnp.concatenate([jnp.where((q_lane >= h * HEAD_DIM) & (q_lane < (h + 1) * HEAD_DIM), q, zero)
                                for h in range(n_heads)], axis=0)

    def store_out(w):
        o_ref[pl.ds(pl.multiple_of(w * Q_BLOCK, Q_BLOCK), Q_BLOCK), :] = (
            oacc_ref[w].T.astype(o_ref.dtype))

    def scores(p, slot, i):
        w = jnp.minimum(2 * p + i, n_blocks - 1)
        k_win = kp_ref[pl.ds(pl.multiple_of(w * Q_BLOCK, Q_BLOCK), WINDOW), :]
        z_ref[slot, i] = _nt_dot(k_win, stacked_queries(w))

    def weights(p, slot, i):
        w = 2 * p + i
        for h in range(n_heads):
            z_view, s_view, a_view = z_ref.at[slot, i], s_ref.at[slot, i, h], a_ref.at[slot, i, h]
            tots = [_streams(z_view, h * Q_BLOCK, b * KEY_BLOCK, kqs[b], s_view)
                    for b in range(WINDOW_BLOCKS)]
            carry = None
            for b in range(WINDOW_BLOCKS - 1, -1, -1):
                excl, total = _suffix(tots[b], row)
                _apply_offsets(s_view, a_view, b * KEY_BLOCK, excl if carry is None else carry * excl)
                carry = total if carry is None else carry * total
            c_ref[w, h] = carry
        alive = jnp.max(jnp.maximum(c_ref[w, 0], c_ref[w, 1])) > 0.0
        flag_ref[w] = alive.astype(jnp.int32)

    def values(p, slot, i):
        w = jnp.maximum(2 * p + i, 0)
        k0 = pl.multiple_of(w * Q_BLOCK, Q_BLOCK)
        for h in range(n_heads):
            v_win = vp_ref[h * HEAD_DIM:(h + 1) * HEAD_DIM, pl.ds(k0, V_WINDOW)]
            oacc_ref[w, h * HEAD_DIM:(h + 1) * HEAD_DIM, :] = jnp.dot(v_win, a_ref[slot, i, h],
                                                                      preferred_element_type=F32)
        store_out(w)

    for i in range(2):
        scores(0, 0, i)
    a_ref[...] = jnp.zeros(a_ref.shape, a_ref.dtype)

    def two_pairs(m, _):
        p = 2 * m
        for p_w, slot in ((p, 0), (p + 1, 1)):
            for i in range(2):
                scores(p_w + 1, 1 - slot, i)
                weights(p_w, slot, i)
                values(p_w - 1, 1 - slot, i)
        return 0

    lax.fori_loop(0, n_blocks // 4, two_pairs, 0)
    for i in range(2):
        values(n_blocks // 2 - 1, 1, i)

    def older_keys(w):
        q2 = stacked_queries(w)
        n_sub = Q_BLOCK // KEY_BLOCK

        def cond(st):
            return jnp.logical_and(st[0] >= 1, st[1] > 0)

        def body(st):
            j = st[0]
            k0 = pl.multiple_of(j * Q_BLOCK, Q_BLOCK)
            z_ref[0, 0, 0:Q_BLOCK, :] = _nt_dot(kp_ref[pl.ds(k0, Q_BLOCK), :], q2)
            carries = []
            for h in range(n_heads):
                s_view, a_view = s_ref.at[0, 0, h], t_ref.at[h]
                tots = [_streams(z_ref.at[0, 0], h * Q_BLOCK, b * KEY_BLOCK, None, s_view) for b in range(n_sub)]
                carry = c_ref[w, h]
                for b in range(n_sub - 1, -1, -1):
                    excl, total = _suffix(tots[b], row)
                    _apply_offsets(s_view, a_view, b * KEY_BLOCK, carry * excl)
                    carry = carry * total
                v_blk = vp_ref[h * HEAD_DIM:(h + 1) * HEAD_DIM, pl.ds(k0, Q_BLOCK)]
                oacc_ref[w, h * HEAD_DIM:(h + 1) * HEAD_DIM, :] += jnp.dot(
                    v_blk, t_ref[h], preferred_element_type=F32)
                carries.append(carry)
                c_ref[w, h] = carry
            alive = jnp.max(functools.reduce(jnp.maximum, carries)) > 0.0
            return (j - 1, alive.astype(jnp.int32))

        lax.while_loop(cond, body, (w - 1, jnp.int32(1)))

    def finish(w, _):
        @pl.when(flag_ref[w] > 0)
        def _():
            older_keys(w)
            store_out(w)

        return 0

    lax.fori_loop(2, n_blocks, finish, 0)


def _attention(q, k_perm, vt_perm):
    bsz, n_groups, s, _ = q.shape
    n_heads = LANES // HEAD_DIM
    n_blocks = s // Q_BLOCK
    group_spec = pl.BlockSpec((None, None, s, LANES), lambda b, h: (b, h, 0, 0))
    return pl.pallas_call(
        _attn_kernel,
        grid=(bsz, n_groups),
        in_specs=[group_spec, group_spec, pl.BlockSpec((None, LANES, s), lambda b, h: (b, h, 0))],
        out_specs=group_spec,
        out_shape=jax.ShapeDtypeStruct(q.shape, BF16),
        scratch_shapes=[
            pltpu.VMEM((KEY_PAD + s, LANES), BF16),
            pltpu.VMEM((LANES, KEY_PAD + s + V_WINDOW - WINDOW), BF16),
            pltpu.VMEM((2, 2, WINDOW, n_heads * Q_BLOCK), F32),
            pltpu.VMEM((2, 2, n_heads, WINDOW, Q_BLOCK), F32),
            pltpu.VMEM((2, 2, n_heads, V_WINDOW, Q_BLOCK), BF16),
            pltpu.VMEM((n_heads, Q_BLOCK, Q_BLOCK), BF16),
            pltpu.VMEM((n_blocks, LANES, Q_BLOCK), F32),
            pltpu.VMEM((n_blocks, n_heads, SUBLANES, Q_BLOCK), F32),
            pltpu.SMEM((n_blocks,), jnp.int32),
        ],
        compiler_params=_params("arbitrary", "arbitrary"),
        name="stickbreak_attention",
    )(q, k_perm, vt_perm)


def _ssm_kernel(u_ref, bmat_ref, cmat_ref, lam_ref, d_ref, z_ref, xs_ref, st_ref, *, nb, row_chunk):
    rows = u_ref.shape[0]
    half = st_ref.shape[1] // 2

    @pl.when(pl.program_id(1) == 0)
    def _():
        st_ref[...] = jnp.zeros_like(st_ref)

    a_re = jnp.broadcast_to(lam_ref[0, 0:1, :], (nb, half))
    a_im = jnp.broadcast_to(lam_ref[0, 1:2, :], (nb, half))
    x_re, x_im = st_ref[:, :half], st_ref[:, half:]

    for r0 in range(0, rows, row_chunk):
        xs_ref[r0:r0 + row_chunk, :] = jnp.dot(u_ref[r0:r0 + row_chunk, :].astype(BF16), bmat_ref[0],
                                               preferred_element_type=F32)
        for r in range(r0, r0 + row_chunk, nb):
            n_re = a_re * x_re - a_im * x_im + xs_ref[r:r + nb, :half]
            n_im = a_re * x_im + a_im * x_re + xs_ref[r:r + nb, half:]
            xs_ref[r:r + nb, :half] = n_re
            xs_ref[r:r + nb, half:] = n_im
            x_re, x_im = n_re, n_im
        y = jnp.dot(xs_ref[r0:r0 + row_chunk, :].astype(BF16), cmat_ref[0], preferred_element_type=F32)
        z_ref[r0:r0 + row_chunk, :] = jax.nn.gelu(y + d_ref[...] * u_ref[r0:r0 + row_chunk, :])

    st_ref[:, :half] = x_re
    st_ref[:, half:] = x_im


def _ssm(u_tm, bmat, cmat, lam, d_skip, nb):
    rows_total, d = u_tm.shape
    s = rows_total // nb
    tt = min(SSM_TIME_TILE, s)
    rows = tt * nb
    n_state = bmat.shape[-1]
    return pl.pallas_call(
        functools.partial(_ssm_kernel, nb=nb, row_chunk=min(512, rows)),
        grid=(d // SSM_CHUNK, s // tt),
        in_specs=[
            pl.BlockSpec((rows, SSM_CHUNK), lambda c, i: (i, c)),
            pl.BlockSpec((1, SSM_CHUNK, n_state), lambda c, i: (c, 0, 0)),
            pl.BlockSpec((1, n_state, SSM_CHUNK), lambda c, i: (c, 0, 0)),
            pl.BlockSpec((1, 2, n_state // 2), lambda c, i: (c, 0, 0)),
            pl.BlockSpec((1, SSM_CHUNK), lambda c, i: (0, c)),
        ],
        out_specs=pl.BlockSpec((rows, SSM_CHUNK), lambda c, i: (i, c)),
        out_shape=jax.ShapeDtypeStruct(u_tm.shape, F32),
        scratch_shapes=[pltpu.VMEM((rows, n_state), F32), pltpu.VMEM((nb, n_state), F32)],
        compiler_params=_params("arbitrary", "arbitrary"),
        name="s5_scan",
    )(u_tm, bmat, cmat, lam, d_skip)


def _ssm_matrices(a_re, a_im, log_dt, b_re, b_im, c_re, c_im):
    g, p = a_re.shape
    hch = b_re.shape[-1]
    gpc = SSM_CHUNK // hch
    nc = g // gpc
    lam = lax.complex(a_re.astype(F32), a_im.astype(F32))
    dt = jnp.exp(log_dt.astype(F32))[:, None]
    lam_bar = jnp.exp(lam * dt)
    b_bar = ((lam_bar - 1) / lam)[..., None] * lax.complex(b_re.astype(F32), b_im.astype(F32))
    eye = jnp.eye(gpc, dtype=F32)

    def in_blocks(m):
        m = m.reshape(nc, gpc, p, hch).transpose(0, 1, 3, 2)
        return jnp.einsum('cghp,gk->cghkp', m, eye).reshape(nc, gpc * hch, gpc * p)

    def out_blocks(m):
        m = m.reshape(nc, gpc, hch, p).transpose(0, 1, 3, 2)
        return jnp.einsum('cgph,gk->cgpkh', m, eye).reshape(nc, gpc * p, gpc * hch)

    bmat = jnp.concatenate([in_blocks(jnp.real(b_bar)), in_blocks(jnp.imag(b_bar))], axis=-1)
    cmat = jnp.concatenate([out_blocks(c_re.astype(F32)), out_blocks(-c_im.astype(F32))], axis=1)
    lam_ri = jnp.stack([jnp.real(lam_bar).reshape(nc, gpc * p), jnp.imag(lam_bar).reshape(nc, gpc * p)], axis=1)
    return bmat.astype(BF16), cmat.astype(BF16), lam_ri


def _glu_kernel(z_ref, wg_ref, bg_ref, wo_ref, x_ref, mod_ref, o_ref, slab_ref, zb_ref):
    bsz, tt, d = x_ref.shape
    pitch = slab_ref.shape[1] // tt
    for c in range(d // LANES):
        for t in range(tt):
            slab_ref[c, t * pitch:t * pitch + bsz, :] = z_ref[t * bsz:(t + 1) * bsz, c * LANES:(c + 1) * LANES]
        for b in range(bsz):
            zb_ref[b * tt:(b + 1) * tt, c * LANES:(c + 1) * LANES] = slab_ref[c, pl.ds(b, tt, stride=pitch), :]
    z = zb_ref[...]
    a = jnp.dot(z.astype(BF16), wg_ref[...], preferred_element_type=F32) + bg_ref[...]
    g = z * jax.nn.sigmoid(a)
    y = jnp.dot(g.astype(BF16), wo_ref[...], preferred_element_type=F32)
    for b in range(bsz):
        o_ref[b] = x_ref[b] + mod_ref[b, 2:3] * y[b * tt:(b + 1) * tt, :]


def _glu_residual(z_tm, w_glu, b_glu, w_o, x, mod):
    bsz, s, d = x.shape
    tt = min(TIME_TILE, s)
    pitch = _pitch(bsz)
    return pl.pallas_call(
        _glu_kernel,
        grid=(s // tt,),
        in_specs=[
            pl.BlockSpec((tt * bsz, d), lambda j: (j, 0)),
            _resident(w_glu.shape),
            _resident(b_glu.shape),
            _resident(w_o.shape),
            pl.BlockSpec((bsz, tt, d), lambda j: (0, j, 0)),
            _resident(mod.shape),
        ],
        out_specs=pl.BlockSpec((bsz, tt, d), lambda j: (0, j, 0)),
        out_shape=jax.ShapeDtypeStruct(x.shape, F32),
        scratch_shapes=[pltpu.VMEM((d // LANES, tt * pitch, LANES), F32), pltpu.VMEM((bsz * tt, d), F32)],
        compiler_params=_params("arbitrary"),
        name="glu_residual",
    )(z_tm, w_glu, b_glu, w_o, x, mod)


def _ffn_chunks(f):
    assert f % MXU_COLS == 0
    tiles = f // MXU_COLS
    n = min(FFN_CHUNKS, tiles)
    sizes = [(tiles // n + (1 if c < tiles % n else 0)) * MXU_COLS for c in range(n)]
    starts = [sum(sizes[:c]) for c in range(n)]
    return list(zip(starts, sizes))


def _ffn_kernel(x_ref, xh_ref, mod_ref, gain_ref, wup_ref, cw_ref, cb_ref, wdn_ref, *rest, mixer, final):
    if mixer:
        a_ref, ah_ref, wo_ref = rest[:3]
        rest = rest[3:]
    if final:
        fmod_ref, fgain_ref = rest[:2]
        rest = rest[2:]
    o_ref, hs_ref, acc_ref = rest[:3]
    u_refs = rest[3:]
    tm = x_ref.shape[1]
    f = wdn_ref.shape[0]
    m = mod_ref[0]
    gain = gain_ref[...]
    x, x_halo = x_ref[0], xh_ref[0]
    if mixer:
        a_cat = jnp.concatenate([_load_lane_groups(ah_ref), _load_lane_groups(a_ref)], axis=0)
        y_mix = m[2:3] * jnp.dot(a_cat, wo_ref[...], preferred_element_type=F32)
        x, x_halo = x + y_mix[HALO:], x_halo + y_mix[:HALO]
    h_halo = _modnorm(x_halo, gain, m[3:4], m[4:5])
    h_halo = jnp.where(pl.program_id(1) > 0, h_halo, 0.0)
    hs_ref[0:HALO, :] = h_halo.astype(BF16)
    hs_ref[HALO:, :] = _modnorm(x, gain, m[3:4], m[4:5]).astype(BF16)

    def conv(u_ref, c0, tf):
        w = cw_ref[:, c0:c0 + tf]
        return (w[2:3] * u_ref[pl.ds(HALO, tm), :] + w[1:2] * u_ref[pl.ds(HALO - 1, tm), :]
                + w[0:1] * u_ref[pl.ds(HALO - 2, tm), :] + cb_ref[:, c0:c0 + tf])

    for c, (g0, tf) in enumerate(_ffn_chunks(f)):
        ug_ref, uv_ref = u_refs[2 * c], u_refs[2 * c + 1]
        v0 = f + g0
        hs = hs_ref[...]
        ug_ref[...] = jnp.dot(hs, wup_ref[:, g0:g0 + tf], preferred_element_type=F32)
        uv_ref[...] = jnp.dot(hs, wup_ref[:, v0:v0 + tf], preferred_element_type=F32)
        act = (jax.nn.silu(conv(ug_ref, g0, tf)) * conv(uv_ref, v0, tf)).astype(BF16)
        y = jnp.dot(act, wdn_ref[g0:g0 + tf, :], preferred_element_type=F32)
        if c == 0:
            acc_ref[...] = y
        else:
            acc_ref[...] += y
    out = x + m[5:6] * acc_ref[...]
    if final:
        fm = fmod_ref[0]
        out = _modnorm(out, fgain_ref[...], fm[0:1], fm[1:2])
    o_ref[0] = out


def _conv_ffn(x, mod, gain, w_up, conv_w, conv_b, w_down, mixer=None, final=None):
    bsz, s, d = x.shape
    f = w_down.shape[0]
    tm = min(ROW_TILE, s)
    halo_blocks = tm // HALO
    tile_spec = pl.BlockSpec((1, tm, d), lambda b, j: (b, j, 0))
    halo_spec = pl.BlockSpec((1, HALO, d), lambda b, j: (b, jnp.maximum(j * halo_blocks - 1, 0), 0))
    in_specs = [
        tile_spec,
        halo_spec,
        pl.BlockSpec((1,) + mod.shape[1:], lambda b, j: (b, 0, 0)),
        _resident(gain.shape),
        _resident(w_up.shape),
        _resident(conv_w.shape),
        _resident(conv_b.shape),
        _resident(w_down.shape),
    ]
    args = [x, x, mod, gain, w_up, conv_w, conv_b, w_down]
    if mixer is not None:
        n_groups = mixer[0].shape[1]
        in_specs += [pl.BlockSpec((1, n_groups, tm, LANES), lambda b, j: (b, 0, j, 0)),
                     pl.BlockSpec((1, n_groups, HALO, LANES),
                                  lambda b, j: (b, 0, jnp.maximum(j * halo_blocks - 1, 0), 0)),
                     _resident(mixer[1].shape)]
        args += [mixer[0], mixer[0], mixer[1]]
    if final is not None:
        in_specs += [pl.BlockSpec((1,) + final[0].shape[1:], lambda b, j: (b, 0, 0)), _resident(final[1].shape)]
        args += list(final)
    u_scratch = [pltpu.VMEM((tm + HALO, tf), F32) for _, tf in _ffn_chunks(f) for _ in range(2)]
    return pl.pallas_call(
        functools.partial(_ffn_kernel, mixer=mixer is not None, final=final is not None),
        grid=(bsz, s // tm),
        in_specs=in_specs,
        out_specs=pl.BlockSpec((1, tm, d), lambda b, j: (b, j, 0)),
        out_shape=jax.ShapeDtypeStruct(x.shape, F32),
        scratch_shapes=[pltpu.VMEM((tm + HALO, d), BF16), pltpu.VMEM((tm, d), F32)] + u_scratch,
        compiler_params=_params("arbitrary", "arbitrary"),
        name="conv_ffn",
    )(*args)


def kernel(x, c, norm_mix, norm_ffn, w_mod, b_mod, w_qkv, w_o_attn, w_in_ssm, a_re, a_im, log_dt, b_re, b_im, c_re, c_im, d_skip, w_glu, b_glu, w_o_ssm, w_up, conv_w, conv_b, w_down, norm_out, w_fin, b_fin):
    bsz, s, d = x.shape
    depth = w_mod.shape[0]
    assert s % (4 * Q_BLOCK) == 0 and d % LANES == 0 and bsz % SUBLANES == 0

    mods = _mod_project(c, w_mod, b_mod).reshape(depth, bsz, 6, d)
    fin_mod = _mod_project(c, w_fin[None], b_fin[None]).reshape(bsz, 2, d)

    for i in range(depth):
        mod = mods[i]
        j = i // 2
        gain_mix = norm_mix[i][None]
        if i % 2 == 0:
            wq = (w_qkv[j][:, :d] * (-math.log2(math.e) * HEAD_DIM ** -0.5)).astype(BF16)
            wk = w_qkv[j][:, d:2 * d].astype(BF16)
            wvt = w_qkv[j][:, 2 * d:].T.astype(BF16)
            q, k_perm, vt_perm = _qkv_project(x, mod, gain_mix, wq, wk, wvt)
            mixer = (_attention(q, k_perm, vt_perm), w_o_attn[j].astype(BF16))
        else:
            mixer = None
            u_tm = _norm_project_time_major(x, mod, gain_mix, w_in_ssm[j].astype(BF16))
            bmat, cmat, lam = _ssm_matrices(a_re[j], a_im[j], log_dt[j], b_re[j], b_im[j], c_re[j], c_im[j])
            z_tm = _ssm(u_tm, bmat, cmat, lam, d_skip[j][None], nb=bsz)
            x = _glu_residual(z_tm, w_glu[j].astype(BF16), b_glu[j][None], w_o_ssm[j].astype(BF16), x, mod)
        final = (fin_mod, norm_out[None]) if i == depth - 1 else None
        x = _conv_ffn(x, mod, norm_ffn[i][None], w_up[i].astype(BF16), conv_w[i], conv_b[i][None],
                      w_down[i].astype(BF16), mixer=mixer, final=final)
    return x
```

```python
import functools
import math

import jax
import jax.numpy as jnp
from jax import lax
from jax.experimental import pallas as pl
from jax.experimental.pallas import tpu as pltpu

F32 = jnp.float32
BF16 = jnp.bfloat16

HEAD_DIM = 64
EPS = 1e-6

LANES = 128
SUBLANES = 8
MXU_COLS = 256
Q_BLOCK = LANES
KEY_BLOCK = 64
KEY_ROWS = KEY_BLOCK // SUBLANES
PAST_BLOCKS = 4
WINDOW_BLOCKS = PAST_BLOCKS + Q_BLOCK // KEY_BLOCK
KEY_PAD = PAST_BLOCKS * KEY_BLOCK
WINDOW = WINDOW_BLOCKS * KEY_BLOCK
V_WINDOW = -(-WINDOW // LANES) * LANES
HALO = 16
SSM_CHUNK = LANES
VMEM_LIMIT = 56 * 1024 * 1024
MASKED_SCORE = 1e30

ROW_TILE = 512
FFN_CHUNKS = 2
TIME_TILE = 32
SSM_TIME_TILE = 256
MOD_COL_TILE = 2048


def _params(*sem):
    return pltpu.CompilerParams(dimension_semantics=sem, vmem_limit_bytes=VMEM_LIMIT)


def _resident(shape):
    nd = len(shape)
    return pl.BlockSpec(shape, lambda *_: (0,) * nd, pipeline_mode=pl.Buffered(1))


def _modnorm(x, gain, shift, scale):
    ms = jnp.mean(x * x, axis=-1, keepdims=True)
    y = x * lax.rsqrt(ms + EPS) * gain
    return y * (1.0 + scale) + shift


def _mod_kernel(c_ref, w_ref, b_ref, o_ref):
    ca = jax.nn.silu(c_ref[...])
    o_ref[0] = jnp.dot(ca, w_ref[0], preferred_element_type=F32) + b_ref[0]


def _mod_project(c, w, b):
    nl, d, n = w.shape
    bsz = c.shape[0]
    tn = min(MOD_COL_TILE, n)
    return pl.pallas_call(
        _mod_kernel,
        grid=(nl, n // tn),
        in_specs=[
            pl.BlockSpec((bsz, d), lambda l, j: (0, 0)),
            pl.BlockSpec((1, d, tn), lambda l, j: (l, 0, j)),
            pl.BlockSpec((1, 1, tn), lambda l, j: (l, 0, j)),
        ],
        out_specs=pl.BlockSpec((1, bsz, tn), lambda l, j: (l, 0, j)),
        out_shape=jax.ShapeDtypeStruct((nl, bsz, n), F32),
        compiler_params=_params("arbitrary", "arbitrary"),
        name="adaln_project",
    )(c, w, b.reshape(nl, 1, n))


def _store_lane_groups(o_ref, y):
    for g in range(o_ref.shape[1]):
        o_ref[0, g] = y[:, g * LANES:(g + 1) * LANES].astype(o_ref.dtype)


def _load_lane_groups(a_ref):
    return jnp.concatenate([a_ref[0, g] for g in range(a_ref.shape[1])], axis=1)


def _qkv_kernel(x_ref, mod_ref, gain_ref, wq_ref, wk_ref, wvt_ref, q_ref, k_ref, vt_ref, h_ref, hp_ref):
    tm = x_ref.shape[1]
    m = mod_ref[0]
    h = _modnorm(x_ref[0], gain_ref[...], m[0:1], m[1:2])
    _store_lane_groups(q_ref, jnp.dot(h.astype(BF16), wq_ref[...], preferred_element_type=F32))
    for c in range(h_ref.shape[0]):
        h_ref[c] = h[:, c * LANES:(c + 1) * LANES]
        for blk in range(tm // KEY_BLOCK):
            for r2 in range(KEY_ROWS // 2):
                base = blk * KEY_BLOCK
                lo = h_ref[c, pl.ds(base + 2 * r2, SUBLANES, stride=KEY_ROWS), :]
                hi = h_ref[c, pl.ds(base + 2 * r2 + 1, SUBLANES, stride=KEY_ROWS), :]
                hp_ref[base + 2 * SUBLANES * r2:base + 2 * SUBLANES * (r2 + 1), c * LANES:(c + 1) * LANES] = (
                    jnp.concatenate([lo, hi], axis=0).astype(BF16))
    hp = hp_ref[...]
    _store_lane_groups(k_ref, jnp.dot(hp, wk_ref[...], preferred_element_type=F32))
    vt_ref[0] = lax.dot_general(wvt_ref[...], hp, (((1,), (1,)), ((), ())),
                                preferred_element_type=F32).astype(vt_ref.dtype)


def _qkv_project(x, mod, gain, wq, wk, wvt):
    bsz, s, d = x.shape
    tm = min(ROW_TILE, s)
    row_spec = pl.BlockSpec((1, tm, d), lambda b, j: (b, j, 0))
    group_spec = pl.BlockSpec((1, d // LANES, tm, LANES), lambda b, j: (b, 0, j, 0))
    return pl.pallas_call(
        _qkv_kernel,
        grid=(bsz, s // tm),
        in_specs=[
            row_spec,
            pl.BlockSpec((1,) + mod.shape[1:], lambda b, j: (b, 0, 0)),
            _resident(gain.shape),
            _resident(wq.shape),
            _resident(wk.shape),
            _resident(wvt.shape),
        ],
        out_specs=[group_spec, group_spec, pl.BlockSpec((1, d, tm), lambda b, j: (b, 0, j))],
        out_shape=[jax.ShapeDtypeStruct((bsz, d // LANES, s, LANES), BF16),
                   jax.ShapeDtypeStruct((bsz, d // LANES, s, LANES), BF16),
                   jax.ShapeDtypeStruct((bsz, d, s), BF16)],
        scratch_shapes=[pltpu.VMEM((d // LANES, tm, LANES), F32), pltpu.VMEM((tm, d), BF16)],
        compiler_params=_params("arbitrary", "arbitrary"),
        name="qkv_project",
    )(x, mod, gain, wq, wk, wvt)


def _pitch(rows):
    p = -(-rows // SUBLANES)
    return SUBLANES * (p if p % 2 else p + 1)


def _norm_matmul_kernel(x_ref, mod_ref, gain_ref, w_ref, o_ref, hs_ref, slab_ref):
    bsz, tt, d = x_ref.shape
    pitch = slab_ref.shape[1] // bsz
    for b in range(bsz):
        h = _modnorm(x_ref[b], gain_ref[...], mod_ref[b, 0:1], mod_ref[b, 1:2])
        hs_ref[b * tt:(b + 1) * tt, :] = h.astype(BF16)
    u = jnp.dot(hs_ref[...], w_ref[...], preferred_element_type=F32)
    for c in range(d // LANES):
        for b in range(bsz):
            slab_ref[c, b * pitch:b * pitch + tt, :] = u[b * tt:(b + 1) * tt, c * LANES:(c + 1) * LANES]
        for t in range(tt):
            o_ref[t * bsz:(t + 1) * bsz, c * LANES:(c + 1) * LANES] = slab_ref[c, pl.ds(t, bsz, stride=pitch), :]


def _norm_project_time_major(x, mod, gain, w):
    bsz, s, d = x.shape
    n = w.shape[1]
    tt = min(TIME_TILE, s)
    pitch = _pitch(tt)
    return pl.pallas_call(
        _norm_matmul_kernel,
        grid=(s // tt,),
        in_specs=[
            pl.BlockSpec((bsz, tt, d), lambda j: (0, j, 0)),
            _resident(mod.shape),
            _resident(gain.shape),
            _resident(w.shape),
        ],
        out_specs=pl.BlockSpec((tt * bsz, n), lambda j: (j, 0)),
        out_shape=jax.ShapeDtypeStruct((s * bsz, n), F32),
        scratch_shapes=[pltpu.VMEM((bsz * tt, d), BF16), pltpu.VMEM((n // LANES, bsz * pitch, LANES), F32)],
        compiler_params=_params("arbitrary"),
        name="norm_project",
    )(x, mod, gain, w)


def _streams(z_view, col, base, kq, s_view):
    run = jnp.ones((SUBLANES, Q_BLOCK), F32)
    for r in range(KEY_ROWS - 1, -1, -1):
        lo = base + SUBLANES * r
        zn = z_view[lo:lo + SUBLANES, col:col + Q_BLOCK]
        if kq is not None:
            zn = jnp.where(kq > r, zn, MASKED_SCORE)
        e = jnp.exp2(zn)
        a = run * (1.0 / (1.0 + e))
        s_view[lo:lo + SUBLANES, :] = a
        run = run - a
    return run


def _suffix(tot, row):
    p = tot
    for d in (1, 2, 4):
        p = jnp.where(row + d < SUBLANES, p * pltpu.roll(p, SUBLANES - d, 0), p)
    excl = jnp.where(row < SUBLANES - 1, pltpu.roll(p, SUBLANES - 1, 0), 1.0)
    return excl, jnp.broadcast_to(p[0:1, :], p.shape)


def _apply_offsets(s_view, a_view, base, off):
    off2 = jnp.concatenate([off, off], axis=0)
    for r2 in range(KEY_BLOCK // 16):
        lo = base + 16 * r2
        a_view[lo:lo + 16, :] = (s_view[lo:lo + 16, :] * off2).astype(BF16)


def _nt_dot(a, b):
    return lax.dot_general(a, b, (((1,), (1,)), ((), ())), preferred_element_type=F32)


def _attn_kernel(q_ref, k_ref, vt_ref, o_ref, kp_ref, vp_ref, z_ref, s_ref, a_ref, t_ref, oacc_ref, c_ref,
                 flag_ref):
    s_len = q_ref.shape[0]
    n_heads = LANES // HEAD_DIM
    row = lax.broadcasted_iota(jnp.int32, (SUBLANES, Q_BLOCK), 0)
    lane = lax.broadcasted_iota(jnp.int32, (SUBLANES, Q_BLOCK), 1)
    kqs = [lane - KEY_ROWS * row - (b - PAST_BLOCKS) * KEY_BLOCK if b >= PAST_BLOCKS else None
           for b in range(WINDOW_BLOCKS)]
    q_lane = lax.broadcasted_iota(jnp.int32, (Q_BLOCK, LANES), 1)

    n_blocks = s_len // Q_BLOCK

    kp_ref[0:KEY_PAD, :] = jnp.zeros((KEY_PAD, LANES), kp_ref.dtype)
    kp_ref[KEY_PAD:, :] = k_ref[...]
    vp_ref[:, 0:KEY_PAD] = jnp.zeros((LANES, KEY_PAD), vp_ref.dtype)
    vp_ref[:, KEY_PAD:KEY_PAD + s_len] = vt_ref[...]
    if V_WINDOW > WINDOW:
        vp_ref[:, KEY_PAD + s_len:] = jnp.zeros((LANES, V_WINDOW - WINDOW), vp_ref.dtype)

    def stacked_queries(w):
        q = q_ref[pl.ds(pl.multiple_of(w * Q_BLOCK, Q_BLOCK), Q_BLOCK), :]
        zero = jnp.zeros_like(q)
        return jnp.concatenate([jnp.where((q_lane >= h * HEAD_DIM) & (q_lane < (h + 1) * HEAD_DIM), q, zero)
                                for h in range(n_heads)], axis=0)

    def store_out(w):
        o_ref[pl.ds(pl.multiple_of(w * Q_BLOCK, Q_BLOCK), Q_BLOCK), :] = (
            oacc_ref[w].T.astype(o_ref.dtype))

    def scores(p, slot, i):
        w = jnp.minimum(2 * p + i, n_blocks - 1)
        k_win = kp_ref[pl.ds(pl.multiple_of(w * Q_BLOCK, Q_BLOCK), WINDOW), :]
        z_ref[slot, i] = _nt_dot(k_win, stacked_queries(w))

    def weights(p, slot, i):
        w = 2 * p + i
        for h in range(n_heads):
            z_view, s_view, a_view = z_ref.at[slot, i], s_ref.at[slot, i, h], a_ref.at[slot, i, h]
            tots = [_streams(z_view, h * Q_BLOCK, b * KEY_BLOCK, kqs[b], s_view)
                    for b in range(WINDOW_BLOCKS)]
            carry = None
            for b in range(WINDOW_BLOCKS - 1, -1, -1):
                excl, total = _suffix(tots[b], row)
                _apply_offsets(s_view, a_view, b * KEY_BLOCK, excl if carry is None else carry * excl)
                carry = total if carry is None else carry * total
            c_ref[w, h] = carry
        alive = jnp.max(jnp.maximum(c_ref[w, 0], c_ref[w, 1])) > 0.0
        flag_ref[w] = alive.astype(jnp.int32)

    def values(p, slot, i):
        w = jnp.maximum(2 * p + i, 0)
        k0 = pl.multiple_of(w * Q_BLOCK, Q_BLOCK)
        for h in range(n_heads):
            v_win = vp_ref[h * HEAD_DIM:(h + 1) * HEAD_DIM, pl.ds(k0, V_WINDOW)]
            oacc_ref[w, h * HEAD_DIM:(h + 1) * HEAD_DIM, :] = jnp.dot(v_win, a_ref[slot, i, h],
                                                                      preferred_element_type=F32)
        store_out(w)

    for i in range(2):
        scores(0, 0, i)
    a_ref[...] = jnp.zeros(a_ref.shape, a_ref.dtype)

    def two_pairs(m, _):
        p = 2 * m
        for p_w, slot in ((p, 0), (p + 1, 1)):
            for i in range(2):
                scores(p_w + 1, 1 - slot, i)
                weights(p_w, slot, i)
                values(p_w - 1, 1 - slot, i)
        return 0

    lax.fori_loop(0, n_blocks // 4, two_pairs, 0)
    for i in range(2):
        values(n_blocks // 2 - 1, 1, i)

    def older_keys(w):
        q2 = stacked_queries(w)
        n_sub = Q_BLOCK // KEY_BLOCK

        def cond(st):
            return jnp.logical_and(st[0] >= 1, st[1] > 0)

        def body(st):
            j = st[0]
            k0 = pl.multiple_of(j * Q_BLOCK, Q_BLOCK)
            z_ref[0, 0, 0:Q_BLOCK, :] = _nt_dot(kp_ref[pl.ds(k0, Q_BLOCK), :], q2)
            carries = []
            for h in range(n_heads):
                s_view, a_view = s_ref.at[0, 0, h], t_ref.at[h]
                tots = [_streams(z_ref.at[0, 0], h * Q_BLOCK, b * KEY_BLOCK, None, s_view) for b in range(n_sub)]
                carry = c_ref[w, h]
                for b in range(n_sub - 1, -1, -1):
                    excl, total = _suffix(tots[b], row)
                    _apply_offsets(s_view, a_view, b * KEY_BLOCK, carry * excl)
                    carry = carry * total
                v_blk = vp_ref[h * HEAD_DIM:(h + 1) * HEAD_DIM, pl.ds(k0, Q_BLOCK)]
                oacc_ref[w, h * HEAD_DIM:(h + 1) * HEAD_DIM, :] += jnp.dot(
                    v_blk, t_ref[h], preferred_element_type=F32)
                carries.append(carry)
                c_ref[w, h] = carry
            alive = jnp.max(functools.reduce(jnp.maximum, carries)) > 0.0
            return (j - 1, alive.astype(jnp.int32))

        lax.while_loop(cond, body, (w - 1, jnp.int32(1)))

    def finish(w, _):
        @pl.when(flag_ref[w] > 0)
        def _():
            older_keys(w)
            store_out(w)

        return 0

    lax.fori_loop(2, n_blocks, finish, 0)


def _attention(q, k_perm, vt_perm):
    bsz, n_groups, s, _ = q.shape
    n_heads = LANES // HEAD_DIM
    n_blocks = s // Q_BLOCK
    group_spec = pl.BlockSpec((None, None, s, LANES), lambda b, h: (b, h, 0, 0))
    return pl.pallas_call(
        _attn_kernel,
        grid=(bsz, n_groups),
        in_specs=[group_spec, group_spec, pl.BlockSpec((None, LANES, s), lambda b, h: (b, h, 0))],
        out_specs=group_spec,
        out_shape=jax.ShapeDtypeStruct(q.shape, BF16),
        scratch_shapes=[
            pltpu.VMEM((KEY_PAD + s, LANES), BF16),
            pltpu.VMEM((LANES, KEY_PAD + s + V_WINDOW - WINDOW), BF16),
            pltpu.VMEM((2, 2, WINDOW, n_heads * Q_BLOCK), F32),
            pltpu.VMEM((2, 2, n_heads, WINDOW, Q_BLOCK), F32),
            pltpu.VMEM((2, 2, n_heads, V_WINDOW, Q_BLOCK), BF16),
            pltpu.VMEM((n_heads, Q_BLOCK, Q_BLOCK), BF16),
            pltpu.VMEM((n_blocks, LANES, Q_BLOCK), F32),
            pltpu.VMEM((n_blocks, n_heads, SUBLANES, Q_BLOCK), F32),
            pltpu.SMEM((n_blocks,), jnp.int32),
        ],
        compiler_params=_params("arbitrary", "arbitrary"),
        name="stickbreak_attention",
    )(q, k_perm, vt_perm)


def _ssm_kernel(u_ref, bmat_ref, cmat_ref, lam_ref, d_ref, z_ref, xs_ref, st_ref, *, nb, row_chunk):
    rows = u_ref.shape[0]
    half = st_ref.shape[1] // 2

    @pl.when(pl.program_id(1) == 0)
    def _():
        st_ref[...] = jnp.zeros_like(st_ref)

    a_re = jnp.broadcast_to(lam_ref[0, 0:1, :], (nb, half))
    a_im = jnp.broadcast_to(lam_ref[0, 1:2, :], (nb, half))
    x_re, x_im = st_ref[:, :half], st_ref[:, half:]

    for r0 in range(0, rows, row_chunk):
        xs_ref[r0:r0 + row_chunk, :] = jnp.dot(u_ref[r0:r0 + row_chunk, :].astype(BF16), bmat_ref[0],
                                               preferred_element_type=F32)
        for r in range(r0, r0 + row_chunk, nb):
            n_re = a_re * x_re - a_im * x_im + xs_ref[r:r + nb, :half]
            n_im = a_re * x_im + a_im * x_re + xs_ref[r:r + nb, half:]
            xs_ref[r:r + nb, :half] = n_re
            xs_ref[r:r + nb, half:] = n_im
            x_re, x_im = n_re, n_im
        y = jnp.dot(xs_ref[r0:r0 + row_chunk, :].astype(BF16), cmat_ref[0], preferred_element_type=F32)
        z_ref[r0:r0 + row_chunk, :] = jax.nn.gelu(y + d_ref[...] * u_ref[r0:r0 + row_chunk, :])

    st_ref[:, :half] = x_re
    st_ref[:, half:] = x_im


def _ssm(u_tm, bmat, cmat, lam, d_skip, nb):
    rows_total, d = u_tm.shape
    s = rows_total // nb
    tt = min(SSM_TIME_TILE, s)
    rows = tt * nb
    n_state = bmat.shape[-1]
    return pl.pallas_call(
        functools.partial(_ssm_kernel, nb=nb, row_chunk=min(512, rows)),
        grid=(d // SSM_CHUNK, s // tt),
        in_specs=[
            pl.BlockSpec((rows, SSM_CHUNK), lambda c, i: (i, c)),
            pl.BlockSpec((1, SSM_CHUNK, n_state), lambda c, i: (c, 0, 0)),
            pl.BlockSpec((1, n_state, SSM_CHUNK), lambda c, i: (c, 0, 0)),
            pl.BlockSpec((1, 2, n_state // 2), lambda c, i: (c, 0, 0)),
            pl.BlockSpec((1, SSM_CHUNK), lambda c, i: (0, c)),
        ],
        out_specs=pl.BlockSpec((rows, SSM_CHUNK), lambda c, i: (i, c)),
        out_shape=jax.ShapeDtypeStruct(u_tm.shape, F32),
        scratch_shapes=[pltpu.VMEM((rows, n_state), F32), pltpu.VMEM((nb, n_state), F32)],
        compiler_params=_params("arbitrary", "arbitrary"),
        name="s5_scan",
    )(u_tm, bmat, cmat, lam, d_skip)


def _ssm_matrices(a_re, a_im, log_dt, b_re, b_im, c_re, c_im):
    g, p = a_re.shape
    hch = b_re.shape[-1]
    gpc = SSM_CHUNK // hch
    nc = g // gpc
    lam = lax.complex(a_re.astype(F32), a_im.astype(F32))
    dt = jnp.exp(log_dt.astype(F32))[:, None]
    lam_bar = jnp.exp(lam * dt)
    b_bar = ((lam_bar - 1) / lam)[..., None] * lax.complex(b_re.astype(F32), b_im.astype(F32))
    eye = jnp.eye(gpc, dtype=F32)

    def in_blocks(m):
        m = m.reshape(nc, gpc, p, hch).transpose(0, 1, 3, 2)
        return jnp.einsum('cghp,gk->cghkp', m, eye).reshape(nc, gpc * hch, gpc * p)

    def out_blocks(m):
        m = m.reshape(nc, gpc, hch, p).transpose(0, 1, 3, 2)
        return jnp.einsum('cgph,gk->cgpkh', m, eye).reshape(nc, gpc * p, gpc * hch)

    bmat = jnp.concatenate([in_blocks(jnp.real(b_bar)), in_blocks(jnp.imag(b_bar))], axis=-1)
    cmat = jnp.concatenate([out_blocks(c_re.astype(F32)), out_blocks(-c_im.astype(F32))], axis=1)
    lam_ri = jnp.stack([jnp.real(lam_bar).reshape(nc, gpc * p), jnp.imag(lam_bar).reshape(nc, gpc * p)], axis=1)
    return bmat.astype(BF16), cmat.astype(BF16), lam_ri


def _glu_kernel(z_ref, wg_ref, bg_ref, wo_ref, x_ref, mod_ref, o_ref, slab_ref, zb_ref):
    bsz, tt, d = x_ref.shape
    pitch = slab_ref.shape[1] // tt
    for c in range(d // LANES):
        for t in range(tt):
            slab_ref[c, t * pitch:t * pitch + bsz, :] = z_ref[t * bsz:(t + 1) * bsz, c * LANES:(c + 1) * LANES]
        for b in range(bsz):
            zb_ref[b * tt:(b + 1) * tt, c * LANES:(c + 1) * LANES] = slab_ref[c, pl.ds(b, tt, stride=pitch), :]
    z = zb_ref[...]
    a = jnp.dot(z.astype(BF16), wg_ref[...], preferred_element_type=F32) + bg_ref[...]
    g = z * jax.nn.sigmoid(a)
    y = jnp.dot(g.astype(BF16), wo_ref[...], preferred_element_type=F32)
    for b in range(bsz):
        o_ref[b] = x_ref[b] + mod_ref[b, 2:3] * y[b * tt:(b + 1) * tt, :]


def _glu_residual(z_tm, w_glu, b_glu, w_o, x, mod):
    bsz, s, d = x.shape
    tt = min(TIME_TILE, s)
    pitch = _pitch(bsz)
    return pl.pallas_call(
        _glu_kernel,
        grid=(s // tt,),
        in_specs=[
            pl.BlockSpec((tt * bsz, d), lambda j: (j, 0)),
            _resident(w_glu.shape),
            _resident(b_glu.shape),
            _resident(w_o.shape),
            pl.BlockSpec((bsz, tt, d), lambda j: (0, j, 0)),
            _resident(mod.shape),
        ],
        out_specs=pl.BlockSpec((bsz, tt, d), lambda j: (0, j, 0)),
        out_shape=jax.ShapeDtypeStruct(x.shape, F32),
        scratch_shapes=[pltpu.VMEM((d // LANES, tt * pitch, LANES), F32), pltpu.VMEM((bsz * tt, d), F32)],
        compiler_params=_params("arbitrary"),
        name="glu_residual",
    )(z_tm, w_glu, b_glu, w_o, x, mod)


def _ffn_chunks(f):
    assert f % MXU_COLS == 0
    tiles = f // MXU_COLS
    n = min(FFN_CHUNKS, tiles)
    sizes = [(tiles // n + (1 if c < tiles % n else 0)) * MXU_COLS for c in range(n)]
    starts = [sum(sizes[:c]) for c in range(n)]
    return list(zip(starts, sizes))


def _ffn_kernel(x_ref, xh_ref, mod_ref, gain_ref, wup_ref, cw_ref, cb_ref, wdn_ref, *rest, mixer, final):
    if mixer:
        a_ref, ah_ref, wo_ref = rest[:3]
        rest = rest[3:]
    if final:
        fmod_ref, fgain_ref = rest[:2]
        rest = rest[2:]
    o_ref, hs_ref, acc_ref = rest[:3]
    u_refs = rest[3:]
    tm = x_ref.shape[1]
    f = wdn_ref.shape[0]
    m = mod_ref[0]
    gain = gain_ref[...]
    x, x_halo = x_ref[0], xh_ref[0]
    if mixer:
        a_cat = jnp.concatenate([_load_lane_groups(ah_ref), _load_lane_groups(a_ref)], axis=0)
        y_mix = m[2:3] * jnp.dot(a_cat, wo_ref[...], preferred_element_type=F32)
        x, x_halo = x + y_mix[HALO:], x_halo + y_mix[:HALO]
    h_halo = _modnorm(x_halo, gain, m[3:4], m[4:5])
    h_halo = jnp.where(pl.program_id(1) > 0, h_halo, 0.0)
    hs_ref[0:HALO, :] = h_halo.astype(BF16)
    hs_ref[HALO:, :] = _modnorm(x, gain, m[3:4], m[4:5]).astype(BF16)

    def conv(u_ref, c0, tf):
        w = cw_ref[:, c0:c0 + tf]
        return (w[2:3] * u_ref[pl.ds(HALO, tm), :] + w[1:2] * u_ref[pl.ds(HALO - 1, tm), :]
                + w[0:1] * u_ref[pl.ds(HALO - 2, tm), :] + cb_ref[:, c0:c0 + tf])

    for c, (g0, tf) in enumerate(_ffn_chunks(f)):
        ug_ref, uv_ref = u_refs[2 * c], u_refs[2 * c + 1]
        v0 = f + g0
        hs = hs_ref[...]
        ug_ref[...] = jnp.dot(hs, wup_ref[:, g0:g0 + tf], preferred_element_type=F32)
        uv_ref[...] = jnp.dot(hs, wup_ref[:, v0:v0 + tf], preferred_element_type=F32)
        act = (jax.nn.silu(conv(ug_ref, g0, tf)) * conv(uv_ref, v0, tf)).astype(BF16)
        y = jnp.dot(act, wdn_ref[g0:g0 + tf, :], preferred_element_type=F32)
        if c == 0:
            acc_ref[...] = y
        else:
            acc_ref[...] += y
    out = x + m[5:6] * acc_ref[...]
    if final:
        fm = fmod_ref[0]
        out = _modnorm(out, fgain_ref[...], fm[0:1], fm[1:2])
    o_ref[0] = out


def _conv_ffn(x, mod, gain, w_up, conv_w, conv_b, w_down, mixer=None, final=None):
    bsz, s, d = x.shape
    f = w_down.shape[0]
    tm = min(ROW_TILE, s)
    halo_blocks = tm // HALO
    tile_spec = pl.BlockSpec((1, tm, d), lambda b, j: (b, j, 0))
    halo_spec = pl.BlockSpec((1, HALO, d), lambda b, j: (b, jnp.maximum(j * halo_blocks - 1, 0), 0))
    in_specs = [
        tile_spec,
        halo_spec,
        pl.BlockSpec((1,) + mod.shape[1:], lambda b, j: (b, 0, 0)),
        _resident(gain.shape),
        _resident(w_up.shape),
        _resident(conv_w.shape),
        _resident(conv_b.shape),
        _resident(w_down.shape),
    ]
    args = [x, x, mod, gain, w_up, conv_w, conv_b, w_down]
    if mixer is not None:
        n_groups = mixer[0].shape[1]
        in_specs += [pl.BlockSpec((1, n_groups, tm, LANES), lambda b, j: (b, 0, j, 0)),
                     pl.BlockSpec((1, n_groups, HALO, LANES),
                                  lambda b, j: (b, 0, jnp.maximum(j * halo_blocks - 1, 0), 0)),
                     _resident(mixer[1].shape)]
        args += [mixer[0], mixer[0], mixer[1]]
    if final is not None:
        in_specs += [pl.BlockSpec((1,) + final[0].shape[1:], lambda b, j: (b, 0, 0)), _resident(final[1].shape)]
        args += list(final)
    u_scratch = [pltpu.VMEM((tm + HALO, tf), F32) for _, tf in _ffn_chunks(f) for _ in range(2)]
    return pl.pallas_call(
        functools.partial(_ffn_kernel, mixer=mixer is not None, final=final is not None),
        grid=(bsz, s // tm),
        in_specs=in_specs,
        out_specs=pl.BlockSpec((1, tm, d), lambda b, j: (b, j, 0)),
        out_shape=jax.ShapeDtypeStruct(x.shape, F32),
        scratch_shapes=[pltpu.VMEM((tm + HALO, d), BF16), pltpu.VMEM((tm, d), F32)] + u_scratch,
        compiler_params=_params("arbitrary", "arbitrary"),
        name="conv_ffn",
    )(*args)


def kernel(x, c, norm_mix, norm_ffn, w_mod, b_mod, w_qkv, w_o_attn, w_in_ssm, a_re, a_im, log_dt, b_re, b_im, c_re, c_im, d_skip, w_glu, b_glu, w_o_ssm, w_up, conv_w, conv_b, w_down, norm_out, w_fin, b_fin):
    bsz, s, d = x.shape
    depth = w_mod.shape[0]
    assert s % (4 * Q_BLOCK) == 0 and d % LANES == 0 and bsz % SUBLANES == 0

    mods = _mod_project(c, w_mod, b_mod).reshape(depth, bsz, 6, d)
    fin_mod = _mod_project(c, w_fin[None], b_fin[None]).reshape(bsz, 2, d)

    for i in range(depth):
        mod = mods[i]
        j = i // 2
        gain_mix = norm_mix[i][None]
        if i % 2 == 0:
            wq = (w_qkv[j][:, :d] * (-math.log2(math.e) * HEAD_DIM ** -0.5)).astype(BF16)
            wk = w_qkv[j][:, d:2 * d].astype(BF16)
            wvt = w_qkv[j][:, 2 * d:].T.astype(BF16)
            q, k_perm, vt_perm = _qkv_project(x, mod, gain_mix, wq, wk, wvt)
            mixer = (_attention(q, k_perm, vt_perm), w_o_attn[j].astype(BF16))
        else:
            mixer = None
            u_tm = _norm_project_time_major(x, mod, gain_mix, w_in_ssm[j].astype(BF16))
            bmat, cmat, lam = _ssm_matrices(a_re[j], a_im[j], log_dt[j], b_re[j], b_im[j], c_re[j], c_im[j])
            z_tm = _ssm(u_tm, bmat, cmat, lam, d_skip[j][None], nb=bsz)
            x = _glu_residual(z_tm, w_glu[j].astype(BF16), b_glu[j][None], w_o_ssm[j].astype(BF16), x, mod)
        final = (fin_mod, norm_out[None]) if i == depth - 1 else None
        x = _conv_ffn(x, mod, norm_ffn[i][None], w_up[i].astype(BF16), conv_w[i], conv_b[i][None],
                      w_down[i].astype(BF16), mixer=mixer, final=final)
    return x
```

```python
import functools
import math

import jax
import jax.numpy as jnp
from jax import lax
from jax.experimental import pallas as pl
from jax.experimental.pallas import tpu as pltpu

F32 = jnp.float32
BF16 = jnp.bfloat16

HEAD_DIM = 64
EPS = 1e-6

LANES = 128
SUBLANES = 8
MXU_COLS = 256
Q_BLOCK = LANES
KEY_BLOCK = 64
KEY_ROWS = KEY_BLOCK // SUBLANES
PAST_BLOCKS = 4
WINDOW_BLOCKS = PAST_BLOCKS + Q_BLOCK // KEY_BLOCK
KEY_PAD = PAST_BLOCKS * KEY_BLOCK
WINDOW = WINDOW_BLOCKS * KEY_BLOCK
V_WINDOW = -(-WINDOW // LANES) * LANES
HALO = 16
SSM_CHUNK = LANES
VMEM_LIMIT = 56 * 1024 * 1024
MASKED_SCORE = 1e30

ROW_TILE = 512
FFN_CHUNKS = 2
TIME_TILE = 32
SSM_TIME_TILE = 256
MOD_COL_TILE = 2048


def _params(*sem):
    return pltpu.CompilerParams(dimension_semantics=sem, vmem_limit_bytes=VMEM_LIMIT)


def _resident(shape):
    nd = len(shape)
    return pl.BlockSpec(shape, lambda *_: (0,) * nd, pipeline_mode=pl.Buffered(1))


def _modnorm(x, gain, shift, scale):
    ms = jnp.mean(x * x, axis=-1, keepdims=True)
    y = x * lax.rsqrt(ms + EPS) * gain
    return y * (1.0 + scale) + shift


def _mod_kernel(c_ref, w_ref, b_ref, o_ref):
    ca = jax.nn.silu(c_ref[...])
    o_ref[0] = jnp.dot(ca, w_ref[0], preferred_element_type=F32) + b_ref[0]


def _mod_project(c, w, b):
    nl, d, n = w.shape
    bsz = c.shape[0]
    tn = min(MOD_COL_TILE, n)
    return pl.pallas_call(
        _mod_kernel,
        grid=(nl, n // tn),
        in_specs=[
            pl.BlockSpec((bsz, d), lambda l, j: (0, 0)),
            pl.BlockSpec((1, d, tn), lambda l, j: (l, 0, j)),
            pl.BlockSpec((1, 1, tn), lambda l, j: (l, 0, j)),
        ],
        out_specs=pl.BlockSpec((1, bsz, tn), lambda l, j: (l, 0, j)),
        out_shape=jax.ShapeDtypeStruct((nl, bsz, n), F32),
        compiler_params=_params("arbitrary", "arbitrary"),
        name="adaln_project",
    )(c, w, b.reshape(nl, 1, n))


def _store_lane_groups(o_ref, y):
    for g in range(o_ref.shape[1]):
        o_ref[0, g] = y[:, g * LANES:(g + 1) * LANES].astype(o_ref.dtype)


def _load_lane_groups(a_ref):
    return jnp.concatenate([a_ref[0, g] for g in range(a_ref.shape[1])], axis=1)


def _qkv_kernel(x_ref, mod_ref, gain_ref, wq_ref, wk_ref, wvt_ref, q_ref, k_ref, vt_ref, h_ref, hp_ref):
    tm = x_ref.shape[1]
    m = mod_ref[0]
    h = _modnorm(x_ref[0], gain_ref[...], m[0:1], m[1:2])
    _store_lane_groups(q_ref, jnp.dot(h.astype(BF16), wq_ref[...], preferred_element_type=F32))
    for c in range(h_ref.shape[0]):
        h_ref[c] = h[:, c * LANES:(c + 1) * LANES]
        for blk in range(tm // KEY_BLOCK):
            for r2 in range(KEY_ROWS // 2):
                base = blk * KEY_BLOCK
                lo = h_ref[c, pl.ds(base + 2 * r2, SUBLANES, stride=KEY_ROWS), :]
                hi = h_ref[c, pl.ds(base + 2 * r2 + 1, SUBLANES, stride=KEY_ROWS), :]
                hp_ref[base + 2 * SUBLANES * r2:base + 2 * SUBLANES * (r2 + 1), c * LANES:(c + 1) * LANES] = (
                    jnp.concatenate([lo, hi], axis=0).astype(BF16))
    hp = hp_ref[...]
    _store_lane_groups(k_ref, jnp.dot(hp, wk_ref[...], preferred_element_type=F32))
    vt_ref[0] = lax.dot_general(wvt_ref[...], hp, (((1,), (1,)), ((), ())),
                                preferred_element_type=F32).astype(vt_ref.dtype)


def _qkv_project(x, mod, gain, wq, wk, wvt):
    bsz, s, d = x.shape
    tm = min(ROW_TILE, s)
    row_spec = pl.BlockSpec((1, tm, d), lambda b, j: (b, j, 0))
    group_spec = pl.BlockSpec((1, d // LANES, tm, LANES), lambda b, j: (b, 0, j, 0))
    return pl.pallas_call(
        _qkv_kernel,
        grid=(bsz, s // tm),
        in_specs=[
            row_spec,
            pl.BlockSpec((1,) + mod.shape[1:], lambda b, j: (b, 0, 0)),
            _resident(gain.shape),
            _resident(wq.shape),
            _resident(wk.shape),
            _resident(wvt.shape),
        ],
        out_specs=[group_spec, group_spec, pl.BlockSpec((1, d, tm), lambda b, j: (b, 0, j))],
        out_shape=[jax.ShapeDtypeStruct((bsz, d // LANES, s, LANES), BF16),
                   jax.ShapeDtypeStruct((bsz, d // LANES, s, LANES), BF16),
                   jax.ShapeDtypeStruct((bsz, d, s), BF16)],
        scratch_shapes=[pltpu.VMEM((d // LANES, tm, LANES), F32), pltpu.VMEM((tm, d), BF16)],
        compiler_params=_params("arbitrary", "arbitrary"),
        name="qkv_project",
    )(x, mod, gain, wq, wk, wvt)


def _pitch(rows):
    p = -(-rows // SUBLANES)
    return SUBLANES * (p if p % 2 else p + 1)


def _norm_matmul_kernel(x_ref, mod_ref, gain_ref, w_ref, o_ref, hs_ref, slab_ref):
    bsz, tt, d = x_ref.shape
    pitch = slab_ref.shape[1] // bsz
    for b in range(bsz):
        h = _modnorm(x_ref[b], gain_ref[...], mod_ref[b, 0:1], mod_ref[b, 1:2])
        hs_ref[b * tt:(b + 1) * tt, :] = h.astype(BF16)
    u = jnp.dot(hs_ref[...], w_ref[...], preferred_element_type=F32)
    for c in range(d // LANES):
        for b in range(bsz):
            slab_ref[c, b * pitch:b * pitch + tt, :] = u[b * tt:(b + 1) * tt, c * LANES:(c + 1) * LANES]
        for t in range(tt):
            o_ref[t * bsz:(t + 1) * bsz, c * LANES:(c + 1) * LANES] = (
                slab_ref[c, pl.ds(t, bsz, stride=pitch), :].astype(o_ref.dtype))


def _norm_project_time_major(x, mod, gain, w):
    bsz, s, d = x.shape
    n = w.shape[1]
    tt = min(TIME_TILE, s)
    pitch = _pitch(tt)
    return pl.pallas_call(
        _norm_matmul_kernel,
        grid=(s // tt,),
        in_specs=[
            pl.BlockSpec((bsz, tt, d), lambda j: (0, j, 0)),
            _resident(mod.shape),
            _resident(gain.shape),
            _resident(w.shape),
        ],
        out_specs=pl.BlockSpec((tt * bsz, n), lambda j: (j, 0)),
        out_shape=jax.ShapeDtypeStruct((s * bsz, n), BF16),
        scratch_shapes=[pltpu.VMEM((bsz * tt, d), BF16), pltpu.VMEM((n // LANES, bsz * pitch, LANES), F32)],
        compiler_params=_params("arbitrary"),
        name="norm_project",
    )(x, mod, gain, w)


def _streams(z_view, col, base, kq, s_view):
    run = jnp.ones((SUBLANES, Q_BLOCK), F32)
    for r in range(KEY_ROWS - 1, -1, -1):
        lo = base + SUBLANES * r
        zn = z_view[lo:lo + SUBLANES, col:col + Q_BLOCK]
        if kq is not None:
            zn = jnp.where(kq > r, zn, MASKED_SCORE)
        e = jnp.exp2(zn)
        a = run * (1.0 / (1.0 + e))
        s_view[lo:lo + SUBLANES, :] = a
        run = run - a
    return run


def _suffix(tot, row):
    p = tot
    for d in (1, 2, 4):
        p = jnp.where(row + d < SUBLANES, p * pltpu.roll(p, SUBLANES - d, 0), p)
    excl = jnp.where(row < SUBLANES - 1, pltpu.roll(p, SUBLANES - 1, 0), 1.0)
    return excl, jnp.broadcast_to(p[0:1, :], p.shape)


def _apply_offsets(s_view, a_view, base, off):
    off2 = jnp.concatenate([off, off], axis=0)
    for r2 in range(KEY_BLOCK // 16):
        lo = base + 16 * r2
        a_view[lo:lo + 16, :] = (s_view[lo:lo + 16, :] * off2).astype(BF16)


def _nt_dot(a, b):
    return lax.dot_general(a, b, (((1,), (1,)), ((), ())), preferred_element_type=F32)


def _attn_kernel(q_ref, k_ref, vt_ref, o_ref, kp_ref, vp_ref, z_ref, s_ref, a_ref, t_ref, oacc_ref, c_ref,
                 flag_ref):
    s_len = q_ref.shape[0]
    n_heads = LANES // HEAD_DIM
    row = lax.broadcasted_iota(jnp.int32, (SUBLANES, Q_BLOCK), 0)
    lane = lax.broadcasted_iota(jnp.int32, (SUBLANES, Q_BLOCK), 1)
    kqs = [lane - KEY_ROWS * row - (b - PAST_BLOCKS) * KEY_BLOCK if b >= PAST_BLOCKS else None
           for b in range(WINDOW_BLOCKS)]
    q_lane = lax.broadcasted_iota(jnp.int32, (Q_BLOCK, LANES), 1)

    n_blocks = s_len // Q_BLOCK

    kp_ref[0:KEY_PAD, :] = jnp.zeros((KEY_PAD, LANES), kp_ref.dtype)
    kp_ref[KEY_PAD:, :] = k_ref[...]
    vp_ref[:, 0:KEY_PAD] = jnp.zeros((LANES, KEY_PAD), vp_ref.dtype)
    vp_ref[:, KEY_PAD:KEY_PAD + s_len] = vt_ref[...]
    if V_WINDOW > WINDOW:
        vp_ref[:, KEY_PAD + s_len:] = jnp.zeros((LANES, V_WINDOW - WINDOW), vp_ref.dtype)

    def stacked_queries(w):
        q = q_ref[pl.ds(pl.multiple_of(w * Q_BLOCK, Q_BLOCK), Q_BLOCK), :]
        zero = jnp.zeros_like(q)
        return jnp.concatenate([jnp.where((q_lane >= h * HEAD_DIM) & (q_lane < (h + 1) * HEAD_DIM), q, zero)
                                for h in range(n_heads)], axis=0)

    def store_out(w):
        o_ref[pl.ds(pl.multiple_of(w * Q_BLOCK, Q_BLOCK), Q_BLOCK), :] = (
            oacc_ref[w].T.astype(o_ref.dtype))

    def scores(p, slot, i):
        w = jnp.minimum(2 * p + i, n_blocks - 1)
        k_win = kp_ref[pl.ds(pl.multiple_of(w * Q_BLOCK, Q_BLOCK), WINDOW), :]
        z_ref[slot, i] = _nt_dot(k_win, stacked_queries(w))

    def weights(p, slot, i):
        w = 2 * p + i
        for h in range(n_heads):
            z_view, s_view, a_view = z_ref.at[slot, i], s_ref.at[slot, i, h], a_ref.at[slot, i, h]
            tots = [_streams(z_view, h * Q_BLOCK, b * KEY_BLOCK, kqs[b], s_view)
                    for b in range(WINDOW_BLOCKS)]
            carry = None
            for b in range(WINDOW_BLOCKS - 1, -1, -1):
                excl, total = _suffix(tots[b], row)
                _apply_offsets(s_view, a_view, b * KEY_BLOCK, excl if carry is None else carry * excl)
                carry = total if carry is None else carry * total
            c_ref[w, h] = carry
        alive = jnp.max(jnp.maximum(c_ref[w, 0], c_ref[w, 1])) > 0.0
        flag_ref[w] = alive.astype(jnp.int32)

    def values(p, slot, i):
        w = jnp.maximum(2 * p + i, 0)
        k0 = pl.multiple_of(w * Q_BLOCK, Q_BLOCK)
        for h in range(n_heads):
            v_win = vp_ref[h * HEAD_DIM:(h + 1) * HEAD_DIM, pl.ds(k0, V_WINDOW)]
            oacc_ref[w, h * HEAD_DIM:(h + 1) * HEAD_DIM, :] = jnp.dot(v_win, a_ref[slot, i, h],
                                                                      preferred_element_type=F32)
        store_out(w)

    for i in range(2):
        scores(0, 0, i)
    a_ref[...] = jnp.zeros(a_ref.shape, a_ref.dtype)

    def two_pairs(m, _):
        p = 2 * m
        for p_w, slot in ((p, 0), (p + 1, 1)):
            for i in range(2):
                scores(p_w + 1, 1 - slot, i)
                weights(p_w, slot, i)
                values(p_w - 1, 1 - slot, i)
        return 0

    lax.fori_loop(0, n_blocks // 4, two_pairs, 0)
    for i in range(2):
        values(n_blocks // 2 - 1, 1, i)

    def older_keys(w):
        q2 = stacked_queries(w)
        n_sub = Q_BLOCK // KEY_BLOCK

        def cond(st):
            return jnp.logical_and(st[0] >= 1, st[1] > 0)

        def body(st):
            j = st[0]
            k0 = pl.multiple_of(j * Q_BLOCK, Q_BLOCK)
            z_ref[0, 0, 0:Q_BLOCK, :] = _nt_dot(kp_ref[pl.ds(k0, Q_BLOCK), :], q2)
            carries = []
            for h in range(n_heads):
                s_view, a_view = s_ref.at[0, 0, h], t_ref.at[h]
                tots = [_streams(z_ref.at[0, 0], h * Q_BLOCK, b * KEY_BLOCK, None, s_view) for b in range(n_sub)]
                carry = c_ref[w, h]
                for b in range(n_sub - 1, -1, -1):
                    excl, total = _suffix(tots[b], row)
                    _apply_offsets(s_view, a_view, b * KEY_BLOCK, carry * excl)
                    carry = carry * total
                v_blk = vp_ref[h * HEAD_DIM:(h + 1) * HEAD_DIM, pl.ds(k0, Q_BLOCK)]
                oacc_ref[w, h * HEAD_DIM:(h + 1) * HEAD_DIM, :] += jnp.dot(
                    v_blk, t_ref[h], preferred_element_type=F32)
                carries.append(carry)
                c_ref[w, h] = carry
            alive = jnp.max(functools.reduce(jnp.maximum, carries)) > 0.0
            return (j - 1, alive.astype(jnp.int32))

        lax.while_loop(cond, body, (w - 1, jnp.int32(1)))

    def finish(w, _):
        @pl.when(flag_ref[w] > 0)
        def _():
            older_keys(w)
            store_out(w)

        return 0

    lax.fori_loop(2, n_blocks, finish, 0)


def _attention(q, k_perm, vt_perm):
    bsz, n_groups, s, _ = q.shape
    n_heads = LANES // HEAD_DIM
    n_blocks = s // Q_BLOCK
    group_spec = pl.BlockSpec((None, None, s, LANES), lambda b, h: (b, h, 0, 0))
    return pl.pallas_call(
        _attn_kernel,
        grid=(bsz, n_groups),
        in_specs=[group_spec, group_spec, pl.BlockSpec((None, LANES, s), lambda b, h: (b, h, 0))],
        out_specs=group_spec,
        out_shape=jax.ShapeDtypeStruct(q.shape, BF16),
        scratch_shapes=[
            pltpu.VMEM((KEY_PAD + s, LANES), BF16),
            pltpu.VMEM((LANES, KEY_PAD + s + V_WINDOW - WINDOW), BF16),
            pltpu.VMEM((2, 2, WINDOW, n_heads * Q_BLOCK), F32),
            pltpu.VMEM((2, 2, n_heads, WINDOW, Q_BLOCK), F32),
            pltpu.VMEM((2, 2, n_heads, V_WINDOW, Q_BLOCK), BF16),
            pltpu.VMEM((n_heads, Q_BLOCK, Q_BLOCK), BF16),
            pltpu.VMEM((n_blocks, LANES, Q_BLOCK), F32),
            pltpu.VMEM((n_blocks, n_heads, SUBLANES, Q_BLOCK), F32),
            pltpu.SMEM((n_blocks,), jnp.int32),
        ],
        compiler_params=_params("arbitrary", "arbitrary"),
        name="stickbreak_attention",
    )(q, k_perm, vt_perm)


def _ssm_kernel(u_ref, bmat_ref, cmat_ref, lam_ref, d_ref, z_ref, xs_ref, st_ref, *, nb, row_chunk):
    rows = u_ref.shape[0]
    half = st_ref.shape[1] // 2

    @pl.when(pl.program_id(1) == 0)
    def _():
        st_ref[...] = jnp.zeros_like(st_ref)

    a_re = jnp.broadcast_to(lam_ref[0, 0:1, :], (nb, half))
    a_im = jnp.broadcast_to(lam_ref[0, 1:2, :], (nb, half))
    x_re, x_im = st_ref[:, :half], st_ref[:, half:]

    for r0 in range(0, rows, row_chunk):
        xs_ref[r0:r0 + row_chunk, :] = jnp.dot(u_ref[r0:r0 + row_chunk, :].astype(BF16), bmat_ref[0],
                                               preferred_element_type=F32)
        for r in range(r0, r0 + row_chunk, nb):
            n_re = a_re * x_re - a_im * x_im + xs_ref[r:r + nb, :half]
            n_im = a_re * x_im + a_im * x_re + xs_ref[r:r + nb, half:]
            xs_ref[r:r + nb, :half] = n_re
            xs_ref[r:r + nb, half:] = n_im
            x_re, x_im = n_re, n_im
        y = jnp.dot(xs_ref[r0:r0 + row_chunk, :].astype(BF16), cmat_ref[0], preferred_element_type=F32)
        z_ref[r0:r0 + row_chunk, :] = jax.nn.gelu(
            y + d_ref[...] * u_ref[r0:r0 + row_chunk, :].astype(F32)).astype(z_ref.dtype)

    st_ref[:, :half] = x_re
    st_ref[:, half:] = x_im


def _ssm(u_tm, bmat, cmat, lam, d_skip, nb):
    rows_total, d = u_tm.shape
    s = rows_total // nb
    tt = min(SSM_TIME_TILE, s)
    rows = tt * nb
    n_state = bmat.shape[-1]
    return pl.pallas_call(
        functools.partial(_ssm_kernel, nb=nb, row_chunk=min(512, rows)),
        grid=(d // SSM_CHUNK, s // tt),
        in_specs=[
            pl.BlockSpec((rows, SSM_CHUNK), lambda c, i: (i, c)),
            pl.BlockSpec((1, SSM_CHUNK, n_state), lambda c, i: (c, 0, 0)),
            pl.BlockSpec((1, n_state, SSM_CHUNK), lambda c, i: (c, 0, 0)),
            pl.BlockSpec((1, 2, n_state // 2), lambda c, i: (c, 0, 0)),
            pl.BlockSpec((1, SSM_CHUNK), lambda c, i: (0, c)),
        ],
        out_specs=pl.BlockSpec((rows, SSM_CHUNK), lambda c, i: (i, c)),
        out_shape=jax.ShapeDtypeStruct(u_tm.shape, BF16),
        scratch_shapes=[pltpu.VMEM((rows, n_state), F32), pltpu.VMEM((nb, n_state), F32)],
        compiler_params=_params("arbitrary", "arbitrary"),
        name="s5_scan",
    )(u_tm, bmat, cmat, lam, d_skip)


def _ssm_matrices(a_re, a_im, log_dt, b_re, b_im, c_re, c_im):
    g, p = a_re.shape
    hch = b_re.shape[-1]
    gpc = SSM_CHUNK // hch
    nc = g // gpc
    lam = lax.complex(a_re.astype(F32), a_im.astype(F32))
    dt = jnp.exp(log_dt.astype(F32))[:, None]
    lam_bar = jnp.exp(lam * dt)
    b_bar = ((lam_bar - 1) / lam)[..., None] * lax.complex(b_re.astype(F32), b_im.astype(F32))
    eye = jnp.eye(gpc, dtype=F32)

    def in_blocks(m):
        m = m.reshape(nc, gpc, p, hch).transpose(0, 1, 3, 2)
        return jnp.einsum('cghp,gk->cghkp', m, eye).reshape(nc, gpc * hch, gpc * p)

    def out_blocks(m):
        m = m.reshape(nc, gpc, hch, p).transpose(0, 1, 3, 2)
        return jnp.einsum('cgph,gk->cgpkh', m, eye).reshape(nc, gpc * p, gpc * hch)

    bmat = jnp.concatenate([in_blocks(jnp.real(b_bar)), in_blocks(jnp.imag(b_bar))], axis=-1)
    cmat = jnp.concatenate([out_blocks(c_re.astype(F32)), out_blocks(-c_im.astype(F32))], axis=1)
    lam_ri = jnp.stack([jnp.real(lam_bar).reshape(nc, gpc * p), jnp.imag(lam_bar).reshape(nc, gpc * p)], axis=1)
    return bmat.astype(BF16), cmat.astype(BF16), lam_ri


def _glu_kernel(z_ref, wg_ref, bg_ref, wo_ref, x_ref, mod_ref, o_ref, slab_ref, zb_ref):
    bsz, tt, d = x_ref.shape
    pitch = slab_ref.shape[1] // tt
    for c in range(d // LANES):
        for t in range(tt):
            slab_ref[c, t * pitch:t * pitch + bsz, :] = (
                z_ref[t * bsz:(t + 1) * bsz, c * LANES:(c + 1) * LANES].astype(F32))
        for b in range(bsz):
            zb_ref[b * tt:(b + 1) * tt, c * LANES:(c + 1) * LANES] = slab_ref[c, pl.ds(b, tt, stride=pitch), :]
    z = zb_ref[...]
    a = jnp.dot(z.astype(BF16), wg_ref[...], preferred_element_type=F32) + bg_ref[...]
    g = z * jax.nn.sigmoid(a)
    y = jnp.dot(g.astype(BF16), wo_ref[...], preferred_element_type=F32)
    for b in range(bsz):
        o_ref[b] = x_ref[b] + mod_ref[b, 2:3] * y[b * tt:(b + 1) * tt, :]


def _glu_residual(z_tm, w_glu, b_glu, w_o, x, mod):
    bsz, s, d = x.shape
    tt = min(TIME_TILE, s)
    pitch = _pitch(bsz)
    return pl.pallas_call(
        _glu_kernel,
        grid=(s // tt,),
        in_specs=[
            pl.BlockSpec((tt * bsz, d), lambda j: (j, 0)),
            _resident(w_glu.shape),
            _resident(b_glu.shape),
            _resident(w_o.shape),
            pl.BlockSpec((bsz, tt, d), lambda j: (0, j, 0)),
            _resident(mod.shape),
        ],
        out_specs=pl.BlockSpec((bsz, tt, d), lambda j: (0, j, 0)),
        out_shape=jax.ShapeDtypeStruct(x.shape, F32),
        scratch_shapes=[pltpu.VMEM((d // LANES, tt * pitch, LANES), F32), pltpu.VMEM((bsz * tt, d), F32)],
        compiler_params=_params("arbitrary"),
        name="glu_residual",
    )(z_tm, w_glu, b_glu, w_o, x, mod)


def _ffn_chunks(f):
    assert f % MXU_COLS == 0
    tiles = f // MXU_COLS
    n = min(FFN_CHUNKS, tiles)
    sizes = [(tiles // n + (1 if c < tiles % n else 0)) * MXU_COLS for c in range(n)]
    starts = [sum(sizes[:c]) for c in range(n)]
    return list(zip(starts, sizes))


def _ffn_kernel(x_ref, xh_ref, mod_ref, gain_ref, wup_ref, cw_ref, cb_ref, wdn_ref, *rest, mixer, final):
    if mixer:
        a_ref, ah_ref, wo_ref = rest[:3]
        rest = rest[3:]
    if final:
        fmod_ref, fgain_ref = rest[:2]
        rest = rest[2:]
    o_ref, hs_ref, acc_ref = rest[:3]
    u_refs = rest[3:]
    tm = x_ref.shape[1]
    f = wdn_ref.shape[0]
    m = mod_ref[0]
    gain = gain_ref[...]
    x, x_halo = x_ref[0], xh_ref[0]
    if mixer:
        a_cat = jnp.concatenate([_load_lane_groups(ah_ref), _load_lane_groups(a_ref)], axis=0)
        y_mix = m[2:3] * jnp.dot(a_cat, wo_ref[...], preferred_element_type=F32)
        x, x_halo = x + y_mix[HALO:], x_halo + y_mix[:HALO]
    h_halo = _modnorm(x_halo, gain, m[3:4], m[4:5])
    h_halo = jnp.where(pl.program_id(1) > 0, h_halo, 0.0)
    hs_ref[0:HALO, :] = h_halo.astype(BF16)
    hs_ref[HALO:, :] = _modnorm(x, gain, m[3:4], m[4:5]).astype(BF16)

    def conv(u_ref, c0, tf):
        w = cw_ref[:, c0:c0 + tf]
        return (w[2:3] * u_ref[pl.ds(HALO, tm), :] + w[1:2] * u_ref[pl.ds(HALO - 1, tm), :]
                + w[0:1] * u_ref[pl.ds(HALO - 2, tm), :] + cb_ref[:, c0:c0 + tf])

    for c, (g0, tf) in enumerate(_ffn_chunks(f)):
        ug_ref, uv_ref = u_refs[2 * c], u_refs[2 * c + 1]
        v0 = f + g0
        hs = hs_ref[...]
        ug_ref[...] = jnp.dot(hs, wup_ref[:, g0:g0 + tf], preferred_element_type=F32)
        uv_ref[...] = jnp.dot(hs, wup_ref[:, v0:v0 + tf], preferred_element_type=F32)
        act = (jax.nn.silu(conv(ug_ref, g0, tf)) * conv(uv_ref, v0, tf)).astype(BF16)
        y = jnp.dot(act, wdn_ref[g0:g0 + tf, :], preferred_element_type=F32)
        if c == 0:
            acc_ref[...] = y
        else:
            acc_ref[...] += y
    out = x + m[5:6] * acc_ref[...]
    if final:
        fm = fmod_ref[0]
        out = _modnorm(out, fgain_ref[...], fm[0:1], fm[1:2])
    o_ref[0] = out


def _conv_ffn(x, mod, gain, w_up, conv_w, conv_b, w_down, mixer=None, final=None):
    bsz, s, d = x.shape
    f = w_down.shape[0]
    tm = min(ROW_TILE, s)
    halo_blocks = tm // HALO
    tile_spec = pl.BlockSpec((1, tm, d), lambda b, j: (b, j, 0))
    halo_spec = pl.BlockSpec((1, HALO, d), lambda b, j: (b, jnp.maximum(j * halo_blocks - 1, 0), 0))
    in_specs = [
        tile_spec,
        halo_spec,
        pl.BlockSpec((1,) + mod.shape[1:], lambda b, j: (b, 0, 0)),
        _resident(gain.shape),
        _resident(w_up.shape),
        _resident(conv_w.shape),
        _resident(conv_b.shape),
        _resident(w_down.shape),
    ]
    args = [x, x, mod, gain, w_up, conv_w, conv_b, w_down]
    if mixer is not None:
        n_groups = mixer[0].shape[1]
        in_specs += [pl.BlockSpec((1, n_groups, tm, LANES), lambda b, j: (b, 0, j, 0)),
                     pl.BlockSpec((1, n_groups, HALO, LANES),
                                  lambda b, j: (b, 0, jnp.maximum(j * halo_blocks - 1, 0), 0)),
                     _resident(mixer[1].shape)]
        args += [mixer[0], mixer[0], mixer[1]]
    if final is not None:
        in_specs += [pl.BlockSpec((1,) + final[0].shape[1:], lambda b, j: (b, 0, 0)), _resident(final[1].shape)]
        args += list(final)
    u_scratch = [pltpu.VMEM((tm + HALO, tf), F32) for _, tf in _ffn_chunks(f) for _ in range(2)]
    return pl.pallas_call(
        functools.partial(_ffn_kernel, mixer=mixer is not None, final=final is not None),
        grid=(bsz, s // tm),
        in_specs=in_specs,
        out_specs=pl.BlockSpec((1, tm, d), lambda b, j: (b, j, 0)),
        out_shape=jax.ShapeDtypeStruct(x.shape, F32),
        scratch_shapes=[pltpu.VMEM((tm + HALO, d), BF16), pltpu.VMEM((tm, d), F32)] + u_scratch,
        compiler_params=_params("arbitrary", "arbitrary"),
        name="conv_ffn",
    )(*args)


def kernel(x, c, norm_mix, norm_ffn, w_mod, b_mod, w_qkv, w_o_attn, w_in_ssm, a_re, a_im, log_dt, b_re, b_im, c_re, c_im, d_skip, w_glu, b_glu, w_o_ssm, w_up, conv_w, conv_b, w_down, norm_out, w_fin, b_fin):
    bsz, s, d = x.shape
    depth = w_mod.shape[0]
    assert s % (4 * Q_BLOCK) == 0 and d % LANES == 0 and bsz % SUBLANES == 0

    mods = _mod_project(c, w_mod, b_mod).reshape(depth, bsz, 6, d)
    fin_mod = _mod_project(c, w_fin[None], b_fin[None]).reshape(bsz, 2, d)

    for i in range(depth):
        mod = mods[i]
        j = i // 2
        gain_mix = norm_mix[i][None]
        if i % 2 == 0:
            wq = (w_qkv[j][:, :d] * (-math.log2(math.e) * HEAD_DIM ** -0.5)).astype(BF16)
            wk = w_qkv[j][:, d:2 * d].astype(BF16)
            wvt = w_qkv[j][:, 2 * d:].T.astype(BF16)
            q, k_perm, vt_perm = _qkv_project(x, mod, gain_mix, wq, wk, wvt)
            mixer = (_attention(q, k_perm, vt_perm), w_o_attn[j].astype(BF16))
        else:
            mixer = None
            u_tm = _norm_project_time_major(x, mod, gain_mix, w_in_ssm[j].astype(BF16))
            bmat, cmat, lam = _ssm_matrices(a_re[j], a_im[j], log_dt[j], b_re[j], b_im[j], c_re[j], c_im[j])
            z_tm = _ssm(u_tm, bmat, cmat, lam, d_skip[j][None], nb=bsz)
            x = _glu_residual(z_tm, w_glu[j].astype(BF16), b_glu[j][None], w_o_ssm[j].astype(BF16), x, mod)
        final = (fin_mod, norm_out[None]) if i == depth - 1 else None
        x = _conv_ffn(x, mod, norm_ffn[i][None], w_up[i].astype(BF16), conv_w[i], conv_b[i][None],
                      w_down[i].astype(BF16), mixer=mixer, final=final)
    return x
```

```python
import functools
import math

import jax
import jax.numpy as jnp
from jax import lax
from jax.experimental import pallas as pl
from jax.experimental.pallas import tpu as pltpu

F32 = jnp.float32
BF16 = jnp.bfloat16

HEAD_DIM = 64
EPS = 1e-6

LANES = 128
SUBLANES = 8
MXU_COLS = 256
Q_BLOCK = LANES
KEY_BLOCK = 64
KEY_ROWS = KEY_BLOCK // SUBLANES
PAST_BLOCKS = 4
WINDOW_BLOCKS = PAST_BLOCKS + Q_BLOCK // KEY_BLOCK
KEY_PAD = PAST_BLOCKS * KEY_BLOCK
WINDOW = WINDOW_BLOCKS * KEY_BLOCK
V_WINDOW = -(-WINDOW // LANES) * LANES
HALO = 16
SSM_CHUNK = LANES
VMEM_LIMIT = 56 * 1024 * 1024
MASKED_SCORE = 1e30

ROW_TILE = 512
FFN_CHUNKS = 2
TIME_TILE = 32
SSM_TIME_TILE = 512
MOD_COL_TILE = 2048


def _params(*sem):
    return pltpu.CompilerParams(dimension_semantics=sem, vmem_limit_bytes=VMEM_LIMIT)


def _resident(shape):
    nd = len(shape)
    return pl.BlockSpec(shape, lambda *_: (0,) * nd, pipeline_mode=pl.Buffered(1))


def _modnorm(x, gain, shift, scale):
    ms = jnp.mean(x * x, axis=-1, keepdims=True)
    y = x * lax.rsqrt(ms + EPS) * gain
    return y * (1.0 + scale) + shift


def _mod_kernel(c_ref, w_ref, b_ref, o_ref):
    ca = jax.nn.silu(c_ref[...])
    o_ref[0] = jnp.dot(ca, w_ref[0], preferred_element_type=F32) + b_ref[0]


def _mod_project(c, w, b):
    nl, d, n = w.shape
    bsz = c.shape[0]
    tn = min(MOD_COL_TILE, n)
    return pl.pallas_call(
        _mod_kernel,
        grid=(nl, n // tn),
        in_specs=[
            pl.BlockSpec((bsz, d), lambda l, j: (0, 0)),
            pl.BlockSpec((1, d, tn), lambda l, j: (l, 0, j)),
            pl.BlockSpec((1, 1, tn), lambda l, j: (l, 0, j)),
        ],
        out_specs=pl.BlockSpec((1, bsz, tn), lambda l, j: (l, 0, j)),
        out_shape=jax.ShapeDtypeStruct((nl, bsz, n), F32),
        compiler_params=_params("arbitrary", "arbitrary"),
        name="adaln_project",
    )(c, w, b.reshape(nl, 1, n))


def _store_lane_groups(o_ref, y):
    for g in range(o_ref.shape[1]):
        o_ref[0, g] = y[:, g * LANES:(g + 1) * LANES].astype(o_ref.dtype)


def _load_lane_groups(a_ref):
    return jnp.concatenate([a_ref[0, g] for g in range(a_ref.shape[1])], axis=1)


def _qkv_kernel(x_ref, mod_ref, gain_ref, wq_ref, wk_ref, wvt_ref, q_ref, k_ref, vt_ref, h_ref, hp_ref):
    tm = x_ref.shape[1]
    m = mod_ref[0]
    h = _modnorm(x_ref[0], gain_ref[...], m[0:1], m[1:2])
    _store_lane_groups(q_ref, jnp.dot(h.astype(BF16), wq_ref[...], preferred_element_type=F32))
    for c in range(h_ref.shape[0]):
        h_ref[c] = h[:, c * LANES:(c + 1) * LANES]
        for blk in range(tm // KEY_BLOCK):
            for r2 in range(KEY_ROWS // 2):
                base = blk * KEY_BLOCK
                lo = h_ref[c, pl.ds(base + 2 * r2, SUBLANES, stride=KEY_ROWS), :]
                hi = h_ref[c, pl.ds(base + 2 * r2 + 1, SUBLANES, stride=KEY_ROWS), :]
                hp_ref[base + 2 * SUBLANES * r2:base + 2 * SUBLANES * (r2 + 1), c * LANES:(c + 1) * LANES] = (
                    jnp.concatenate([lo, hi], axis=0).astype(BF16))
    hp = hp_ref[...]
    _store_lane_groups(k_ref, jnp.dot(hp, wk_ref[...], preferred_element_type=F32))
    vt_ref[0] = lax.dot_general(wvt_ref[...], hp, (((1,), (1,)), ((), ())),
                                preferred_element_type=F32).astype(vt_ref.dtype)


def _qkv_project(x, mod, gain, wq, wk, wvt):
    bsz, s, d = x.shape
    tm = min(ROW_TILE, s)
    row_spec = pl.BlockSpec((1, tm, d), lambda b, j: (b, j, 0))
    group_spec = pl.BlockSpec((1, d // LANES, tm, LANES), lambda b, j: (b, 0, j, 0))
    return pl.pallas_call(
        _qkv_kernel,
        grid=(bsz, s // tm),
        in_specs=[
            row_spec,
            pl.BlockSpec((1,) + mod.shape[1:], lambda b, j: (b, 0, 0)),
            _resident(gain.shape),
            _resident(wq.shape),
            _resident(wk.shape),
            _resident(wvt.shape),
        ],
        out_specs=[group_spec, group_spec, pl.BlockSpec((1, d, tm), lambda b, j: (b, 0, j))],
        out_shape=[jax.ShapeDtypeStruct((bsz, d // LANES, s, LANES), BF16),
                   jax.ShapeDtypeStruct((bsz, d // LANES, s, LANES), BF16),
                   jax.ShapeDtypeStruct((bsz, d, s), BF16)],
        scratch_shapes=[pltpu.VMEM((d // LANES, tm, LANES), F32), pltpu.VMEM((tm, d), BF16)],
        compiler_params=_params("arbitrary", "arbitrary"),
        name="qkv_project",
    )(x, mod, gain, wq, wk, wvt)


def _pitch(rows):
    p = -(-rows // SUBLANES)
    return SUBLANES * (p if p % 2 else p + 1)


def _norm_matmul_kernel(x_ref, mod_ref, gain_ref, w_ref, o_ref, hs_ref, slab_ref):
    bsz, tt, d = x_ref.shape
    pitch = slab_ref.shape[1] // bsz
    for b in range(bsz):
        h = _modnorm(x_ref[b], gain_ref[...], mod_ref[b, 0:1], mod_ref[b, 1:2])
        hs_ref[b * tt:(b + 1) * tt, :] = h.astype(BF16)
    u = jnp.dot(hs_ref[...], w_ref[...], preferred_element_type=F32)
    for c in range(d // LANES):
        for b in range(bsz):
            slab_ref[c, b * pitch:b * pitch + tt, :] = u[b * tt:(b + 1) * tt, c * LANES:(c + 1) * LANES]
        for t in range(tt):
            o_ref[t * bsz:(t + 1) * bsz, c * LANES:(c + 1) * LANES] = slab_ref[c, pl.ds(t, bsz, stride=pitch), :]


def _norm_project_time_major(x, mod, gain, w):
    bsz, s, d = x.shape
    n = w.shape[1]
    tt = min(TIME_TILE, s)
    pitch = _pitch(tt)
    return pl.pallas_call(
        _norm_matmul_kernel,
        grid=(s // tt,),
        in_specs=[
            pl.BlockSpec((bsz, tt, d), lambda j: (0, j, 0)),
            _resident(mod.shape),
            _resident(gain.shape),
            _resident(w.shape),
        ],
        out_specs=pl.BlockSpec((tt * bsz, n), lambda j: (j, 0)),
        out_shape=jax.ShapeDtypeStruct((s * bsz, n), F32),
        scratch_shapes=[pltpu.VMEM((bsz * tt, d), BF16), pltpu.VMEM((n // LANES, bsz * pitch, LANES), F32)],
        compiler_params=_params("arbitrary"),
        name="norm_project",
    )(x, mod, gain, w)


def _streams(z_view, col, base, kq, s_view):
    run = jnp.ones((SUBLANES, Q_BLOCK), F32)
    for r in range(KEY_ROWS - 1, -1, -1):
        lo = base + SUBLANES * r
        zn = z_view[lo:lo + SUBLANES, col:col + Q_BLOCK]
        if kq is not None:
            zn = jnp.where(kq > r, zn, MASKED_SCORE)
        e = jnp.exp2(zn)
        a = run * (1.0 / (1.0 + e))
        s_view[lo:lo + SUBLANES, :] = a
        run = run - a
    return run


def _suffix(tot, row):
    p = tot
    for d in (1, 2, 4):
        p = jnp.where(row + d < SUBLANES, p * pltpu.roll(p, SUBLANES - d, 0), p)
    excl = jnp.where(row < SUBLANES - 1, pltpu.roll(p, SUBLANES - 1, 0), 1.0)
    return excl, jnp.broadcast_to(p[0:1, :], p.shape)


def _apply_offsets(s_view, a_view, base, off):
    off2 = jnp.concatenate([off, off], axis=0)
    for r2 in range(KEY_BLOCK // 16):
        lo = base + 16 * r2
        a_view[lo:lo + 16, :] = (s_view[lo:lo + 16, :] * off2).astype(BF16)


def _nt_dot(a, b):
    return lax.dot_general(a, b, (((1,), (1,)), ((), ())), preferred_element_type=F32)


def _attn_kernel(q_ref, k_ref, vt_ref, o_ref, kp_ref, vp_ref, z_ref, s_ref, a_ref, t_ref, oacc_ref, c_ref,
                 flag_ref):
    s_len = q_ref.shape[0]
    n_heads = LANES // HEAD_DIM
    row = lax.broadcasted_iota(jnp.int32, (SUBLANES, Q_BLOCK), 0)
    lane = lax.broadcasted_iota(jnp.int32, (SUBLANES, Q_BLOCK), 1)
    kqs = [lane - KEY_ROWS * row - (b - PAST_BLOCKS) * KEY_BLOCK if b >= PAST_BLOCKS else None
           for b in range(WINDOW_BLOCKS)]
    q_lane = lax.broadcasted_iota(jnp.int32, (Q_BLOCK, LANES), 1)

    n_blocks = s_len // Q_BLOCK

    kp_ref[0:KEY_PAD, :] = jnp.zeros((KEY_PAD, LANES), kp_ref.dtype)
    kp_ref[KEY_PAD:, :] = k_ref[...]
    vp_ref[:, 0:KEY_PAD] = jnp.zeros((LANES, KEY_PAD), vp_ref.dtype)
    vp_ref[:, KEY_PAD:KEY_PAD + s_len] = vt_ref[...]
    if V_WINDOW > WINDOW:
        vp_ref[:, KEY_PAD + s_len:] = jnp.zeros((LANES, V_WINDOW - WINDOW), vp_ref.dtype)

    def stacked_queries(w):
        q = q_ref[pl.ds(pl.multiple_of(w * Q_BLOCK, Q_BLOCK), Q_BLOCK), :]
        zero = jnp.zeros_like(q)
        return jnp.concatenate([jnp.where((q_lane >= h * HEAD_DIM) & (q_lane < (h + 1) * HEAD_DIM), q, zero)
                                for h in range(n_heads)], axis=0)

    def store_out(w):
        o_ref[pl.ds(pl.multiple_of(w * Q_BLOCK, Q_BLOCK), Q_BLOCK), :] = (
            oacc_ref[w].T.astype(o_ref.dtype))

    def scores(p, slot, i):
        w = jnp.minimum(2 * p + i, n_blocks - 1)
        k_win = kp_ref[pl.ds(pl.multiple_of(w * Q_BLOCK, Q_BLOCK), WINDOW), :]
        z_ref[slot, i] = _nt_dot(k_win, stacked_queries(w))

    def weights(p, slot, i):
        w = 2 * p + i
        for h in range(n_heads):
            z_view, s_view, a_view = z_ref.at[slot, i], s_ref.at[slot, i, h], a_ref.at[slot, i, h]
            tots = [_streams(z_view, h * Q_BLOCK, b * KEY_BLOCK, kqs[b], s_view)
                    for b in range(WINDOW_BLOCKS)]
            carry = None
            for b in range(WINDOW_BLOCKS - 1, -1, -1):
                excl, total = _suffix(tots[b], row)
                _apply_offsets(s_view, a_view, b * KEY_BLOCK, excl if carry is None else carry * excl)
                carry = total if carry is None else carry * total
            c_ref[w, h] = carry
        alive = jnp.max(jnp.maximum(c_ref[w, 0], c_ref[w, 1])) > 0.0
        flag_ref[w] = alive.astype(jnp.int32)

    def values(p, slot, i):
        w = jnp.maximum(2 * p + i, 0)
        k0 = pl.multiple_of(w * Q_BLOCK, Q_BLOCK)
        for h in range(n_heads):
            v_win = vp_ref[h * HEAD_DIM:(h + 1) * HEAD_DIM, pl.ds(k0, V_WINDOW)]
            oacc_ref[w, h * HEAD_DIM:(h + 1) * HEAD_DIM, :] = jnp.dot(v_win, a_ref[slot, i, h],
                                                                      preferred_element_type=F32)
        store_out(w)

    for i in range(2):
        scores(0, 0, i)
    a_ref[...] = jnp.zeros(a_ref.shape, a_ref.dtype)

    def two_pairs(m, _):
        p = 2 * m
        for p_w, slot in ((p, 0), (p + 1, 1)):
            for i in range(2):
                scores(p_w + 1, 1 - slot, i)
                weights(p_w, slot, i)
                values(p_w - 1, 1 - slot, i)
        return 0

    lax.fori_loop(0, n_blocks // 4, two_pairs, 0)
    for i in range(2):
        values(n_blocks // 2 - 1, 1, i)

    def older_keys(w):
        q2 = stacked_queries(w)
        n_sub = Q_BLOCK // KEY_BLOCK

        def cond(st):
            return jnp.logical_and(st[0] >= 1, st[1] > 0)

        def body(st):
            j = st[0]
            k0 = pl.multiple_of(j * Q_BLOCK, Q_BLOCK)
            z_ref[0, 0, 0:Q_BLOCK, :] = _nt_dot(kp_ref[pl.ds(k0, Q_BLOCK), :], q2)
            carries = []
            for h in range(n_heads):
                s_view, a_view = s_ref.at[0, 0, h], t_ref.at[h]
                tots = [_streams(z_ref.at[0, 0], h * Q_BLOCK, b * KEY_BLOCK, None, s_view) for b in range(n_sub)]
                carry = c_ref[w, h]
                for b in range(n_sub - 1, -1, -1):
                    excl, total = _suffix(tots[b], row)
                    _apply_offsets(s_view, a_view, b * KEY_BLOCK, carry * excl)
                    carry = carry * total
                v_blk = vp_ref[h * HEAD_DIM:(h + 1) * HEAD_DIM, pl.ds(k0, Q_BLOCK)]
                oacc_ref[w, h * HEAD_DIM:(h + 1) * HEAD_DIM, :] += jnp.dot(
                    v_blk, t_ref[h], preferred_element_type=F32)
                carries.append(carry)
                c_ref[w, h] = carry
            alive = jnp.max(functools.reduce(jnp.maximum, carries)) > 0.0
            return (j - 1, alive.astype(jnp.int32))

        lax.while_loop(cond, body, (w - 1, jnp.int32(1)))

    def finish(w, _):
        @pl.when(flag_ref[w] > 0)
        def _():
            older_keys(w)
            store_out(w)

        return 0

    lax.fori_loop(2, n_blocks, finish, 0)


def _attention(q, k_perm, vt_perm):
    bsz, n_groups, s, _ = q.shape
    n_heads = LANES // HEAD_DIM
    n_blocks = s // Q_BLOCK
    group_spec = pl.BlockSpec((None, None, s, LANES), lambda b, h: (b, h, 0, 0))
    return pl.pallas_call(
        _attn_kernel,
        grid=(bsz, n_groups),
        in_specs=[group_spec, group_spec, pl.BlockSpec((None, LANES, s), lambda b, h: (b, h, 0))],
        out_specs=group_spec,
        out_shape=jax.ShapeDtypeStruct(q.shape, BF16),
        scratch_shapes=[
            pltpu.VMEM((KEY_PAD + s, LANES), BF16),
            pltpu.VMEM((LANES, KEY_PAD + s + V_WINDOW - WINDOW), BF16),
            pltpu.VMEM((2, 2, WINDOW, n_heads * Q_BLOCK), F32),
            pltpu.VMEM((2, 2, n_heads, WINDOW, Q_BLOCK), F32),
            pltpu.VMEM((2, 2, n_heads, V_WINDOW, Q_BLOCK), BF16),
            pltpu.VMEM((n_heads, Q_BLOCK, Q_BLOCK), BF16),
            pltpu.VMEM((n_blocks, LANES, Q_BLOCK), F32),
            pltpu.VMEM((n_blocks, n_heads, SUBLANES, Q_BLOCK), F32),
            pltpu.SMEM((n_blocks,), jnp.int32),
        ],
        compiler_params=_params("arbitrary", "arbitrary"),
        name="stickbreak_attention",
    )(q, k_perm, vt_perm)


def _ssm_kernel(u_ref, bmat_ref, cmat_ref, lam_ref, d_ref, z_ref, xs_ref, st_ref, *, nb, row_chunk):
    rows = u_ref.shape[0]
    half = st_ref.shape[1] // 2

    @pl.when(pl.program_id(1) == 0)
    def _():
        st_ref[...] = jnp.zeros_like(st_ref)

    a_re = jnp.broadcast_to(lam_ref[0, 0:1, :], (nb, half))
    a_im = jnp.broadcast_to(lam_ref[0, 1:2, :], (nb, half))
    x_re, x_im = st_ref[:, :half], st_ref[:, half:]

    for r0 in range(0, rows, row_chunk):
        xs_ref[r0:r0 + row_chunk, :] = jnp.dot(u_ref[r0:r0 + row_chunk, :].astype(BF16), bmat_ref[0],
                                               preferred_element_type=F32)
        for r in range(r0, r0 + row_chunk, nb):
            n_re = a_re * x_re - a_im * x_im + xs_ref[r:r + nb, :half]
            n_im = a_re * x_im + a_im * x_re + xs_ref[r:r + nb, half:]
            xs_ref[r:r + nb, :half] = n_re
            xs_ref[r:r + nb, half:] = n_im
            x_re, x_im = n_re, n_im
        y = jnp.dot(xs_ref[r0:r0 + row_chunk, :].astype(BF16), cmat_ref[0], preferred_element_type=F32)
        z_ref[r0:r0 + row_chunk, :] = jax.nn.gelu(y + d_ref[...] * u_ref[r0:r0 + row_chunk, :])

    st_ref[:, :half] = x_re
    st_ref[:, half:] = x_im


def _ssm(u_tm, bmat, cmat, lam, d_skip, nb):
    rows_total, d = u_tm.shape
    s = rows_total // nb
    tt = min(SSM_TIME_TILE, s)
    rows = tt * nb
    n_state = bmat.shape[-1]
    return pl.pallas_call(
        functools.partial(_ssm_kernel, nb=nb, row_chunk=min(512, rows)),
        grid=(d // SSM_CHUNK, s // tt),
        in_specs=[
            pl.BlockSpec((rows, SSM_CHUNK), lambda c, i: (i, c)),
            pl.BlockSpec((1, SSM_CHUNK, n_state), lambda c, i: (c, 0, 0)),
            pl.BlockSpec((1, n_state, SSM_CHUNK), lambda c, i: (c, 0, 0)),
            pl.BlockSpec((1, 2, n_state // 2), lambda c, i: (c, 0, 0)),
            pl.BlockSpec((1, SSM_CHUNK), lambda c, i: (0, c)),
        ],
        out_specs=pl.BlockSpec((rows, SSM_CHUNK), lambda c, i: (i, c)),
        out_shape=jax.ShapeDtypeStruct(u_tm.shape, F32),
        scratch_shapes=[pltpu.VMEM((rows, n_state), F32), pltpu.VMEM((nb, n_state), F32)],
        compiler_params=_params("arbitrary", "arbitrary"),
        name="s5_scan",
    )(u_tm, bmat, cmat, lam, d_skip)


def _ssm_matrices(a_re, a_im, log_dt, b_re, b_im, c_re, c_im):
    g, p = a_re.shape
    hch = b_re.shape[-1]
    gpc = SSM_CHUNK // hch
    nc = g // gpc
    lam = lax.complex(a_re.astype(F32), a_im.astype(F32))
    dt = jnp.exp(log_dt.astype(F32))[:, None]
    lam_bar = jnp.exp(lam * dt)
    b_bar = ((lam_bar - 1) / lam)[..., None] * lax.complex(b_re.astype(F32), b_im.astype(F32))
    eye = jnp.eye(gpc, dtype=F32)

    def in_blocks(m):
        m = m.reshape(nc, gpc, p, hch).transpose(0, 1, 3, 2)
        return jnp.einsum('cghp,gk->cghkp', m, eye).reshape(nc, gpc * hch, gpc * p)

    def out_blocks(m):
        m = m.reshape(nc, gpc, hch, p).transpose(0, 1, 3, 2)
        return jnp.einsum('cgph,gk->cgpkh', m, eye).reshape(nc, gpc * p, gpc * hch)

    bmat = jnp.concatenate([in_blocks(jnp.real(b_bar)), in_blocks(jnp.imag(b_bar))], axis=-1)
    cmat = jnp.concatenate([out_blocks(c_re.astype(F32)), out_blocks(-c_im.astype(F32))], axis=1)
    lam_ri = jnp.stack([jnp.real(lam_bar).reshape(nc, gpc * p), jnp.imag(lam_bar).reshape(nc, gpc * p)], axis=1)
    return bmat.astype(BF16), cmat.astype(BF16), lam_ri


def _glu_kernel(z_ref, wg_ref, bg_ref, wo_ref, x_ref, mod_ref, o_ref, slab_ref, zb_ref):
    bsz, tt, d = x_ref.shape
    pitch = slab_ref.shape[1] // tt
    for c in range(d // LANES):
        for t in range(tt):
            slab_ref[c, t * pitch:t * pitch + bsz, :] = z_ref[t * bsz:(t + 1) * bsz, c * LANES:(c + 1) * LANES]
        for b in range(bsz):
            zb_ref[b * tt:(b + 1) * tt, c * LANES:(c + 1) * LANES] = slab_ref[c, pl.ds(b, tt, stride=pitch), :]
    z = zb_ref[...]
    a = jnp.dot(z.astype(BF16), wg_ref[...], preferred_element_type=F32) + bg_ref[...]
    g = z * jax.nn.sigmoid(a)
    y = jnp.dot(g.astype(BF16), wo_ref[...], preferred_element_type=F32)
    for b in range(bsz):
        o_ref[b] = x_ref[b] + mod_ref[b, 2:3] * y[b * tt:(b + 1) * tt, :]


def _glu_residual(z_tm, w_glu, b_glu, w_o, x, mod):
    bsz, s, d = x.shape
    tt = min(TIME_TILE, s)
    pitch = _pitch(bsz)
    return pl.pallas_call(
        _glu_kernel,
        grid=(s // tt,),
        in_specs=[
            pl.BlockSpec((tt * bsz, d), lambda j: (j, 0)),
            _resident(w_glu.shape),
            _resident(b_glu.shape),
            _resident(w_o.shape),
            pl.BlockSpec((bsz, tt, d), lambda j: (0, j, 0)),
            _resident(mod.shape),
        ],
        out_specs=pl.BlockSpec((bsz, tt, d), lambda j: (0, j, 0)),
        out_shape=jax.ShapeDtypeStruct(x.shape, F32),
        scratch_shapes=[pltpu.VMEM((d // LANES, tt * pitch, LANES), F32), pltpu.VMEM((bsz * tt, d), F32)],
        compiler_params=_params("arbitrary"),
        name="glu_residual",
    )(z_tm, w_glu, b_glu, w_o, x, mod)


def _ffn_chunks(f):
    assert f % MXU_COLS == 0
    tiles = f // MXU_COLS
    n = min(FFN_CHUNKS, tiles)
    sizes = [(tiles // n + (1 if c < tiles % n else 0)) * MXU_COLS for c in range(n)]
    starts = [sum(sizes[:c]) for c in range(n)]
    return list(zip(starts, sizes))


def _ffn_kernel(x_ref, xh_ref, mod_ref, gain_ref, wup_ref, cw_ref, cb_ref, wdn_ref, *rest, mixer, final):
    if mixer:
        a_ref, ah_ref, wo_ref = rest[:3]
        rest = rest[3:]
    if final:
        fmod_ref, fgain_ref = rest[:2]
        rest = rest[2:]
    o_ref, hs_ref, acc_ref = rest[:3]
    u_refs = rest[3:]
    tm = x_ref.shape[1]
    f = wdn_ref.shape[0]
    m = mod_ref[0]
    gain = gain_ref[...]
    x, x_halo = x_ref[0], xh_ref[0]
    if mixer:
        a_cat = jnp.concatenate([_load_lane_groups(ah_ref), _load_lane_groups(a_ref)], axis=0)
        y_mix = m[2:3] * jnp.dot(a_cat, wo_ref[...], preferred_element_type=F32)
        x, x_halo = x + y_mix[HALO:], x_halo + y_mix[:HALO]
    h_halo = _modnorm(x_halo, gain, m[3:4], m[4:5])
    h_halo = jnp.where(pl.program_id(1) > 0, h_halo, 0.0)
    hs_ref[0:HALO, :] = h_halo.astype(BF16)
    hs_ref[HALO:, :] = _modnorm(x, gain, m[3:4], m[4:5]).astype(BF16)

    def conv(u_ref, c0, tf):
        w = cw_ref[:, c0:c0 + tf]
        return (w[2:3] * u_ref[pl.ds(HALO, tm), :] + w[1:2] * u_ref[pl.ds(HALO - 1, tm), :]
                + w[0:1] * u_ref[pl.ds(HALO - 2, tm), :] + cb_ref[:, c0:c0 + tf])

    for c, (g0, tf) in enumerate(_ffn_chunks(f)):
        ug_ref, uv_ref = u_refs[2 * c], u_refs[2 * c + 1]
        v0 = f + g0
        hs = hs_ref[...]
        ug_ref[...] = jnp.dot(hs, wup_ref[:, g0:g0 + tf], preferred_element_type=F32)
        uv_ref[...] = jnp.dot(hs, wup_ref[:, v0:v0 + tf], preferred_element_type=F32)
        act = (jax.nn.silu(conv(ug_ref, g0, tf)) * conv(uv_ref, v0, tf)).astype(BF16)
        y = jnp.dot(act, wdn_ref[g0:g0 + tf, :], preferred_element_type=F32)
        if c == 0:
            acc_ref[...] = y
        else:
            acc_ref[...] += y
    out = x + m[5:6] * acc_ref[...]
    if final:
        fm = fmod_ref[0]
        out = _modnorm(out, fgain_ref[...], fm[0:1], fm[1:2])
    o_ref[0] = out


def _conv_ffn(x, mod, gain, w_up, conv_w, conv_b, w_down, mixer=None, final=None):
    bsz, s, d = x.shape
    f = w_down.shape[0]
    tm = min(ROW_TILE, s)
    halo_blocks = tm // HALO
    tile_spec = pl.BlockSpec((1, tm, d), lambda b, j: (b, j, 0))
    halo_spec = pl.BlockSpec((1, HALO, d), lambda b, j: (b, jnp.maximum(j * halo_blocks - 1, 0), 0))
    in_specs = [
        tile_spec,
        halo_spec,
        pl.BlockSpec((1,) + mod.shape[1:], lambda b, j: (b, 0, 0)),
        _resident(gain.shape),
        _resident(w_up.shape),
        _resident(conv_w.shape),
        _resident(conv_b.shape),
        _resident(w_down.shape),
    ]
    args = [x, x, mod, gain, w_up, conv_w, conv_b, w_down]
    if mixer is not None:
        n_groups = mixer[0].shape[1]
        in_specs += [pl.BlockSpec((1, n_groups, tm, LANES), lambda b, j: (b, 0, j, 0)),
                     pl.BlockSpec((1, n_groups, HALO, LANES),
                                  lambda b, j: (b, 0, jnp.maximum(j * halo_blocks - 1, 0), 0)),
                     _resident(mixer[1].shape)]
        args += [mixer[0], mixer[0], mixer[1]]
    if final is not None:
        in_specs += [pl.BlockSpec((1,) + final[0].shape[1:], lambda b, j: (b, 0, 0)), _resident(final[1].shape)]
        args += list(final)
    u_scratch = [pltpu.VMEM((tm + HALO, tf), F32) for _, tf in _ffn_chunks(f) for _ in range(2)]
    return pl.pallas_call(
        functools.partial(_ffn_kernel, mixer=mixer is not None, final=final is not None),
        grid=(bsz, s // tm),
        in_specs=in_specs,
        out_specs=pl.BlockSpec((1, tm, d), lambda b, j: (b, j, 0)),
        out_shape=jax.ShapeDtypeStruct(x.shape, F32),
        scratch_shapes=[pltpu.VMEM((tm + HALO, d), BF16), pltpu.VMEM((tm, d), F32)] + u_scratch,
        compiler_params=_params("arbitrary", "arbitrary"),
        name="conv_ffn",
    )(*args)


def kernel(x, c, norm_mix, norm_ffn, w_mod, b_mod, w_qkv, w_o_attn, w_in_ssm, a_re, a_im, log_dt, b_re, b_im, c_re, c_im, d_skip, w_glu, b_glu, w_o_ssm, w_up, conv_w, conv_b, w_down, norm_out, w_fin, b_fin):
    bsz, s, d = x.shape
    depth = w_mod.shape[0]
    assert s % (4 * Q_BLOCK) == 0 and d % LANES == 0 and bsz % SUBLANES == 0

    mods = _mod_project(c, w_mod, b_mod).reshape(depth, bsz, 6, d)
    fin_mod = _mod_project(c, w_fin[None], b_fin[None]).reshape(bsz, 2, d)

    for i in range(depth):
        mod = mods[i]
        j = i // 2
        gain_mix = norm_mix[i][None]
        if i % 2 == 0:
            wq = (w_qkv[j][:, :d] * (-math.log2(math.e) * HEAD_DIM ** -0.5)).astype(BF16)
            wk = w_qkv[j][:, d:2 * d].astype(BF16)
            wvt = w_qkv[j][:, 2 * d:].T.astype(BF16)
            q, k_perm, vt_perm = _qkv_project(x, mod, gain_mix, wq, wk, wvt)
            mixer = (_attention(q, k_perm, vt_perm), w_o_attn[j].astype(BF16))
        else:
            mixer = None
            u_tm = _norm_project_time_major(x, mod, gain_mix, w_in_ssm[j].astype(BF16))
            bmat, cmat, lam = _ssm_matrices(a_re[j], a_im[j], log_dt[j], b_re[j], b_im[j], c_re[j], c_im[j])
            z_tm = _ssm(u_tm, bmat, cmat, lam, d_skip[j][None], nb=bsz)
            x = _glu_residual(z_tm, w_glu[j].astype(BF16), b_glu[j][None], w_o_ssm[j].astype(BF16), x, mod)
        final = (fin_mod, norm_out[None]) if i == depth - 1 else None
        x = _conv_ffn(x, mod, norm_ffn[i][None], w_up[i].astype(BF16), conv_w[i], conv_b[i][None],
                      w_down[i].astype(BF16), mixer=mixer, final=final)
    return x
```
